```python
import math
import jax, jax.numpy as jnp
from jax import lax
import numpy as np

D_MODEL = 2048
BATCH = 16
SEQ = 2048
DEPTH = 2
DEC_BATCH = 16
DEC_SEQ = 64
PAST_LEN = 4096

CHUNK = 64
N_A_LAYERS = DEPTH // 2
N_B_LAYERS = DEPTH - N_A_LAYERS
SSM_D_INNER = 2 * D_MODEL
SSM_HEAD_DIM = 64
SSM_HEADS = SSM_D_INNER // SSM_HEAD_DIM
SSM_GROUPS = 8
SSM_D_STATE = 128
CONV_W = 4
CONV_DIM = SSM_D_INNER + 2 * SSM_GROUPS * SSM_D_STATE
SSM_IN_PROJ = SSM_D_INNER + CONV_DIM + SSM_HEADS
WINDOW = 128
ATT_HEAD_DIM = 64
ATT_HEADS = D_MODEL // ATT_HEAD_DIM
ATT_KV_HEADS = ATT_HEADS // 8
ATT_GROUP = ATT_HEADS // ATT_KV_HEADS
MOE_GROUPS = 4
MOE_PER_GROUP = 4
MOE_EXPERTS = MOE_GROUPS * MOE_PER_GROUP
MOE_TOP_K = 2
MOE_FF = D_MODEL // 4
N_MOD = 6 * DEPTH + 4
EPS = 1e-6

kernel_name = 'yoco_ssd_swa_sink_hmoe_stream_step'


def _rmsnorm(x, g):
    xf = x.astype(jnp.float32)
    xf = xf * lax.rsqrt(jnp.mean(xf * xf, axis=-1, keepdims=True) + EPS)
    return (xf * g.astype(jnp.float32)).astype(x.dtype)


def _modulate(xn, shift, scale):
    return xn * (1 + scale) + shift


def _alibi_slopes(n):
    return jnp.asarray(2.0 ** (-8.0 * np.arange(1, n + 1) / n), dtype=jnp.float32)


def _causal_dwconv(xp, w, bias):
    y = lax.conv_general_dilated(xp, w[:, None, :], window_strides=(1,), padding='VALID',
                                 dimension_numbers=('NWC', 'WIO', 'NWC'),
                                 feature_group_count=xp.shape[-1])
    return y + bias


def _ssd(x, dt, a, bm, cm, h0):
    f32 = jnp.float32
    b, L, H, P = x.shape
    G, N = bm.shape[2], bm.shape[3]
    R = H // G
    k = min(CHUNK, L)
    nc = L // k

    def to_chunks(t):
        return t.reshape((b, nc, k) + t.shape[2:]).swapaxes(0, 1)

    xs = (to_chunks(x.astype(f32).reshape(b, L, G, R, P)), to_chunks(dt.astype(f32).reshape(b, L, G, R)),
          to_chunks(bm.astype(f32)), to_chunks(cm.astype(f32)))
    ar = a.astype(f32).reshape(G, R)
    causal = jnp.tril(jnp.ones((k, k), dtype=bool))[None, :, :, None, None]

    def step(state, inp):
        xc, dtc, bc, cc = inp
        acs = jnp.cumsum(dtc * ar, axis=1)
        seg = acs[:, :, None] - acs[:, None, :]
        decay = jnp.exp(jnp.where(causal, seg, -jnp.inf))
        cb = jnp.einsum('btgn,bsgn->btsg', cc, bc)
        y_intra = jnp.einsum('btsgr,bsgrp->btgrp', cb[..., None] * decay * dtc[:, None], xc)
        y_inter = jnp.einsum('btgn,bgrpn->btgrp', cc, state) * jnp.exp(acs)[..., None]
        w_end = dtc * jnp.exp(acs[:, -1:] - acs)
        new_state = (state * jnp.exp(acs[:, -1])[..., None, None]
                     + jnp.einsum('bsgn,bsgr,bsgrp->bgrpn', bc, w_end, xc))
        return new_state, y_intra + y_inter

    h_last, ys = lax.scan(step, h0.astype(f32).reshape(b, G, R, P, N), xs)
    y = ys.swapaxes(0, 1).reshape(b, L, H, P)
    return y, h_last.reshape(b, H, P, N)


def _mamba2(u, conv_prev, ssm_prev, w_in, conv_w, conv_b, dt_bias, a_log, d_skip, norm_g, w_out):
    b, L, _ = u.shape
    zxbcdt = u @ w_in
    z = zxbcdt[..., :SSM_D_INNER]
    xbc = zxbcdt[..., SSM_D_INNER:SSM_D_INNER + CONV_DIM]
    dt_raw = zxbcdt[..., SSM_D_INNER + CONV_DIM:]
    xbc_full = jnp.concatenate([conv_prev.astype(xbc.dtype), xbc], axis=1)
    conv_new = xbc_full[:, -(CONV_W - 1):]
    xbc = jax.nn.silu(_causal_dwconv(xbc_full, conv_w, conv_b))
    gn = SSM_GROUPS * SSM_D_STATE
    xs = xbc[..., :SSM_D_INNER].reshape(b, L, SSM_HEADS, SSM_HEAD_DIM)
    bm = xbc[..., SSM_D_INNER:SSM_D_INNER + gn].reshape(b, L, SSM_GROUPS, SSM_D_STATE)
    cm = xbc[..., SSM_D_INNER + gn:].reshape(b, L, SSM_GROUPS, SSM_D_STATE)
    dt = jax.nn.softplus(dt_raw.astype(jnp.float32) + dt_bias.astype(jnp.float32))
    a = -jnp.exp(a_log.astype(jnp.float32))
    y, ssm_new = _ssd(xs, dt, a, bm, cm, ssm_prev)
    y = y + d_skip.astype(jnp.float32)[:, None] * xs.astype(jnp.float32)
    y = y.reshape(b, L, SSM_D_INNER) * jax.nn.silu(z.astype(jnp.float32))
    y = _rmsnorm(y, norm_g).astype(u.dtype)
    return y @ w_out, conv_new, ssm_new.astype(u.dtype)


def _dist(t, s):
    return jnp.abs(jnp.arange(t)[:, None] + WINDOW - jnp.arange(s)[None, :]).astype(jnp.float32)


def _sink_attention(q, k, v, dist, valid, slopes, sinks):
    s = jnp.einsum('btgrd,bsgd->bgrts', q, k).astype(jnp.float32) * (ATT_HEAD_DIM ** -0.5)
    s = s - slopes.reshape(ATT_KV_HEADS, ATT_GROUP)[:, :, None, None] * dist
    s = jnp.where(valid, s, -jnp.inf)
    sink = sinks.astype(jnp.float32).reshape(ATT_KV_HEADS, ATT_GROUP)[:, :, None, None]
    m = jnp.maximum(jnp.max(s, axis=-1, keepdims=True), sink)
    p = jnp.exp(s - m)
    probs = p / (jnp.sum(p, axis=-1, keepdims=True) + jnp.exp(sink - m))
    return jnp.einsum('bgrts,bsgd->btgrd', probs.astype(v.dtype), v)


def _swa_prompt(q, k, v, slopes, sinks):
    b, L = q.shape[0], q.shape[1]
    nc = L // CHUNK
    pad = jnp.zeros((b, WINDOW) + k.shape[2:], k.dtype)
    k_pad = jnp.concatenate([pad, k], axis=1)
    v_pad = jnp.concatenate([pad, v], axis=1)
    qc = q.reshape((b, nc, CHUNK) + q.shape[2:]).swapaxes(0, 1)
    dist = _dist(CHUNK, CHUNK + WINDOW)
    j = jnp.arange(CHUNK + WINDOW)

    def block(args):
        c, qb = args
        kb = lax.dynamic_slice_in_dim(k_pad, c * CHUNK, CHUNK + WINDOW, axis=1)
        vb = lax.dynamic_slice_in_dim(v_pad, c * CHUNK, CHUNK + WINDOW, axis=1)
        valid = jnp.broadcast_to((c * CHUNK - WINDOW + j >= 0)[None, :], (CHUNK, CHUNK + WINDOW))
        return _sink_attention(qb, kb, vb, dist, valid, slopes, sinks)

    o = lax.map(block, (jnp.arange(nc), qc))
    return o.swapaxes(0, 1).reshape(q.shape)


def _swa_sample(q, k_all, v_all, slopes, sinks):
    t, s = q.shape[1], k_all.shape[1]
    valid = jnp.broadcast_to((PAST_LEN - WINDOW + jnp.arange(s) >= 0)[None, :], (t, s))
    return _sink_attention(q, k_all, v_all, _dist(t, s), valid, slopes, sinks)


def _hmoe(x, router_g, bias_g, router_e, bias_e, w_gate, w_up, w_down):
    shp = x.shape
    xt = x.reshape(-1, shp[-1])
    lg = (xt @ router_g + bias_g).astype(jnp.float32)
    pg = jax.nn.softmax(lg, axis=-1)
    _, gi = lax.top_k(lg, 1)
    p_sel = jnp.take_along_axis(pg, gi, axis=-1)
    le = (xt @ router_e + bias_e).astype(jnp.float32).reshape(-1, MOE_GROUPS, MOE_PER_GROUP)
    le_g = jnp.take_along_axis(le, gi[:, :, None], axis=1)[:, 0]
    tv, ti = lax.top_k(le_g, MOE_TOP_K)
    w = jax.nn.softmax(tv, axis=-1) * p_sel
    eid = gi * MOE_PER_GROUP + ti
    gates = jnp.sum(jax.nn.one_hot(eid, MOE_EXPERTS, dtype=jnp.float32) * w[..., None], axis=1)

    def expert(acc, inp):
        wg_e, wu_e, wd_e, g_e = inp
        hid = jax.nn.silu(xt @ wg_e) * (xt @ wu_e)
        return acc + g_e[:, None].astype(xt.dtype) * (hid @ wd_e), None

    acc, _ = lax.scan(expert, jnp.zeros_like(xt), (w_gate, w_up, w_down, gates.T))
    return acc.reshape(shp)


def _trunk(x, c, conv_st, ssm_st, past_k, past_v, p):
    b, L, _ = x.shape
    mods = (jax.nn.silu(c) @ p['ada_w'] + p['ada_b']).reshape(b, N_MOD, D_MODEL)

    def mod(i):
        return mods[:, i, None, :]

    slopes = _alibi_slopes(ATT_HEADS)
    h = x
    new_conv, new_ssm = [], []
    k_all = v_all = k_state = v_state = None
    for i in range(DEPTH):
        hn = _modulate(_rmsnorm(h, p['norm_mix'][i]), mod(6 * i), mod(6 * i + 1))
        if i < N_A_LAYERS:
            out, cs, ss = _mamba2(hn, conv_st[i], ssm_st[i], p['ssm_w_in'][i], p['ssm_conv_w'][i],
                                  p['ssm_conv_b'][i], p['ssm_dt_bias'][i], p['ssm_a_log'][i],
                                  p['ssm_d'][i], p['ssm_norm'][i], p['ssm_w_out'][i])
            new_conv.append(cs)
            new_ssm.append(ss)
        else:
            j = i - N_A_LAYERS
            if j == 0:
                kvn = _modulate(_rmsnorm(h, p['norm_kv']), mod(6 * DEPTH), mod(6 * DEPTH + 1))
                kv = (kvn @ p['attn_w_kv']).reshape(b, L, 2, ATT_KV_HEADS, ATT_HEAD_DIM)
                k_new, v_new = kv[:, :, 0], kv[:, :, 1]
                if past_k is None:
                    k_all, v_all = k_new, v_new
                else:
                    k_all = jnp.concatenate([past_k.astype(k_new.dtype), k_new], axis=1)
                    v_all = jnp.concatenate([past_v.astype(v_new.dtype), v_new], axis=1)
                k_state, v_state = k_all[:, -WINDOW:], v_all[:, -WINDOW:]
            q = (hn @ p['attn_w_q'][j]).reshape(b, L, ATT_KV_HEADS, ATT_GROUP, ATT_HEAD_DIM)
            if past_k is None:
                o = _swa_prompt(q, k_all, v_all, slopes, p['attn_sinks'][j])
            else:
                o = _swa_sample(q, k_all, v_all, slopes, p['attn_sinks'][j])
            out = o.reshape(b, L, ATT_HEADS * ATT_HEAD_DIM) @ p['attn_w_o'][j]
        h = h + mod(6 * i + 2) * out
        hn = _modulate(_rmsnorm(h, p['norm_ffn'][i]), mod(6 * i + 3), mod(6 * i + 4))
        h = h + mod(6 * i + 5) * _hmoe(hn, p['moe_router_g'][i], p['moe_bias_g'][i], p['moe_router_e'][i],
                                       p['moe_bias_e'][i], p['moe_w_gate'][i], p['moe_w_up'][i],
                                       p['moe_w_down'][i])
    y = _modulate(_rmsnorm(h, p['norm_out']), mod(6 * DEPTH + 2), mod(6 * DEPTH + 3))
    return y, jnp.stack(new_conv), jnp.stack(new_ssm), k_state, v_state


def setup_inputs(seed: int = 0) -> dict:
    key = jax.random.key(seed)
    ks = jax.random.split(key, 40)
    f32 = jnp.float32
    D = D_MODEL
    NA, NB = N_A_LAYERS, N_B_LAYERS

    def nrm(i, shape, scale=1.0):
        return jax.random.normal(ks[i], shape, f32) * scale

    dt0 = jnp.exp(jax.random.uniform(ks[20], (NA, SSM_HEADS), f32, math.log(1e-3), math.log(1e-1)))
    att_w = ATT_HEADS * ATT_HEAD_DIM
    return {
        'x_prompt': nrm(0, (BATCH, SEQ, D)),
        'x_sample': nrm(1, (DEC_BATCH, DEC_SEQ, D)),
        'state_conv': nrm(2, (NA, DEC_BATCH, CONV_W - 1, CONV_DIM)),
        'state_ssm': nrm(3, (NA, DEC_BATCH, SSM_HEADS, SSM_HEAD_DIM, SSM_D_STATE), 0.1),
        'cache_k': nrm(4, (DEC_BATCH, WINDOW, ATT_KV_HEADS, ATT_HEAD_DIM)),
        'cache_v': nrm(5, (DEC_BATCH, WINDOW, ATT_KV_HEADS, ATT_HEAD_DIM)),
        'c_prompt': nrm(6, (BATCH, D)),
        'c_sample': nrm(7, (DEC_BATCH, D)),
        'ada_w': nrm(8, (D, N_MOD * D), 0.5 * D ** -0.5),
        'ada_b': nrm(9, (N_MOD * D,), 0.01),
        'norm_mix': 1.0 + nrm(10, (DEPTH, D), 0.05),
        'norm_ffn': 1.0 + nrm(11, (DEPTH, D), 0.05),
        'norm_kv': 1.0 + nrm(12, (D,), 0.05),
        'norm_out': 1.0 + nrm(13, (D,), 0.05),
        'ssm_w_in': nrm(14, (NA, D, SSM_IN_PROJ), D ** -0.5),
        'ssm_conv_w': nrm(15, (NA, CONV_W, CONV_DIM), CONV_W ** -0.5),
        'ssm_conv_b': nrm(16, (NA, CONV_DIM), 0.01),
        'ssm_dt_bias': dt0 + jnp.log(-jnp.expm1(-dt0)),
        'ssm_a_log': jnp.log(jax.random.uniform(ks[17], (NA, SSM_HEADS), f32, 1.0, 16.0)),
        'ssm_d': 1.0 + nrm(18, (NA, SSM_HEADS), 0.1),
        'ssm_norm': 1.0 + nrm(19, (NA, SSM_D_INNER), 0.05),
        'ssm_w_out': nrm(21, (NA, SSM_D_INNER, D), SSM_D_INNER ** -0.5),
        'attn_w_kv': nrm(22, (D, 2 * ATT_KV_HEADS * ATT_HEAD_DIM), D ** -0.5),
        'attn_w_q': nrm(23, (NB, D, att_w), D ** -0.5),
        'attn_sinks': nrm(24, (NB, ATT_HEADS), 0.5),
        'attn_w_o': nrm(25, (NB, att_w, D), att_w ** -0.5),
        'moe_router_g': nrm(26, (DEPTH, D, MOE_GROUPS), D ** -0.5),
        'moe_bias_g': nrm(27, (DEPTH, MOE_GROUPS), 0.01),
        'moe_router_e': nrm(28, (DEPTH, D, MOE_EXPERTS), D ** -0.5),
        'moe_bias_e': nrm(29, (DEPTH, MOE_EXPERTS), 0.01),
        'moe_w_gate': nrm(30, (DEPTH, MOE_EXPERTS, D, MOE_FF), D ** -0.5),
        'moe_w_up': nrm(31, (DEPTH, MOE_EXPERTS, D, MOE_FF), D ** -0.5),
        'moe_w_down': nrm(32, (DEPTH, MOE_EXPERTS, MOE_FF, D), MOE_FF ** -0.5),
    }


def reference(x_prompt, x_sample, state_conv, state_ssm, cache_k, cache_v, c_prompt, c_sample,
              ada_w, ada_b, norm_mix, norm_ffn, norm_kv, norm_out,
              ssm_w_in, ssm_conv_w, ssm_conv_b, ssm_dt_bias, ssm_a_log, ssm_d, ssm_norm, ssm_w_out,
              attn_w_kv, attn_w_q, attn_sinks, attn_w_o,
              moe_router_g, moe_bias_g, moe_router_e, moe_bias_e, moe_w_gate, moe_w_up, moe_w_down):
    p = dict(ada_w=ada_w, ada_b=ada_b, norm_mix=norm_mix, norm_ffn=norm_ffn, norm_kv=norm_kv,
             norm_out=norm_out, ssm_w_in=ssm_w_in, ssm_conv_w=ssm_conv_w, ssm_conv_b=ssm_conv_b,
             ssm_dt_bias=ssm_dt_bias, ssm_a_log=ssm_a_log, ssm_d=ssm_d, ssm_norm=ssm_norm,
             ssm_w_out=ssm_w_out, attn_w_kv=attn_w_kv, attn_w_q=attn_w_q, attn_sinks=attn_sinks,
             attn_w_o=attn_w_o, moe_router_g=moe_router_g, moe_bias_g=moe_bias_g,
             moe_router_e=moe_router_e, moe_bias_e=moe_bias_e, moe_w_gate=moe_w_gate,
             moe_w_up=moe_w_up, moe_w_down=moe_w_down)
    bp = x_prompt.shape[0]
    zero_conv = jnp.zeros((N_A_LAYERS, bp, CONV_W - 1, CONV_DIM), x_prompt.dtype)
    zero_ssm = jnp.zeros((N_A_LAYERS, bp, SSM_HEADS, SSM_HEAD_DIM, SSM_D_STATE), x_prompt.dtype)
    y_prompt, conv_p, ssm_p, k_p, v_p = _trunk(x_prompt, c_prompt, zero_conv, zero_ssm, None, None, p)
    y_sample, conv_s, ssm_s, k_s, v_s = _trunk(x_sample, c_sample, state_conv, state_ssm, cache_k, cache_v, p)
    return (y_prompt, y_sample, conv_p, ssm_p, k_p, v_p, conv_s, ssm_s, k_s, v_s)
```

```python
import functools

import numpy as np
import jax
import jax.numpy as jnp
from jax import lax
from jax.experimental import pallas as pl
from jax.experimental.pallas import tpu as pltpu

F32 = jnp.float32
BF16 = jnp.bfloat16
I32 = jnp.int32
EPS = 1e-6
ROW_BLOCK = 64
WINDOW_BLOCKS = 2
LANES = 128
MIB = 1024 * 1024

SSM_HEAD_DIM = 64
SSM_GROUPS = 8
SSM_D_STATE = 128
CONV_W = 4
CONV_PAD = 8
ATT_HEAD_DIM = 64
ATT_GROUP = 8
MOE_GROUPS = 4
MOE_PER_GROUP = 4
MOE_PAIR_LO = (0, 0, 0, 1, 1, 2)
MOE_PAIR_HI = (1, 2, 3, 2, 3, 3)
MOE_PAIRS = len(MOE_PAIR_LO)
MOE_BUCKETS = MOE_GROUPS * MOE_PAIRS
MOE_TILE = 256
GATHER_ROWS = 256


def _pick_tile(n, want):
    t = min(want, n)
    t -= t % ROW_BLOCK
    while n % t:
        t -= ROW_BLOCK
    return t


def _silu(x):
    return x * (1.0 / (1.0 + jnp.exp(-x)))


def _softplus(x):
    return jnp.maximum(x, 0.0) + jnp.log1p(jnp.exp(-jnp.abs(x)))


def _rms_mod(hv, g, shift, scale):
    ms = jnp.mean(hv * hv, axis=-1, keepdims=True)
    return (hv * lax.rsqrt(ms + EPS) * g) * (1.0 + scale) + shift


def _dot(a, b):
    return jnp.dot(a, b, preferred_element_type=F32)


def _nt_dot(a, b):
    return lax.dot_general(a, b, (((1,), (1,)), ((), ())), preferred_element_type=F32)


def _split_bf16(x, parts):
    out = []
    r = x
    for _ in range(parts):
        p = r.astype(BF16)
        out.append(p)
        r = r - p.astype(F32)
    return out


def _row_block(s):
    return pl.ds(pl.multiple_of(s * ROW_BLOCK, ROW_BLOCK), ROW_BLOCK)


def _mods_kernel(c_ref, w_ref, b_ref, o_ref):
    a = _silu(c_ref[...]).astype(BF16)
    o_ref[...] = _dot(a, w_ref[...].astype(BF16)) + b_ref[...]


def _mods(c_all, ada_w, ada_b):
    C, D = c_all.shape
    N = ada_w.shape[1]
    tn = 1024
    return pl.pallas_call(
        _mods_kernel,
        out_shape=jax.ShapeDtypeStruct((C, N), F32),
        grid=(N // tn,),
        in_specs=[pl.BlockSpec((C, D), lambda j: (0, 0)),
                  pl.BlockSpec((D, tn), lambda j: (0, j)),
                  pl.BlockSpec((1, tn), lambda j: (0, j))],
        out_specs=pl.BlockSpec((C, tn), lambda j: (0, j)),
        compiler_params=pltpu.CompilerParams(dimension_semantics=("parallel",), vmem_limit_bytes=40 * MIB),
        name="ada_mods",
    )(c_all, ada_w, ada_b.reshape(1, N))


def _nmm_kernel(*refs, nsb, has_res):
    if has_res:
        b2c_ref, h_ref, res_ref, gate_ref, shift_ref, scale_ref, g_ref, w_ref, o_ref, hnew_ref, xn_ref = refs
    else:
        b2c_ref, h_ref, shift_ref, scale_ref, g_ref, w_ref, o_ref, xn_ref = refs
    i = pl.program_id(0)

    @pl.when(pl.program_id(1) == 0)
    def _():
        def body(s, carry):
            c = b2c_ref[i * nsb + s]
            rows = _row_block(s)
            hv = h_ref[rows, :]
            if has_res:
                hv = hv + gate_ref[c] * res_ref[rows, :]
                hnew_ref[rows, :] = hv
            xn_ref[rows, :] = _rms_mod(hv, g_ref[...], shift_ref[c], scale_ref[c]).astype(BF16)
            return carry
        lax.fori_loop(0, nsb, body, 0)

    o_ref[...] = _dot(xn_ref[...], w_ref[...]).astype(o_ref.dtype)


def _norm_matmul(b2c, h, g, shift, scale, w, *, out_dtype, tn, tm_want, res=None, gate=None):
    T, D = h.shape
    N = w.shape[1]
    C = shift.shape[0]
    tm = _pick_tile(T, tm_want)
    nsb = tm // ROW_BLOCK
    has_res = res is not None
    row_spec = pl.BlockSpec((tm, D), lambda i, j, b: (i, 0))
    mod_spec = pl.BlockSpec((C, 1, D), lambda i, j, b: (0, 0, 0))
    in_specs = [row_spec]
    args = [h]
    if has_res:
        in_specs += [row_spec, mod_spec]
        args += [res, gate]
    in_specs += [mod_spec, mod_spec, pl.BlockSpec((1, D), lambda i, j, b: (0, 0)),
                 pl.BlockSpec((D, tn), lambda i, j, b: (0, j))]
    args += [shift, scale, g.reshape(1, D), w]
    out_shape = [jax.ShapeDtypeStruct((T, N), out_dtype)]
    out_specs = [pl.BlockSpec((tm, tn), lambda i, j, b: (i, j))]
    if has_res:
        out_shape.append(jax.ShapeDtypeStruct((T, D), F32))
        out_specs.append(row_spec)
    n_row_bufs = 3 if has_res else 1
    vmem = (2 * n_row_bufs * tm * D * 4 + tm * D * 2 + 2 * D * tn * 2
            + 2 * tm * tn * jnp.dtype(out_dtype).itemsize + 8 * MIB)
    outs = pl.pallas_call(
        functools.partial(_nmm_kernel, nsb=nsb, has_res=has_res),
        out_shape=out_shape,
        grid_spec=pltpu.PrefetchScalarGridSpec(
            num_scalar_prefetch=1, grid=(T // tm, N // tn), in_specs=in_specs, out_specs=out_specs,
            scratch_shapes=[pltpu.VMEM((tm, D), BF16)]),
        compiler_params=pltpu.CompilerParams(dimension_semantics=("parallel", "arbitrary"),
                                             vmem_limit_bytes=int(vmem)),
        name="norm_matmul_res" if has_res else "norm_matmul",
    )(b2c, *args)
    return outs if has_res else outs[0]


def _mmres_kernel(b2c_ref, a_ref, w_ref, h_ref, gate_ref, o_ref, acc_ref, *, nsb):
    i = pl.program_id(0)
    acc_ref[...] = _dot(a_ref[...], w_ref[...])

    def body(s, carry):
        c = b2c_ref[i * nsb + s]
        rows = _row_block(s)
        o_ref[rows, :] = h_ref[rows, :] + gate_ref[c] * acc_ref[rows, :]
        return carry
    lax.fori_loop(0, nsb, body, 0)


def _matmul_residual(b2c, a, w, h, gate, *, tn, tm_want):
    T, K = a.shape
    D = w.shape[1]
    C = gate.shape[0]
    tm = _pick_tile(T, tm_want)
    nsb = tm // ROW_BLOCK
    vmem = 2 * tm * K * 2 + 2 * K * tn * 2 + 5 * tm * tn * 4 + 8 * MIB
    return pl.pallas_call(
        functools.partial(_mmres_kernel, nsb=nsb),
        out_shape=jax.ShapeDtypeStruct((T, D), F32),
        grid_spec=pltpu.PrefetchScalarGridSpec(
            num_scalar_prefetch=1, grid=(T // tm, D // tn),
            in_specs=[pl.BlockSpec((tm, K), lambda i, j, b: (i, 0)),
                      pl.BlockSpec((K, tn), lambda i, j, b: (0, j)),
                      pl.BlockSpec((tm, tn), lambda i, j, b: (i, j)),
                      pl.BlockSpec((C, 1, tn), lambda i, j, b: (0, 0, j))],
            out_specs=pl.BlockSpec((tm, tn), lambda i, j, b: (i, j)),
            scratch_shapes=[pltpu.VMEM((tm, tn), F32)]),
        compiler_params=pltpu.CompilerParams(dimension_semantics=("parallel", "arbitrary"),
                                             vmem_limit_bytes=int(vmem)),
        name="matmul_residual",
    )(b2c, a, w, h, gate)


def _final_kernel(b2c_ref, h_ref, res_ref, gate_ref, shift_ref, scale_ref, g_ref, o_ref, *, nsb, blk_off):
    i = pl.program_id(0) + blk_off

    def body(s, carry):
        c = b2c_ref[i * nsb + s]
        rows = _row_block(s)
        hv = h_ref[rows, :] + gate_ref[c] * res_ref[rows, :]
        o_ref[rows, :] = _rms_mod(hv, g_ref[...], shift_ref[c], scale_ref[c])
        return carry
    lax.fori_loop(0, nsb, body, 0)


def _final_norm(b2c, h, res, gate, shift, scale, g, *, row_off, n_rows, tm_want):
    T, D = h.shape
    C = shift.shape[0]
    tm = _pick_tile(int(np.gcd(row_off, n_rows)) if row_off else n_rows, tm_want)
    nsb = tm // ROW_BLOCK
    blk_off = row_off // tm
    row_spec = pl.BlockSpec((tm, D), lambda i, b: (i + blk_off, 0))
    mod_spec = pl.BlockSpec((C, 1, D), lambda i, b: (0, 0, 0))
    return pl.pallas_call(
        functools.partial(_final_kernel, nsb=nsb, blk_off=blk_off),
        out_shape=jax.ShapeDtypeStruct((n_rows, D), F32),
        grid_spec=pltpu.PrefetchScalarGridSpec(
            num_scalar_prefetch=1, grid=(n_rows // tm,),
            in_specs=[row_spec, row_spec, mod_spec, mod_spec, mod_spec,
                      pl.BlockSpec((1, D), lambda i, b: (0, 0))],
            out_specs=pl.BlockSpec((tm, D), lambda i, b: (i, 0))),
        compiler_params=pltpu.CompilerParams(dimension_semantics=("parallel",),
                                             vmem_limit_bytes=int(6 * tm * D * 4 + 8 * MIB)),
        name="final_norm",
    )(b2c, h, res, gate, shift, scale, g.reshape(1, D))


def _ssd_kernel(seq_ref, first_ref, last_ref,
                z_ref, x_ref, b_ref, c_ref, dt_ref,
                cpx_ref, cpb_ref, cpc_ref, sprev_ref,
                cwx_ref, cwb_ref, cwc_ref, cbx_ref, cbb_ref, cbc_ref,
                dtb_ref, a_ref, dsk_ref, ng_ref, expand_ref,
                y_ref, cnew_ref, snew_ref,
                extx_ref, extb_ref, extc_ref, st_ref, xs_ref, bm_ref, cm_ref, col_ref, rowa_ref, rowd_ref,
                ybuf_ref):
    Q = ROW_BLOCK
    N = SSM_D_STATE
    HP = x_ref.shape[1]
    GN = b_ref.shape[1]
    G = GN // N
    GW = HP // G
    PAIRS_PER_GROUP = GW // LANES
    k = pl.program_id(0)

    @pl.when(first_ref[k] == 1)
    def _():
        extx_ref[0:CONV_PAD, :] = cpx_ref[0]
        extb_ref[0:CONV_PAD, :] = cpb_ref[0]
        extc_ref[0:CONV_PAD, :] = cpc_ref[0]
        st_ref[...] = sprev_ref[0]

    def conv(ext_ref, cur_ref, w_ref, bias_ref):
        ext_ref[CONV_PAD:CONV_PAD + Q, :] = cur_ref[...].astype(F32)
        acc = bias_ref[...] + w_ref[CONV_W - 1:CONV_W, :] * ext_ref[CONV_PAD:CONV_PAD + Q, :]
        for j in range(CONV_W - 1):
            off = CONV_PAD - (CONV_W - 1) + j
            acc = acc + w_ref[j:j + 1, :] * ext_ref[off:off + Q, :]
        return _silu(acc)

    xs_ref[...] = conv(extx_ref, x_ref, cwx_ref, cbx_ref)
    bm_ref[...] = conv(extb_ref, b_ref, cwb_ref, cbb_ref).astype(BF16)
    cm_ref[...] = conv(extc_ref, c_ref, cwc_ref, cbc_ref).astype(BF16)
    for ext_ref, lo, hi in ((extx_ref, 0, HP), (extb_ref, HP, HP + GN), (extc_ref, HP + GN, HP + 2 * GN)):
        tail = ext_ref[Q:Q + CONV_PAD, :]
        cnew_ref[0, :, lo:hi] = tail
        ext_ref[0:CONV_PAD, :] = tail

    dt = _softplus(dt_ref[...].astype(F32) + dtb_ref[...])
    da = dt * a_ref[...]
    rr = lax.broadcasted_iota(I32, (Q, Q), 0)
    cc = lax.broadcasted_iota(I32, (Q, Q), 1)
    tri = jnp.where(rr >= cc, 1.0, 0.0).astype(BF16)
    cs = _dot(tri, jnp.concatenate(_split_bf16(da, 3), axis=1))
    acs = cs[:, 0:LANES] + cs[:, LANES:2 * LANES] + cs[:, 2 * LANES:3 * LANES]

    both = _split_bf16(jnp.concatenate([acs, dt], axis=0), 2)
    col_ref[...] = _dot(both[0], expand_ref[...]) + _dot(both[1], expand_ref[...])

    pr = lax.broadcasted_iota(I32, (LANES // 2, 2 * LANES), 0)
    pk = lax.broadcasted_iota(I32, (LANES // 2, 2 * LANES), 1)
    esel = jnp.where(pk == jnp.where(pk < LANES, 2 * pr, 2 * pr + 1 + LANES), 1.0, 0.0).astype(BF16)
    zero = jnp.zeros((Q, LANES), BF16)

    def pair_rows(v, parts):
        out = None
        for piece in _split_bf16(v, parts):
            vbd = jnp.concatenate([jnp.concatenate([piece, zero], axis=1),
                                   jnp.concatenate([zero, piece], axis=1)], axis=0)
            r = _nt_dot(esel, vbd)
            out = r if out is None else out + r
        return out

    rowa_ref[...] = pair_rows(acs, 3)
    rowd_ref[...] = pair_rows(dt, 2)

    lane = lax.broadcasted_iota(I32, (Q, LANES), 1)
    trow = lax.broadcasted_iota(I32, (Q, LANES), 0)
    causal = trow >= (lane & (Q - 1))
    lo_half = lane < Q
    er = lax.broadcasted_iota(I32, (N, N), 0)
    ec = lax.broadcasted_iota(I32, (N, N), 1)
    eye = jnp.where(er == ec, 1.0, 0.0).astype(BF16)

    for g in range(G):
        gs = slice(g * GW, (g + 1) * GW)
        bg = bm_ref[:, g * N:(g + 1) * N]
        cg = cm_ref[:, g * N:(g + 1) * N]
        cb2 = _nt_dot(cg, jnp.concatenate([bg, bg], axis=0))
        y_state = _dot(cg, st_ref[:, gs].astype(BF16))
        for pp in range(PAIRS_PER_GROUP):
            p = g * PAIRS_PER_GROUP + pp
            ps = slice(p * LANES, (p + 1) * LANES)
            col = col_ref[0:Q, ps]
            seg = col - rowa_ref[p:p + 1, :]
            decay = jnp.exp(jnp.where(causal, seg, -jnp.inf))
            lmat = (cb2 * decay * rowd_ref[p:p + 1, :]).astype(BF16)
            xp = xs_ref[:, ps]
            xbd = jnp.concatenate([jnp.where(lo_half, xp, 0.0), jnp.where(lo_half, 0.0, xp)],
                                  axis=0).astype(BF16)
            y_in = _dot(lmat, xbd)
            ybuf_ref[:, ps] = (y_in + y_state[:, pp * LANES:(pp + 1) * LANES] * jnp.exp(col)
                               + dsk_ref[:, ps] * xp)
        colg = col_ref[0:Q, gs]
        last = col_ref[Q - 1:Q, gs]
        w_end = col_ref[Q:2 * Q, gs] * jnp.exp(last - colg)
        xw = (xs_ref[:, gs] * w_end).astype(BF16)
        bgt = _nt_dot(eye, bg).astype(BF16)
        st_ref[:, gs] = st_ref[:, gs] * jnp.exp(last) + _dot(bgt, xw)

    yv = ybuf_ref[...] * _silu(z_ref[...].astype(F32))
    ms = jnp.mean(yv * yv, axis=-1, keepdims=True)
    y_ref[...] = (yv * lax.rsqrt(ms + EPS) * ng_ref[...]).astype(BF16)

    @pl.when(last_ref[k] == 1)
    def _():
        snew_ref[0] = st_ref[...]


def _ssd(tabs, zx, cprev, sprev, conv_w, conv_b, dt_bias, a_neg, d_skip, norm_g, expand):
    T = zx.shape[0]
    S, N, HP = sprev.shape
    GN = SSM_GROUPS * N
    Q = ROW_BLOCK
    nblk = T // Q
    bb, cb = HP // GN, HP // GN + 1
    zb = (2 * HP) // GN

    def rows(width, col_blk):
        return pl.BlockSpec((Q, width), lambda k, s, f, l: (k, col_blk))

    def per_seq(shape, col_blk):
        return pl.BlockSpec((1,) + shape, lambda k, s, f, l: (s[k], 0, col_blk))

    def const(shape, col_blk=0):
        return pl.BlockSpec(shape, lambda k, s, f, l: (0, col_blk))

    in_specs = [
        rows(HP, 0), rows(HP, 1), rows(GN, zb), rows(GN, zb + 1), rows(LANES, (2 * HP + 2 * GN) // LANES),
        per_seq((CONV_PAD, HP), 0), per_seq((CONV_PAD, GN), bb), per_seq((CONV_PAD, GN), cb),
        per_seq((N, HP), 0),
        const((CONV_W, HP)), const((CONV_W, GN), bb), const((CONV_W, GN), cb),
        const((1, HP)), const((1, GN), bb), const((1, GN), cb),
        const((1, LANES)), const((1, LANES)), const((1, HP)), const((1, HP)), const((LANES, HP)),
    ]
    out_shape = [jax.ShapeDtypeStruct((T, HP), BF16),
                 jax.ShapeDtypeStruct((S, CONV_PAD, HP + 2 * GN), F32),
                 jax.ShapeDtypeStruct((S, N, HP), F32)]
    out_specs = [pl.BlockSpec((Q, HP), lambda k, s, f, l: (k, 0)),
                 pl.BlockSpec((1, CONV_PAD, HP + 2 * GN), lambda k, s, f, l: (s[k], 0, 0)),
                 pl.BlockSpec((1, N, HP), lambda k, s, f, l: (s[k], 0, 0))]
    scratch = [pltpu.VMEM((Q + CONV_PAD, HP), F32), pltpu.VMEM((Q + CONV_PAD, GN), F32),
               pltpu.VMEM((Q + CONV_PAD, GN), F32), pltpu.VMEM((N, HP), F32), pltpu.VMEM((Q, HP), F32),
               pltpu.VMEM((Q, GN), BF16), pltpu.VMEM((Q, GN), BF16), pltpu.VMEM((2 * Q, HP), F32),
               pltpu.VMEM((LANES // 2, 2 * Q), F32), pltpu.VMEM((LANES // 2, 2 * Q), F32),
               pltpu.VMEM((Q, HP), F32)]
    return pl.pallas_call(
        _ssd_kernel,
        out_shape=out_shape,
        grid_spec=pltpu.PrefetchScalarGridSpec(num_scalar_prefetch=3, grid=(nblk,), in_specs=in_specs,
                                               out_specs=out_specs, scratch_shapes=scratch),
        compiler_params=pltpu.CompilerParams(dimension_semantics=("arbitrary",), vmem_limit_bytes=48 * MIB),
        name="ssd_mixer",
    )(tabs["seq"], tabs["first"], tabs["last"],
      zx, zx, zx, zx, zx, cprev, cprev, cprev, sprev,
      conv_w, conv_w, conv_w, conv_b, conv_b, conv_b, dt_bias, a_neg, d_skip, norm_g, expand)


def _attn_kernel(ext_ref, ninv_ref, sink_ref, q_ref, w0_ref, w1_ref, w2_ref, o_ref, *, slopes):
    Q = ROW_BLOCK
    NK = (WINDOW_BLOCKS + 1) * Q
    DH = ATT_HEAD_DIM
    KVW = w0_ref.shape[1] // 2
    n_kv = KVW // DH
    k = pl.program_id(0)
    t = lax.broadcasted_iota(I32, (Q, NK), 0)
    s = lax.broadcasted_iota(I32, (Q, NK), 1)
    dist = jnp.abs(t + WINDOW_BLOCKS * Q - s).astype(F32)
    mask_bias = jnp.where(s >= ninv_ref[k] * Q, 0.0, -jnp.inf)
    wins = (w0_ref, w1_ref, w2_ref)
    for g in range(n_kv):
        kg = jnp.concatenate([w[:, g * DH:(g + 1) * DH] for w in wins], axis=0).astype(BF16)
        vg = jnp.concatenate([w[:, KVW + g * DH:KVW + (g + 1) * DH] for w in wins], axis=0).astype(BF16)
        for rp in range(ATT_GROUP // 2):
            outs = []
            for r in (2 * rp, 2 * rp + 1):
                h = g * ATT_GROUP + r
                sc = _nt_dot(q_ref[:, h * DH:(h + 1) * DH], kg) * (DH ** -0.5) - slopes[h] * dist + mask_bias
                sink = sink_ref[h]
                m = jnp.maximum(jnp.max(sc, axis=-1, keepdims=True), sink)
                p = jnp.exp(sc - m)
                den = jnp.sum(p, axis=-1, keepdims=True) + jnp.exp(sink - m)
                outs.append(_dot(p.astype(BF16), vg) * (1.0 / den))
            h0 = g * ATT_GROUP + 2 * rp
            o_ref[:, h0 * DH:(h0 + 2) * DH] = jnp.concatenate(outs, axis=1).astype(BF16)


def _attention(tabs, q, kv_ext, sinks):
    T, HD = q.shape
    Q = ROW_BLOCK
    n_heads = HD // ATT_HEAD_DIM
    slopes = tuple(float(2.0 ** (-8.0 * (i + 1) / n_heads)) for i in range(n_heads))
    KV2 = kv_ext.shape[1]

    def win(j):
        return pl.BlockSpec((Q, KV2), lambda k, e, n: (e[k] - WINDOW_BLOCKS + j, 0))

    return pl.pallas_call(
        functools.partial(_attn_kernel, slopes=slopes),
        out_shape=jax.ShapeDtypeStruct((T, HD), BF16),
        grid_spec=pltpu.PrefetchScalarGridSpec(
            num_scalar_prefetch=2, grid=(T // Q,),
            in_specs=[pl.BlockSpec(memory_space=pltpu.SMEM),
                      pl.BlockSpec((Q, HD), lambda k, e, n: (k, 0)), win(0), win(1), win(2)],
            out_specs=pl.BlockSpec((Q, HD), lambda k, e, n: (k, 0))),
        compiler_params=pltpu.CompilerParams(dimension_semantics=("parallel",), vmem_limit_bytes=32 * MIB),
        name="swa_attention",
    )(tabs["ext"], tabs["ninv"], sinks, q, kv_ext, kv_ext, kv_ext)


def _router_kernel(b2c_ref, h_ref, shift_ref, scale_ref, g_ref, wh_ref, wl_ref, rb_ref, hn_ref, meta_ref,
                   xh_ref, xl_ref, *, nsb):
    i = pl.program_id(0)

    def body(s, carry):
        c = b2c_ref[i * nsb + s]
        rows = _row_block(s)
        xn = _rms_mod(h_ref[rows, :], g_ref[...], shift_ref[c], scale_ref[c])
        hn_ref[rows, :] = xn
        hi, lo = _split_bf16(xn, 2)
        xh_ref[rows, :] = hi
        xl_ref[rows, :] = lo
        return carry
    lax.fori_loop(0, nsb, body, 0)

    lt = (_nt_dot(wh_ref[...], xh_ref[...]) + _nt_dot(wh_ref[...], xl_ref[...])
          + _nt_dot(wl_ref[...], xh_ref[...]) + rb_ref[...])

    def first_max(vals):
        m = vals[0]
        for v in vals[1:]:
            m = jnp.maximum(m, v)
        idx = jnp.full(m.shape, len(vals) - 1, I32)
        for j in range(len(vals) - 2, -1, -1):
            idx = jnp.where(vals[j] == m, j, idx)
        return m, idx

    lg = [lt[j:j + 1, :] for j in range(MOE_GROUPS)]
    mg, gi = first_max(lg)
    p_sel = 1.0 / sum(jnp.exp(v - mg) for v in lg)
    le = []
    for j in range(MOE_PER_GROUP):
        v = lt[MOE_GROUPS + (MOE_GROUPS - 1) * MOE_PER_GROUP + j:MOE_GROUPS + (MOE_GROUPS - 1) * MOE_PER_GROUP + j + 1, :]
        for grp in range(MOE_GROUPS - 2, -1, -1):
            row = MOE_GROUPS + grp * MOE_PER_GROUP + j
            v = jnp.where(gi == grp, lt[row:row + 1, :], v)
        le.append(v)
    m1, i1 = first_max(le)
    m2, i2 = first_max([jnp.where(i1 == j, -jnp.inf, le[j]) for j in range(MOE_PER_GROUP)])
    e2 = jnp.exp(m2 - m1)
    w1 = p_sel * (1.0 / (1.0 + e2))
    w2 = p_sel * (e2 / (1.0 + e2))
    lo_i = jnp.minimum(i1, i2)
    hi_i = jnp.maximum(i1, i2)
    first_is_lo = i1 < i2
    pair_base = jnp.where(lo_i == 0, 0, jnp.where(lo_i == 1, 3, 5))
    bucket = gi * MOE_PAIRS + pair_base + hi_i - lo_i - 1
    meta_ref[0:1, :] = bucket.astype(F32)
    meta_ref[1:2, :] = jnp.where(first_is_lo, w1, w2)
    meta_ref[2:3, :] = jnp.where(first_is_lo, w2, w1)
    meta_ref[3:8, :] = jnp.zeros((5, meta_ref.shape[1]), F32)


def _router(b2c, h, g, shift, scale, w_hi, w_lo, rbias, *, tm_want):
    T, D = h.shape
    C = shift.shape[0]
    tm = _pick_tile(T, tm_want)
    nsb = tm // ROW_BLOCK
    row_spec = pl.BlockSpec((tm, D), lambda i, b: (i, 0))
    mod_spec = pl.BlockSpec((C, 1, D), lambda i, b: (0, 0, 0))
    rb = jnp.broadcast_to(rbias.reshape(LANES, 1), (LANES, tm))
    return pl.pallas_call(
        functools.partial(_router_kernel, nsb=nsb),
        out_shape=[jax.ShapeDtypeStruct((T, D), F32), jax.ShapeDtypeStruct((8, T), F32)],
        grid_spec=pltpu.PrefetchScalarGridSpec(
            num_scalar_prefetch=1, grid=(T // tm,),
            in_specs=[row_spec, mod_spec, mod_spec, pl.BlockSpec((1, D), lambda i, b: (0, 0)),
                      pl.BlockSpec((LANES, D), lambda i, b: (0, 0)), pl.BlockSpec((LANES, D), lambda i, b: (0, 0)),
                      pl.BlockSpec((LANES, tm), lambda i, b: (0, 0))],
            out_specs=[row_spec, pl.BlockSpec((8, tm), lambda i, b: (0, i))],
            scratch_shapes=[pltpu.VMEM((tm, D), BF16), pltpu.VMEM((tm, D), BF16)]),
        compiler_params=pltpu.CompilerParams(dimension_semantics=("parallel",),
                                             vmem_limit_bytes=int(5 * tm * D * 4 + 12 * MIB)),
        name="moe_router",
    )(b2c, h, shift, scale, g.reshape(1, D), w_hi, w_lo, rb)


def _gather_kernel(idx_ref, src_ref, dst_ref, sem, *, rows_per_step):
    i = pl.program_id(0)
    slot = lax.rem(i, 2)

    def row_copy(src_row, dst_row, sl):
        return pltpu.make_async_copy(src_ref.at[pl.ds(src_row, 1), :], dst_ref.at[pl.ds(dst_row, 1), :], sem.at[sl])

    def start(r, carry):
        row_copy(idx_ref[0, 0, r], i * rows_per_step + r, slot).start()
        return carry
    lax.fori_loop(0, rows_per_step, start, 0)

    def wait_all(sl):
        def wait(r, carry):
            row_copy(0, 0, sl).wait()
            return carry
        lax.fori_loop(0, rows_per_step, wait, 0)

    @pl.when(i > 0)
    def _():
        wait_all(1 - slot)

    @pl.when(i == pl.num_programs(0) - 1)
    def _():
        wait_all(slot)


def _gather_rows(src, idx):
    n = idx.shape[0]
    D = src.shape[1]
    R = _pick_tile(n, GATHER_ROWS)
    return pl.pallas_call(
        functools.partial(_gather_kernel, rows_per_step=R),
        out_shape=jax.ShapeDtypeStruct((n, D), src.dtype),
        grid=(n // R,),
        in_specs=[pl.BlockSpec((1, 1, R), lambda i: (i, 0, 0), memory_space=pltpu.SMEM),
                  pl.BlockSpec(memory_space=pl.ANY)],
        out_specs=pl.BlockSpec(memory_space=pl.ANY),
        scratch_shapes=[pltpu.SemaphoreType.DMA((2,))],
        compiler_params=pltpu.CompilerParams(dimension_semantics=("arbitrary",)),
        name="row_gather",
    )(idx.reshape(n // R, 1, R), src)


def _ffn_kernel(ea_ref, eb_ref, nv_ref, x_ref, wab_ref, wgua_ref, wgub_ref, wda_ref, wdb_ref, y_ref):
    i = pl.program_id(0)
    F = wda_ref.shape[1]

    @pl.when(nv_ref[i] > 0)
    def _():
        x = x_ref[...].astype(BF16)

        def expert(wgu_ref, wd_ref):
            gu = _dot(x, wgu_ref[0])
            hid = _silu(gu[:, :F]) * gu[:, F:]
            return _dot(hid.astype(BF16), wd_ref[0])
        y_ref[...] = (wab_ref[:, 0:1] * expert(wgua_ref, wda_ref)
                      + wab_ref[:, 1:2] * expert(wgub_ref, wdb_ref))

    @pl.when(nv_ref[i] == 0)
    def _():
        y_ref[...] = jnp.zeros(y_ref.shape, F32)


def _moe_ffn(plan, xs, w_gu, w_d):
    R, D = xs.shape
    tm = MOE_TILE
    E, F, _ = w_d.shape
    return pl.pallas_call(
        _ffn_kernel,
        out_shape=jax.ShapeDtypeStruct((R, D), F32),
        grid_spec=pltpu.PrefetchScalarGridSpec(
            num_scalar_prefetch=3, grid=(R // tm,),
            in_specs=[pl.BlockSpec((tm, D), lambda i, a, b, n: (i, 0)),
                      pl.BlockSpec((tm, LANES), lambda i, a, b, n: (i, 0)),
                      pl.BlockSpec((1, D, 2 * F), lambda i, a, b, n: (a[i], 0, 0)),
                      pl.BlockSpec((1, D, 2 * F), lambda i, a, b, n: (b[i], 0, 0)),
                      pl.BlockSpec((1, F, D), lambda i, a, b, n: (a[i], 0, 0)),
                      pl.BlockSpec((1, F, D), lambda i, a, b, n: (b[i], 0, 0))],
            out_specs=pl.BlockSpec((tm, D), lambda i, a, b, n: (i, 0))),
        compiler_params=pltpu.CompilerParams(dimension_semantics=("arbitrary",), vmem_limit_bytes=48 * MIB),
        name="moe_experts",
    )(plan["ea"], plan["eb"], plan["nvalid"], xs, plan["wab"], w_gu, w_gu, w_d, w_d)


def _moe_plan(meta, T):
    tm = MOE_TILE
    n_tiles = -(-T // tm) + MOE_BUCKETS
    bucket = meta[0].astype(I32)
    onehot = (bucket[:, None] == jnp.arange(MOE_BUCKETS, dtype=I32)[None, :]).astype(I32)
    cum = jnp.cumsum(onehot, axis=0)
    rank = jnp.sum(cum * onehot, axis=1) - 1
    counts = cum[-1]
    ntile_b = (counts + tm - 1) // tm
    tend_b = jnp.cumsum(ntile_b)
    tstart_b = tend_b - ntile_b
    dest = tstart_b[bucket] * tm + rank
    tok_of_pos = jnp.zeros((n_tiles * tm,), I32).at[dest].set(jnp.arange(T, dtype=I32))
    wrow = jnp.zeros((T, LANES), F32).at[:, 0].set(meta[1]).at[:, 1].set(meta[2])
    wab = jnp.zeros((n_tiles * tm, LANES), F32).at[dest].set(wrow)
    tiles = jnp.arange(n_tiles, dtype=I32)
    tile_b = jnp.minimum(jnp.searchsorted(tend_b, tiles, side="right").astype(I32), MOE_BUCKETS - 1)
    nvalid = jnp.where(tiles < tend_b[-1],
                       jnp.clip(counts[tile_b] - (tiles - tstart_b[tile_b]) * tm, 0, tm), 0).astype(I32)
    grp = tile_b // MOE_PAIRS
    pair = tile_b % MOE_PAIRS
    ea = grp * MOE_PER_GROUP + jnp.asarray(MOE_PAIR_LO, I32)[pair]
    eb = grp * MOE_PER_GROUP + jnp.asarray(MOE_PAIR_HI, I32)[pair]
    return dict(dest=dest, tok_of_pos=tok_of_pos, wab=wab, nvalid=nvalid, ea=ea.astype(I32), eb=eb.astype(I32))


def _hmoe(b2c, h, g, shift, scale, router, w_gu, w_d):
    T = h.shape[0]
    hn, meta = _router(b2c, h, g, shift, scale, *router, tm_want=512)
    plan = _moe_plan(meta, T)
    xs = _gather_rows(hn, plan["tok_of_pos"])
    ys = _moe_ffn(plan, xs, w_gu, w_d)
    return _gather_rows(ys, plan["dest"])


def _tables(Bp, Lp, Bs, Ls):
    nbp, nbs = Lp // ROW_BLOCK, Ls // ROW_BLOCK
    seq, first, last, ext, ninv = [], [], [], [], []
    ext_base = 0
    for b in range(Bp + Bs):
        nb = nbp if b < Bp else nbs
        for c in range(nb):
            seq.append(b)
            first.append(int(c == 0))
            last.append(int(c == nb - 1))
            ext.append(ext_base + WINDOW_BLOCKS + c)
            ninv.append(max(0, WINDOW_BLOCKS - c) if b < Bp else 0)
        ext_base += nb + WINDOW_BLOCKS
    as_arr = lambda v: jnp.asarray(np.asarray(v, np.int32))
    return dict(seq=as_arr(seq), first=as_arr(first), last=as_arr(last), ext=as_arr(ext), ninv=as_arr(ninv))


def _router_weights(router_g, bias_g, router_e, bias_e):
    D = router_g.shape[0]
    w = jnp.zeros((LANES, D), F32)
    w = w.at[:MOE_GROUPS].set(router_g.T).at[MOE_GROUPS:MOE_GROUPS + router_e.shape[1]].set(router_e.T)
    w_hi = w.astype(BF16)
    w_lo = (w - w_hi.astype(F32)).astype(BF16)
    rb = jnp.zeros((LANES,), F32).at[:MOE_GROUPS].set(bias_g).at[MOE_GROUPS:MOE_GROUPS + bias_e.shape[0]].set(bias_e)
    return w_hi, w_lo, rb


def kernel(x_prompt, x_sample, state_conv, state_ssm, cache_k, cache_v, c_prompt, c_sample, ada_w, ada_b, norm_mix, norm_ffn, norm_kv, norm_out, ssm_w_in, ssm_conv_w, ssm_conv_b, ssm_dt_bias, ssm_a_log, ssm_d, ssm_norm, ssm_w_out, attn_w_kv, attn_w_q, attn_sinks, attn_w_o, moe_router_g, moe_bias_g, moe_router_e, moe_bias_e, moe_w_gate, moe_w_up, moe_w_down):
    Bp, Lp, D = x_prompt.shape
    Bs, Ls, _ = x_sample.shape
    Tp, Ts = Bp * Lp, Bs * Ls
    T = Tp + Ts
    C = Bp + Bs
    tabs = _tables(Bp, Lp, Bs, Ls)
    b2c = tabs["seq"]

    n_mod = ada_w.shape[1] // D
    mods = _mods(jnp.concatenate([c_prompt, c_sample], axis=0), ada_w, ada_b).reshape(C, n_mod, D)
    mod = lambda i: mods[:, i:i + 1, :]
    x = jnp.concatenate([x_prompt.reshape(Tp, D), x_sample.reshape(Ts, D)], axis=0)

    HP = ssm_w_out.shape[1]
    H = HP // SSM_HEAD_DIM
    N = SSM_D_STATE
    GN = SSM_GROUPS * N
    conv_dim = HP + 2 * GN
    w_in = jnp.pad(ssm_w_in[0], ((0, 0), (0, LANES - H))).astype(BF16)
    zx = _norm_matmul(b2c, x, norm_mix[0], mod(0), mod(1), w_in, out_dtype=BF16,
                      tn=(2 * HP + 2 * GN + LANES) // 9, tm_want=1024)
    cprev = jnp.zeros((C, CONV_PAD, conv_dim), F32).at[Bp:, CONV_PAD - (CONV_W - 1):].set(state_conv[0])
    sprev = jnp.concatenate([jnp.zeros((Bp, N, HP), F32),
                             jnp.transpose(state_ssm[0], (0, 3, 1, 2)).reshape(Bs, N, HP)], axis=0)
    pad_h = lambda v: jnp.pad(v.astype(F32), (0, LANES - H)).reshape(1, LANES)
    expand = (jnp.arange(LANES)[:, None] == (jnp.arange(HP) // SSM_HEAD_DIM)[None, :]).astype(BF16)
    y_ssd, cnew, snew = _ssd(
        tabs, zx, cprev, sprev, ssm_conv_w[0], ssm_conv_b[0].reshape(1, conv_dim), pad_h(ssm_dt_bias[0]),
        pad_h(-jnp.exp(ssm_a_log[0].astype(F32))), jnp.repeat(ssm_d[0].astype(F32), SSM_HEAD_DIM).reshape(1, HP),
        ssm_norm[0].reshape(1, HP), expand)
    h = _matmul_residual(b2c, y_ssd, ssm_w_out[0].astype(BF16), x, mod(2), tn=512, tm_want=1024)

    def experts(layer):
        w_gu = jnp.concatenate([moe_w_gate[layer], moe_w_up[layer]], axis=-1).astype(BF16)
        return w_gu, moe_w_down[layer].astype(BF16)

    def router(layer):
        return _router_weights(moe_router_g[layer], moe_bias_g[layer], moe_router_e[layer], moe_bias_e[layer])

    moe0 = _hmoe(b2c, h, norm_ffn[0], mod(3), mod(4), router(0), *experts(0))

    kv, h = _norm_matmul(b2c, h, norm_kv, mod(12), mod(13), attn_w_kv.astype(BF16), out_dtype=F32,
                         tn=attn_w_kv.shape[1], tm_want=512, res=moe0, gate=mod(5))
    KV2 = kv.shape[1]
    KVW = KV2 // 2
    W = WINDOW_BLOCKS * ROW_BLOCK
    kvp = kv[:Tp].reshape(Bp, Lp, KV2)
    kvs = jnp.concatenate([jnp.concatenate([cache_k.reshape(Bs, W, KVW), cache_v.reshape(Bs, W, KVW)], axis=-1),
                           kv[Tp:].reshape(Bs, Ls, KV2)], axis=1)
    kv_ext = jnp.concatenate([jnp.pad(kvp, ((0, 0), (W, 0), (0, 0))).reshape(Bp * (Lp + W), KV2),
                              kvs.reshape(Bs * (Ls + W), KV2)], axis=0)
    q = _norm_matmul(b2c, h, norm_mix[1], mod(6), mod(7), attn_w_q[0].astype(BF16), out_dtype=BF16,
                     tn=1024, tm_want=1024)
    o = _attention(tabs, q, kv_ext, attn_sinks[0].astype(F32))
    h = _matmul_residual(b2c, o, attn_w_o[0].astype(BF16), h, mod(8), tn=1024, tm_want=1024)
    moe1 = _hmoe(b2c, h, norm_ffn[1], mod(9), mod(10), router(1), *experts(1))

    fin = functools.partial(_final_norm, b2c, h, moe1, mod(11), mod(14), mod(15), norm_out, tm_want=512)
    y_prompt = fin(row_off=0, n_rows=Tp).reshape(Bp, Lp, D)
    y_sample = fin(row_off=Tp, n_rows=Ts).reshape(Bs, Ls, D)

    kv_heads = KVW // ATT_HEAD_DIM
    tail = lambda a, lo: a[:, -W:, lo:lo + KVW].reshape(a.shape[0], W, kv_heads, ATT_HEAD_DIM)
    conv_tail = cnew[:, CONV_PAD - (CONV_W - 1):]
    ssm_new = jnp.transpose(snew.reshape(C, N, H, SSM_HEAD_DIM), (0, 2, 3, 1))
    return (y_prompt, y_sample, conv_tail[None, :Bp], ssm_new[None, :Bp], tail(kvp, 0), tail(kvp, KVW),
            conv_tail[None, Bp:], ssm_new[None, Bp:], tail(kvs, 0), tail(kvs, KVW))
```

```python
import functools

import numpy as np
import jax
import jax.numpy as jnp
from jax import lax
from jax.experimental import pallas as pl
from jax.experimental.pallas import tpu as pltpu
from jax.experimental.pallas import tpu_sc as plsc

F32 = jnp.float32
BF16 = jnp.bfloat16
I32 = jnp.int32
EPS = 1e-6
ROW_BLOCK = 64
WINDOW_BLOCKS = 2
LANES = 128
MIB = 1024 * 1024

SSM_HEAD_DIM = 64
SSM_GROUPS = 8
SSM_D_STATE = 128
CONV_W = 4
CONV_PAD = 8
ATT_HEAD_DIM = 64
ATT_GROUP = 8
MOE_GROUPS = 4
MOE_PER_GROUP = 4
MOE_PAIR_LO = (0, 0, 0, 1, 1, 2)
MOE_PAIR_HI = (1, 2, 3, 2, 3, 3)
MOE_PAIRS = len(MOE_PAIR_LO)
MOE_BUCKETS = MOE_GROUPS * MOE_PAIRS
MOE_TILE = 256
SC_CORES = 2
SC_SUBCORES = 16
SC_GATHER_ROWS = 16


def _pick_tile(n, want):
    t = min(want, n)
    t -= t % ROW_BLOCK
    while n % t:
        t -= ROW_BLOCK
    return t


def _silu(x):
    return x * (1.0 / (1.0 + jnp.exp(-x)))


def _softplus(x):
    return jnp.maximum(x, 0.0) + jnp.log1p(jnp.exp(-jnp.abs(x)))


def _rms_mod(hv, g, shift, scale):
    ms = jnp.mean(hv * hv, axis=-1, keepdims=True)
    return (hv * lax.rsqrt(ms + EPS) * g) * (1.0 + scale) + shift


def _dot(a, b):
    return jnp.dot(a, b, preferred_element_type=F32)


def _nt_dot(a, b):
    return lax.dot_general(a, b, (((1,), (1,)), ((), ())), preferred_element_type=F32)


def _split_bf16(x, parts):
    out = []
    r = x
    for _ in range(parts):
        p = r.astype(BF16)
        out.append(p)
        r = r - p.astype(F32)
    return out


def _row_block(s):
    return pl.ds(pl.multiple_of(s * ROW_BLOCK, ROW_BLOCK), ROW_BLOCK)


def _mods_kernel(c_ref, w_ref, b_ref, o_ref):
    a = _silu(c_ref[...]).astype(BF16)
    o_ref[...] = _dot(a, w_ref[...].astype(BF16)) + b_ref[...]


def _mods(c_all, ada_w, ada_b):
    C, D = c_all.shape
    N = ada_w.shape[1]
    tn = 1024
    return pl.pallas_call(
        _mods_kernel,
        out_shape=jax.ShapeDtypeStruct((C, N), F32),
        grid=(N // tn,),
        in_specs=[pl.BlockSpec((C, D), lambda j: (0, 0)),
                  pl.BlockSpec((D, tn), lambda j: (0, j)),
                  pl.BlockSpec((1, tn), lambda j: (0, j))],
        out_specs=pl.BlockSpec((C, tn), lambda j: (0, j)),
        compiler_params=pltpu.CompilerParams(dimension_semantics=("parallel",), vmem_limit_bytes=40 * MIB),
        name="ada_mods",
    )(c_all, ada_w, ada_b.reshape(1, N))


def _nmm_kernel(*refs, nsb, has_res):
    if has_res:
        b2c_ref, h_ref, res_ref, gate_ref, shift_ref, scale_ref, g_ref, w_ref, o_ref, hnew_ref, xn_ref = refs
    else:
        b2c_ref, h_ref, shift_ref, scale_ref, g_ref, w_ref, o_ref, xn_ref = refs
    i = pl.program_id(0)

    @pl.when(pl.program_id(1) == 0)
    def _():
        def body(s, carry):
            c = b2c_ref[i * nsb + s]
            rows = _row_block(s)
            hv = h_ref[rows, :]
            if has_res:
                hv = hv + gate_ref[c] * res_ref[rows, :]
                hnew_ref[rows, :] = hv
            xn_ref[rows, :] = _rms_mod(hv, g_ref[...], shift_ref[c], scale_ref[c]).astype(BF16)
            return carry
        lax.fori_loop(0, nsb, body, 0)

    o_ref[...] = _dot(xn_ref[...], w_ref[...]).astype(o_ref.dtype)


def _norm_matmul(b2c, h, g, shift, scale, w, *, out_dtype, tn, tm_want, res=None, gate=None):
    T, D = h.shape
    N = w.shape[1]
    C = shift.shape[0]
    tm = _pick_tile(T, tm_want)
    nsb = tm // ROW_BLOCK
    has_res = res is not None
    row_spec = pl.BlockSpec((tm, D), lambda i, j, b: (i, 0))
    mod_spec = pl.BlockSpec((C, 1, D), lambda i, j, b: (0, 0, 0))
    in_specs = [row_spec]
    args = [h]
    if has_res:
        in_specs += [row_spec, mod_spec]
        args += [res, gate]
    in_specs += [mod_spec, mod_spec, pl.BlockSpec((1, D), lambda i, j, b: (0, 0)),
                 pl.BlockSpec((D, tn), lambda i, j, b: (0, j))]
    args += [shift, scale, g.reshape(1, D), w]
    out_shape = [jax.ShapeDtypeStruct((T, N), out_dtype)]
    out_specs = [pl.BlockSpec((tm, tn), lambda i, j, b: (i, j))]
    if has_res:
        out_shape.append(jax.ShapeDtypeStruct((T, D), F32))
        out_specs.append(row_spec)
    n_row_bufs = 3 if has_res else 1
    vmem = (2 * n_row_bufs * tm * D * 4 + tm * D * 2 + 2 * D * tn * 2
            + 2 * tm * tn * jnp.dtype(out_dtype).itemsize + 8 * MIB)
    outs = pl.pallas_call(
        functools.partial(_nmm_kernel, nsb=nsb, has_res=has_res),
        out_shape=out_shape,
        grid_spec=pltpu.PrefetchScalarGridSpec(
            num_scalar_prefetch=1, grid=(T // tm, N // tn), in_specs=in_specs, out_specs=out_specs,
            scratch_shapes=[pltpu.VMEM((tm, D), BF16)]),
        compiler_params=pltpu.CompilerParams(dimension_semantics=("parallel", "arbitrary"),
                                             vmem_limit_bytes=int(vmem)),
        name="norm_matmul_res" if has_res else "norm_matmul",
    )(b2c, *args)
    return outs if has_res else outs[0]


def _mmres_kernel(b2c_ref, a_ref, w_ref, h_ref, gate_ref, o_ref, acc_ref, *, nsb):
    i = pl.program_id(0)
    acc_ref[...] = _dot(a_ref[...], w_ref[...])

    def body(s, carry):
        c = b2c_ref[i * nsb + s]
        rows = _row_block(s)
        o_ref[rows, :] = h_ref[rows, :] + gate_ref[c] * acc_ref[rows, :]
        return carry
    lax.fori_loop(0, nsb, body, 0)


def _matmul_residual(b2c, a, w, h, gate, *, tn, tm_want):
    T, K = a.shape
    D = w.shape[1]
    C = gate.shape[0]
    tm = _pick_tile(T, tm_want)
    nsb = tm // ROW_BLOCK
    vmem = 2 * tm * K * 2 + 2 * K * tn * 2 + 5 * tm * tn * 4 + 8 * MIB
    return pl.pallas_call(
        functools.partial(_mmres_kernel, nsb=nsb),
        out_shape=jax.ShapeDtypeStruct((T, D), F32),
        grid_spec=pltpu.PrefetchScalarGridSpec(
            num_scalar_prefetch=1, grid=(T // tm, D // tn),
            in_specs=[pl.BlockSpec((tm, K), lambda i, j, b: (i, 0)),
                      pl.BlockSpec((K, tn), lambda i, j, b: (0, j)),
                      pl.BlockSpec((tm, tn), lambda i, j, b: (i, j)),
                      pl.BlockSpec((C, 1, tn), lambda i, j, b: (0, 0, j))],
            out_specs=pl.BlockSpec((tm, tn), lambda i, j, b: (i, j)),
            scratch_shapes=[pltpu.VMEM((tm, tn), F32)]),
        compiler_params=pltpu.CompilerParams(dimension_semantics=("parallel", "arbitrary"),
                                             vmem_limit_bytes=int(vmem)),
        name="matmul_residual",
    )(b2c, a, w, h, gate)


def _final_kernel(b2c_ref, h_ref, res_ref, gate_ref, shift_ref, scale_ref, g_ref, o_ref, *, nsb, blk_off):
    i = pl.program_id(0) + blk_off

    def body(s, carry):
        c = b2c_ref[i * nsb + s]
        rows = _row_block(s)
        hv = h_ref[rows, :] + gate_ref[c] * res_ref[rows, :]
        o_ref[rows, :] = _rms_mod(hv, g_ref[...], shift_ref[c], scale_ref[c])
        return carry
    lax.fori_loop(0, nsb, body, 0)


def _final_norm(b2c, h, res, gate, shift, scale, g, *, row_off, n_rows, tm_want):
    T, D = h.shape
    C = shift.shape[0]
    tm = _pick_tile(int(np.gcd(row_off, n_rows)) if row_off else n_rows, tm_want)
    nsb = tm // ROW_BLOCK
    blk_off = row_off // tm
    row_spec = pl.BlockSpec((tm, D), lambda i, b: (i + blk_off, 0))
    mod_spec = pl.BlockSpec((C, 1, D), lambda i, b: (0, 0, 0))
    return pl.pallas_call(
        functools.partial(_final_kernel, nsb=nsb, blk_off=blk_off),
        out_shape=jax.ShapeDtypeStruct((n_rows, D), F32),
        grid_spec=pltpu.PrefetchScalarGridSpec(
            num_scalar_prefetch=1, grid=(n_rows // tm,),
            in_specs=[row_spec, row_spec, mod_spec, mod_spec, mod_spec,
                      pl.BlockSpec((1, D), lambda i, b: (0, 0))],
            out_specs=pl.BlockSpec((tm, D), lambda i, b: (i, 0))),
        compiler_params=pltpu.CompilerParams(dimension_semantics=("parallel",),
                                             vmem_limit_bytes=int(6 * tm * D * 4 + 8 * MIB)),
        name="final_norm",
    )(b2c, h, res, gate, shift, scale, g.reshape(1, D))


def _ssd_kernel(seq_ref, first_ref, last_ref,
                z_ref, x_ref, b_ref, c_ref, dt_ref,
                cpx_ref, cpb_ref, cpc_ref, sprev_ref,
                cwx_ref, cwb_ref, cwc_ref, cbx_ref, cbb_ref, cbc_ref,
                dtb_ref, a_ref, dsk_ref, ng_ref, expand_ref,
                y_ref, cnew_ref, snew_ref,
                extx_ref, extb_ref, extc_ref, st_ref, xs_ref, bm_ref, cm_ref, col_ref, rowa_ref, rowd_ref,
                ybuf_ref):
    Q = ROW_BLOCK
    N = SSM_D_STATE
    HP = x_ref.shape[1]
    GN = b_ref.shape[1]
    G = GN // N
    GW = HP // G
    PAIRS_PER_GROUP = GW // LANES
    k = pl.program_id(0)

    @pl.when(first_ref[k] == 1)
    def _():
        extx_ref[0:CONV_PAD, :] = cpx_ref[0]
        extb_ref[0:CONV_PAD, :] = cpb_ref[0]
        extc_ref[0:CONV_PAD, :] = cpc_ref[0]
        st_ref[...] = sprev_ref[0]

    def conv(ext_ref, cur_ref, w_ref, bias_ref):
        ext_ref[CONV_PAD:CONV_PAD + Q, :] = cur_ref[...].astype(F32)
        acc = bias_ref[...] + w_ref[CONV_W - 1:CONV_W, :] * ext_ref[CONV_PAD:CONV_PAD + Q, :]
        for j in range(CONV_W - 1):
            off = CONV_PAD - (CONV_W - 1) + j
            acc = acc + w_ref[j:j + 1, :] * ext_ref[off:off + Q, :]
        return _silu(acc)

    xs_ref[...] = conv(extx_ref, x_ref, cwx_ref, cbx_ref)
    bm_ref[...] = conv(extb_ref, b_ref, cwb_ref, cbb_ref).astype(BF16)
    cm_ref[...] = conv(extc_ref, c_ref, cwc_ref, cbc_ref).astype(BF16)
    for ext_ref, lo, hi in ((extx_ref, 0, HP), (extb_ref, HP, HP + GN), (extc_ref, HP + GN, HP + 2 * GN)):
        tail = ext_ref[Q:Q + CONV_PAD, :]
        cnew_ref[0, :, lo:hi] = tail
        ext_ref[0:CONV_PAD, :] = tail

    dt = _softplus(dt_ref[...].astype(F32) + dtb_ref[...])
    da = dt * a_ref[...]
    rr = lax.broadcasted_iota(I32, (Q, Q), 0)
    cc = lax.broadcasted_iota(I32, (Q, Q), 1)
    tri = jnp.where(rr >= cc, 1.0, 0.0).astype(BF16)
    cs = _dot(tri, jnp.concatenate(_split_bf16(da, 3), axis=1))
    acs = cs[:, 0:LANES] + cs[:, LANES:2 * LANES] + cs[:, 2 * LANES:3 * LANES]

    both = _split_bf16(jnp.concatenate([acs, dt], axis=0), 2)
    col_ref[...] = _dot(both[0], expand_ref[...]) + _dot(both[1], expand_ref[...])

    pr = lax.broadcasted_iota(I32, (LANES // 2, 2 * LANES), 0)
    pk = lax.broadcasted_iota(I32, (LANES // 2, 2 * LANES), 1)
    esel = jnp.where(pk == jnp.where(pk < LANES, 2 * pr, 2 * pr + 1 + LANES), 1.0, 0.0).astype(BF16)
    zero = jnp.zeros((Q, LANES), BF16)

    def pair_rows(v, parts):
        out = None
        for piece in _split_bf16(v, parts):
            vbd = jnp.concatenate([jnp.concatenate([piece, zero], axis=1),
                                   jnp.concatenate([zero, piece], axis=1)], axis=0)
            r = _nt_dot(esel, vbd)
            out = r if out is None else out + r
        return out

    rowa_ref[...] = pair_rows(acs, 3)
    rowd_ref[...] = pair_rows(dt, 2)

    lane = lax.broadcasted_iota(I32, (Q, LANES), 1)
    trow = lax.broadcasted_iota(I32, (Q, LANES), 0)
    causal = trow >= (lane & (Q - 1))
    lo_half = lane < Q
    er = lax.broadcasted_iota(I32, (N, N), 0)
    ec = lax.broadcasted_iota(I32, (N, N), 1)
    eye = jnp.where(er == ec, 1.0, 0.0).astype(BF16)

    for g in range(G):
        gs = slice(g * GW, (g + 1) * GW)
        bg = bm_ref[:, g * N:(g + 1) * N]
        cg = cm_ref[:, g * N:(g + 1) * N]
        cb2 = _nt_dot(cg, jnp.concatenate([bg, bg], axis=0))
        y_state = _dot(cg, st_ref[:, gs].astype(BF16))
        for pp in range(PAIRS_PER_GROUP):
            p = g * PAIRS_PER_GROUP + pp
            ps = slice(p * LANES, (p + 1) * LANES)
            col = col_ref[0:Q, ps]
            seg = col - rowa_ref[p:p + 1, :]
            decay = jnp.exp(jnp.where(causal, seg, -jnp.inf))
            lmat = (cb2 * decay * rowd_ref[p:p + 1, :]).astype(BF16)
            xp = xs_ref[:, ps]
            xbd = jnp.concatenate([jnp.where(lo_half, xp, 0.0), jnp.where(lo_half, 0.0, xp)],
                                  axis=0).astype(BF16)
            y_in = _dot(lmat, xbd)
            ybuf_ref[:, ps] = (y_in + y_state[:, pp * LANES:(pp + 1) * LANES] * jnp.exp(col)
                               + dsk_ref[:, ps] * xp)
        colg = col_ref[0:Q, gs]
        last = col_ref[Q - 1:Q, gs]
        w_end = col_ref[Q:2 * Q, gs] * jnp.exp(last - colg)
        xw = (xs_ref[:, gs] * w_end).astype(BF16)
        bgt = _nt_dot(eye, bg).astype(BF16)
        st_ref[:, gs] = st_ref[:, gs] * jnp.exp(last) + _dot(bgt, xw)

    yv = ybuf_ref[...] * _silu(z_ref[...].astype(F32))
    ms = jnp.mean(yv * yv, axis=-1, keepdims=True)
    y_ref[...] = (yv * lax.rsqrt(ms + EPS) * ng_ref[...]).astype(BF16)

    @pl.when(last_ref[k] == 1)
    def _():
        snew_ref[0] = st_ref[...]


def _ssd(tabs, zx, cprev, sprev, conv_w, conv_b, dt_bias, a_neg, d_skip, norm_g, expand):
    T = zx.shape[0]
    S, N, HP = sprev.shape
    GN = SSM_GROUPS * N
    Q = ROW_BLOCK
    nblk = T // Q
    bb, cb = HP // GN, HP // GN + 1
    zb = (2 * HP) // GN

    def rows(width, col_blk):
        return pl.BlockSpec((Q, width), lambda k, s, f, l: (k, col_blk))

    def per_seq(shape, col_blk):
        return pl.BlockSpec((1,) + shape, lambda k, s, f, l: (s[k], 0, col_blk))

    def const(shape, col_blk=0):
        return pl.BlockSpec(shape, lambda k, s, f, l: (0, col_blk))

    in_specs = [
        rows(HP, 0), rows(HP, 1), rows(GN, zb), rows(GN, zb + 1), rows(LANES, (2 * HP + 2 * GN) // LANES),
        per_seq((CONV_PAD, HP), 0), per_seq((CONV_PAD, GN), bb), per_seq((CONV_PAD, GN), cb),
        per_seq((N, HP), 0),
        const((CONV_W, HP)), const((CONV_W, GN), bb), const((CONV_W, GN), cb),
        const((1, HP)), const((1, GN), bb), const((1, GN), cb),
        const((1, LANES)), const((1, LANES)), const((1, HP)), const((1, HP)), const((LANES, HP)),
    ]
    out_shape = [jax.ShapeDtypeStruct((T, HP), BF16),
                 jax.ShapeDtypeStruct((S, CONV_PAD, HP + 2 * GN), F32),
                 jax.ShapeDtypeStruct((S, N, HP), F32)]
    out_specs = [pl.BlockSpec((Q, HP), lambda k, s, f, l: (k, 0)),
                 pl.BlockSpec((1, CONV_PAD, HP + 2 * GN), lambda k, s, f, l: (s[k], 0, 0)),
                 pl.BlockSpec((1, N, HP), lambda k, s, f, l: (s[k], 0, 0))]
    scratch = [pltpu.VMEM((Q + CONV_PAD, HP), F32), pltpu.VMEM((Q + CONV_PAD, GN), F32),
               pltpu.VMEM((Q + CONV_PAD, GN), F32), pltpu.VMEM((N, HP), F32), pltpu.VMEM((Q, HP), F32),
               pltpu.VMEM((Q, GN), BF16), pltpu.VMEM((Q, GN), BF16), pltpu.VMEM((2 * Q, HP), F32),
               pltpu.VMEM((LANES // 2, 2 * Q), F32), pltpu.VMEM((LANES // 2, 2 * Q), F32),
               pltpu.VMEM((Q, HP), F32)]
    return pl.pallas_call(
        _ssd_kernel,
        out_shape=out_shape,
        grid_spec=pltpu.PrefetchScalarGridSpec(num_scalar_prefetch=3, grid=(nblk,), in_specs=in_specs,
                                               out_specs=out_specs, scratch_shapes=scratch),
        compiler_params=pltpu.CompilerParams(dimension_semantics=("arbitrary",), vmem_limit_bytes=48 * MIB),
        name="ssd_mixer",
    )(tabs["seq"], tabs["first"], tabs["last"],
      zx, zx, zx, zx, zx, cprev, cprev, cprev, sprev,
      conv_w, conv_w, conv_w, conv_b, conv_b, conv_b, dt_bias, a_neg, d_skip, norm_g, expand)


def _attn_kernel(ext_ref, ninv_ref, q_ref, w0_ref, w1_ref, w2_ref, bias_ref, sink_ref, o_ref):
    Q = ROW_BLOCK
    DH = ATT_HEAD_DIM
    KVW = w0_ref.shape[1] // 2
    n_kv = KVW // DH
    GQ, NK = bias_ref.shape[1], bias_ref.shape[2]
    k = pl.program_id(0)
    s = lax.broadcasted_iota(I32, (GQ, NK), 1)
    mask_bias = jnp.where(s >= ninv_ref[k] * Q, 0.0, -jnp.inf)
    wins = (w0_ref, w1_ref, w2_ref)
    for g in range(n_kv):
        kg = jnp.concatenate([w[:, g * DH:(g + 1) * DH] for w in wins], axis=0).astype(BF16)
        vg = jnp.concatenate([w[:, KVW + g * DH:KVW + (g + 1) * DH] for w in wins], axis=0).astype(BF16)
        qg = jnp.concatenate([q_ref[:, (g * ATT_GROUP + r) * DH:(g * ATT_GROUP + r + 1) * DH]
                              for r in range(ATT_GROUP)], axis=0)
        sc = _nt_dot(qg, kg) * (DH ** -0.5) + bias_ref[g] + mask_bias
        sink = sink_ref[g][:, 0:1]
        m = jnp.maximum(jnp.max(sc, axis=-1, keepdims=True), sink)
        p = jnp.exp(sc - m)
        den = jnp.sum(p, axis=-1, keepdims=True) + jnp.exp(sink - m)
        og = _dot(p.astype(BF16), vg) * (1.0 / den)
        for rp in range(ATT_GROUP // 2):
            h0 = g * ATT_GROUP + 2 * rp
            pair = jnp.concatenate([og[2 * rp * Q:(2 * rp + 1) * Q], og[(2 * rp + 1) * Q:(2 * rp + 2) * Q]], axis=1)
            o_ref[:, h0 * DH:(h0 + 2) * DH] = pair.astype(BF16)


def _attention(tabs, q, kv_ext, sinks):
    T, HD = q.shape
    Q = ROW_BLOCK
    NK = (WINDOW_BLOCKS + 1) * Q
    n_heads = HD // ATT_HEAD_DIM
    KV2 = kv_ext.shape[1]
    n_kv = KV2 // (2 * ATT_HEAD_DIM)
    GQ = ATT_GROUP * Q
    slopes = 2.0 ** (-8.0 * np.arange(1, n_heads + 1) / n_heads)
    dist = np.abs(np.arange(Q)[:, None] + WINDOW_BLOCKS * Q - np.arange(NK)[None, :])
    bias = jnp.asarray((-slopes[:, None, None] * dist[None]).reshape(n_kv, GQ, NK).astype(np.float32))
    sink_rows = jnp.broadcast_to(jnp.repeat(sinks.reshape(n_kv, ATT_GROUP), Q, axis=1)[:, :, None],
                                 (n_kv, GQ, LANES))

    def win(j):
        return pl.BlockSpec((Q, KV2), lambda k, e, n: (e[k] - WINDOW_BLOCKS + j, 0))

    return pl.pallas_call(
        _attn_kernel,
        out_shape=jax.ShapeDtypeStruct((T, HD), BF16),
        grid_spec=pltpu.PrefetchScalarGridSpec(
            num_scalar_prefetch=2, grid=(T // Q,),
            in_specs=[pl.BlockSpec((Q, HD), lambda k, e, n: (k, 0)), win(0), win(1), win(2),
                      pl.BlockSpec((n_kv, GQ, NK), lambda k, e, n: (0, 0, 0)),
                      pl.BlockSpec((n_kv, GQ, LANES), lambda k, e, n: (0, 0, 0))],
            out_specs=pl.BlockSpec((Q, HD), lambda k, e, n: (k, 0))),
        compiler_params=pltpu.CompilerParams(dimension_semantics=("parallel",), vmem_limit_bytes=32 * MIB),
        name="swa_attention",
    )(tabs["ext"], tabs["ninv"], q, kv_ext, kv_ext, kv_ext, bias, sink_rows)


def _router_kernel(b2c_ref, h_ref, shift_ref, scale_ref, g_ref, wh_ref, wl_ref, rb_ref, hn_ref, meta_ref,
                   xh_ref, xl_ref, *, nsb):
    i = pl.program_id(0)

    def body(s, carry):
        c = b2c_ref[i * nsb + s]
        rows = _row_block(s)
        xn = _rms_mod(h_ref[rows, :], g_ref[...], shift_ref[c], scale_ref[c])
        hn_ref[rows, :] = xn
        hi, lo = _split_bf16(xn, 2)
        xh_ref[rows, :] = hi
        xl_ref[rows, :] = lo
        return carry
    lax.fori_loop(0, nsb, body, 0)

    lt = (_nt_dot(wh_ref[...], xh_ref[...]) + _nt_dot(wh_ref[...], xl_ref[...])
          + _nt_dot(wl_ref[...], xh_ref[...]) + rb_ref[...])

    def first_max(vals):
        m = vals[0]
        for v in vals[1:]:
            m = jnp.maximum(m, v)
        idx = jnp.full(m.shape, len(vals) - 1, I32)
        for j in range(len(vals) - 2, -1, -1):
            idx = jnp.where(vals[j] == m, j, idx)
        return m, idx

    lg = [lt[j:j + 1, :] for j in range(MOE_GROUPS)]
    mg, gi = first_max(lg)
    p_sel = 1.0 / sum(jnp.exp(v - mg) for v in lg)
    le = []
    for j in range(MOE_PER_GROUP):
        v = lt[MOE_GROUPS + (MOE_GROUPS - 1) * MOE_PER_GROUP + j:MOE_GROUPS + (MOE_GROUPS - 1) * MOE_PER_GROUP + j + 1, :]
        for grp in range(MOE_GROUPS - 2, -1, -1):
            row = MOE_GROUPS + grp * MOE_PER_GROUP + j
            v = jnp.where(gi == grp, lt[row:row + 1, :], v)
        le.append(v)
    m1, i1 = first_max(le)
    m2, i2 = first_max([jnp.where(i1 == j, -jnp.inf, le[j]) for j in range(MOE_PER_GROUP)])
    e2 = jnp.exp(m2 - m1)
    w1 = p_sel * (1.0 / (1.0 + e2))
    w2 = p_sel * (e2 / (1.0 + e2))
    lo_i = jnp.minimum(i1, i2)
    hi_i = jnp.maximum(i1, i2)
    first_is_lo = i1 < i2
    pair_base = jnp.where(lo_i == 0, 0, jnp.where(lo_i == 1, 3, 5))
    bucket = gi * MOE_PAIRS + pair_base + hi_i - lo_i - 1
    meta_ref[0:1, :] = bucket.astype(F32)
    meta_ref[1:2, :] = jnp.where(first_is_lo, w1, w2)
    meta_ref[2:3, :] = jnp.where(first_is_lo, w2, w1)
    meta_ref[3:8, :] = jnp.zeros((5, meta_ref.shape[1]), F32)


def _router(b2c, h, g, shift, scale, w_hi, w_lo, rbias, *, tm_want):
    T, D = h.shape
    C = shift.shape[0]
    tm = _pick_tile(T, tm_want)
    nsb = tm // ROW_BLOCK
    row_spec = pl.BlockSpec((tm, D), lambda i, b: (i, 0))
    mod_spec = pl.BlockSpec((C, 1, D), lambda i, b: (0, 0, 0))
    rb = jnp.broadcast_to(rbias.reshape(LANES, 1), (LANES, tm))
    return pl.pallas_call(
        functools.partial(_router_kernel, nsb=nsb),
        out_shape=[jax.ShapeDtypeStruct((T, D), F32), jax.ShapeDtypeStruct((8, T), F32)],
        grid_spec=pltpu.PrefetchScalarGridSpec(
            num_scalar_prefetch=1, grid=(T // tm,),
            in_specs=[row_spec, mod_spec, mod_spec, pl.BlockSpec((1, D), lambda i, b: (0, 0)),
                      pl.BlockSpec((LANES, D), lambda i, b: (0, 0)), pl.BlockSpec((LANES, D), lambda i, b: (0, 0)),
                      pl.BlockSpec((LANES, tm), lambda i, b: (0, 0))],
            out_specs=[row_spec, pl.BlockSpec((8, tm), lambda i, b: (0, i))],
            scratch_shapes=[pltpu.VMEM((tm, D), BF16), pltpu.VMEM((tm, D), BF16)]),
        compiler_params=pltpu.CompilerParams(dimension_semantics=("parallel",),
                                             vmem_limit_bytes=int(5 * tm * D * 4 + 12 * MIB)),
        name="moe_router",
    )(b2c, h, shift, scale, g.reshape(1, D), w_hi, w_lo, rb)


def _gather_rows(src, idx):
    n = idx.shape[0]
    D = src.shape[1]
    n_workers = SC_CORES * SC_SUBCORES
    per_worker = n // n_workers
    n_chunks = per_worker // SC_GATHER_ROWS
    assert n == n_workers * n_chunks * SC_GATHER_ROWS, (n, n_workers, SC_GATHER_ROWS)
    mesh = plsc.VectorSubcoreMesh(core_axis_name="c", subcore_axis_name="s", num_cores=SC_CORES,
                                  num_subcores=SC_SUBCORES)

    @functools.partial(
        pl.kernel, mesh=mesh, out_type=jax.ShapeDtypeStruct((n, D), src.dtype),
        scratch_types=[pltpu.VMEM((SC_GATHER_ROWS,), I32), pltpu.VMEM((SC_GATHER_ROWS, D), src.dtype),
                       pltpu.SemaphoreType.DMA],
        name="sc_row_gather")
    def gather(src_hbm, idx_hbm, out_hbm, idx_v, rows_v, sem):
        base = (lax.axis_index("s") * SC_CORES + lax.axis_index("c")) * per_worker

        @pl.loop(0, n_chunks)
        def _(j):
            off = base + j * SC_GATHER_ROWS
            pltpu.sync_copy(idx_hbm.at[pl.ds(off, SC_GATHER_ROWS)], idx_v)
            pltpu.async_copy(src_hbm.at[idx_v], rows_v, sem).wait()
            pltpu.sync_copy(rows_v, out_hbm.at[pl.ds(off, SC_GATHER_ROWS)])

    return gather(src, idx)


def _ffn_kernel(ea_ref, eb_ref, nv_ref, x_ref, wab_ref, wgua_ref, wgub_ref, wda_ref, wdb_ref, y_ref):
    i = pl.program_id(0)
    F = wda_ref.shape[1]

    @pl.when(nv_ref[i] > 0)
    def _():
        x = x_ref[...].astype(BF16)

        def expert(wgu_ref, wd_ref):
            gu = _dot(x, wgu_ref[0])
            hid = _silu(gu[:, :F]) * gu[:, F:]
            return _dot(hid.astype(BF16), wd_ref[0])
        y_ref[...] = (wab_ref[:, 0:1] * expert(wgua_ref, wda_ref)
                      + wab_ref[:, 1:2] * expert(wgub_ref, wdb_ref))

    @pl.when(nv_ref[i] == 0)
    def _():
        y_ref[...] = jnp.zeros(y_ref.shape, F32)


def _moe_ffn(plan, xs, w_gu, w_d):
    R, D = xs.shape
    tm = MOE_TILE
    E, F, _ = w_d.shape
    return pl.pallas_call(
        _ffn_kernel,
        out_shape=jax.ShapeDtypeStruct((R, D), F32),
        grid_spec=pltpu.PrefetchScalarGridSpec(
            num_scalar_prefetch=3, grid=(R // tm,),
            in_specs=[pl.BlockSpec((tm, D), lambda i, a, b, n: (i, 0)),
                      pl.BlockSpec((tm, LANES), lambda i, a, b, n: (i, 0)),
                      pl.BlockSpec((1, D, 2 * F), lambda i, a, b, n: (a[i], 0, 0)),
                      pl.BlockSpec((1, D, 2 * F), lambda i, a, b, n: (b[i], 0, 0)),
                      pl.BlockSpec((1, F, D), lambda i, a, b, n: (a[i], 0, 0)),
                      pl.BlockSpec((1, F, D), lambda i, a, b, n: (b[i], 0, 0))],
            out_specs=pl.BlockSpec((tm, D), lambda i, a, b, n: (i, 0))),
        compiler_params=pltpu.CompilerParams(dimension_semantics=("arbitrary",), vmem_limit_bytes=48 * MIB),
        name="moe_experts",
    )(plan["ea"], plan["eb"], plan["nvalid"], xs, plan["wab"], w_gu, w_gu, w_d, w_d)


def _moe_plan(meta, T):
    tm = MOE_TILE
    n_tiles = -(-T // tm) + MOE_BUCKETS
    bucket = meta[0].astype(I32)
    onehot = (bucket[:, None] == jnp.arange(MOE_BUCKETS, dtype=I32)[None, :]).astype(I32)
    cum = jnp.cumsum(onehot, axis=0)
    rank = jnp.sum(cum * onehot, axis=1) - 1
    counts = cum[-1]
    ntile_b = (counts + tm - 1) // tm
    tend_b = jnp.cumsum(ntile_b)
    tstart_b = tend_b - ntile_b
    dest = tstart_b[bucket] * tm + rank
    tok_of_pos = jnp.zeros((n_tiles * tm,), I32).at[dest].set(jnp.arange(T, dtype=I32))
    wrow = jnp.zeros((T, LANES), F32).at[:, 0].set(meta[1]).at[:, 1].set(meta[2])
    wab = jnp.zeros((n_tiles * tm, LANES), F32).at[dest].set(wrow)
    tiles = jnp.arange(n_tiles, dtype=I32)
    tile_b = jnp.minimum(jnp.searchsorted(tend_b, tiles, side="right").astype(I32), MOE_BUCKETS - 1)
    nvalid = jnp.where(tiles < tend_b[-1],
                       jnp.clip(counts[tile_b] - (tiles - tstart_b[tile_b]) * tm, 0, tm), 0).astype(I32)
    grp = tile_b // MOE_PAIRS
    pair = tile_b % MOE_PAIRS
    ea = grp * MOE_PER_GROUP + jnp.asarray(MOE_PAIR_LO, I32)[pair]
    eb = grp * MOE_PER_GROUP + jnp.asarray(MOE_PAIR_HI, I32)[pair]
    return dict(dest=dest, tok_of_pos=tok_of_pos, wab=wab, nvalid=nvalid, ea=ea.astype(I32), eb=eb.astype(I32))


def _hmoe(b2c, h, g, shift, scale, router, w_gu, w_d):
    T = h.shape[0]
    hn, meta = _router(b2c, h, g, shift, scale, *router, tm_want=512)
    plan = _moe_plan(meta, T)
    xs = _gather_rows(hn, plan["tok_of_pos"])
    ys = _moe_ffn(plan, xs, w_gu, w_d)
    return _gather_rows(ys, plan["dest"])


def _tables(Bp, Lp, Bs, Ls):
    nbp, nbs = Lp // ROW_BLOCK, Ls // ROW_BLOCK
    seq, first, last, ext, ninv = [], [], [], [], []
    ext_base = 0
    for b in range(Bp + Bs):
        nb = nbp if b < Bp else nbs
        for c in range(nb):
            seq.append(b)
            first.append(int(c == 0))
            last.append(int(c == nb - 1))
            ext.append(ext_base + WINDOW_BLOCKS + c)
            ninv.append(max(0, WINDOW_BLOCKS - c) if b < Bp else 0)
        ext_base += nb + WINDOW_BLOCKS
    as_arr = lambda v: jnp.asarray(np.asarray(v, np.int32))
    return dict(seq=as_arr(seq), first=as_arr(first), last=as_arr(last), ext=as_arr(ext), ninv=as_arr(ninv))


def _router_weights(router_g, bias_g, router_e, bias_e):
    D = router_g.shape[0]
    w = jnp.zeros((LANES, D), F32)
    w = w.at[:MOE_GROUPS].set(router_g.T).at[MOE_GROUPS:MOE_GROUPS + router_e.shape[1]].set(router_e.T)
    w_hi = w.astype(BF16)
    w_lo = (w - w_hi.astype(F32)).astype(BF16)
    rb = jnp.zeros((LANES,), F32).at[:MOE_GROUPS].set(bias_g).at[MOE_GROUPS:MOE_GROUPS + bias_e.shape[0]].set(bias_e)
    return w_hi, w_lo, rb


def kernel(x_prompt, x_sample, state_conv, state_ssm, cache_k, cache_v, c_prompt, c_sample, ada_w, ada_b, norm_mix, norm_ffn, norm_kv, norm_out, ssm_w_in, ssm_conv_w, ssm_conv_b, ssm_dt_bias, ssm_a_log, ssm_d, ssm_norm, ssm_w_out, attn_w_kv, attn_w_q, attn_sinks, attn_w_o, moe_router_g, moe_bias_g, moe_router_e, moe_bias_e, moe_w_gate, moe_w_up, moe_w_down):
    Bp, Lp, D = x_prompt.shape
    Bs, Ls, _ = x_sample.shape
    Tp, Ts = Bp * Lp, Bs * Ls
    T = Tp + Ts
    C = Bp + Bs
    tabs = _tables(Bp, Lp, Bs, Ls)
    b2c = tabs["seq"]

    n_mod = ada_w.shape[1] // D
    mods = _mods(jnp.concatenate([c_prompt, c_sample], axis=0), ada_w, ada_b).reshape(C, n_mod, D)
    mod = lambda i: mods[:, i:i + 1, :]
    x = jnp.concatenate([x_prompt.reshape(Tp, D), x_sample.reshape(Ts, D)], axis=0)

    HP = ssm_w_out.shape[1]
    H = HP // SSM_HEAD_DIM
    N = SSM_D_STATE
    GN = SSM_GROUPS * N
    conv_dim = HP + 2 * GN
    w_in = jnp.pad(ssm_w_in[0], ((0, 0), (0, LANES - H))).astype(BF16)
    zx = _norm_matmul(b2c, x, norm_mix[0], mod(0), mod(1), w_in, out_dtype=BF16,
                      tn=(2 * HP + 2 * GN + LANES) // 9, tm_want=1024)
    cprev = jnp.zeros((C, CONV_PAD, conv_dim), F32).at[Bp:, CONV_PAD - (CONV_W - 1):].set(state_conv[0])
    sprev = jnp.concatenate([jnp.zeros((Bp, N, HP), F32),
                             jnp.transpose(state_ssm[0], (0, 3, 1, 2)).reshape(Bs, N, HP)], axis=0)
    pad_h = lambda v: jnp.pad(v.astype(F32), (0, LANES - H)).reshape(1, LANES)
    expand = (jnp.arange(LANES)[:, None] == (jnp.arange(HP) // SSM_HEAD_DIM)[None, :]).astype(BF16)
    y_ssd, cnew, snew = _ssd(
        tabs, zx, cprev, sprev, ssm_conv_w[0], ssm_conv_b[0].reshape(1, conv_dim), pad_h(ssm_dt_bias[0]),
        pad_h(-jnp.exp(ssm_a_log[0].astype(F32))), jnp.repeat(ssm_d[0].astype(F32), SSM_HEAD_DIM).reshape(1, HP),
        ssm_norm[0].reshape(1, HP), expand)
    h = _matmul_residual(b2c, y_ssd, ssm_w_out[0].astype(BF16), x, mod(2), tn=512, tm_want=1024)

    def experts(layer):
        w_gu = jnp.concatenate([moe_w_gate[layer], moe_w_up[layer]], axis=-1).astype(BF16)
        return w_gu, moe_w_down[layer].astype(BF16)

    def router(layer):
        return _router_weights(moe_router_g[layer], moe_bias_g[layer], moe_router_e[layer], moe_bias_e[layer])

    moe0 = _hmoe(b2c, h, norm_ffn[0], mod(3), mod(4), router(0), *experts(0))

    kv, h = _norm_matmul(b2c, h, norm_kv, mod(12), mod(13), attn_w_kv.astype(BF16), out_dtype=F32,
                         tn=attn_w_kv.shape[1], tm_want=512, res=moe0, gate=mod(5))
    KV2 = kv.shape[1]
    KVW = KV2 // 2
    W = WINDOW_BLOCKS * ROW_BLOCK
    kvp = kv[:Tp].reshape(Bp, Lp, KV2)
    kvs = jnp.concatenate([jnp.concatenate([cache_k.reshape(Bs, W, KVW), cache_v.reshape(Bs, W, KVW)], axis=-1),
                           kv[Tp:].reshape(Bs, Ls, KV2)], axis=1)
    kv_ext = jnp.concatenate([jnp.pad(kvp, ((0, 0), (W, 0), (0, 0))).reshape(Bp * (Lp + W), KV2),
                              kvs.reshape(Bs * (Ls + W), KV2)], axis=0)
    q = _norm_matmul(b2c, h, norm_mix[1], mod(6), mod(7), attn_w_q[0].astype(BF16), out_dtype=BF16,
                     tn=1024, tm_want=1024)
    o = _attention(tabs, q, kv_ext, attn_sinks[0].astype(F32))
    h = _matmul_residual(b2c, o, attn_w_o[0].astype(BF16), h, mod(8), tn=1024, tm_want=1024)
    moe1 = _hmoe(b2c, h, norm_ffn[1], mod(9), mod(10), router(1), *experts(1))

    fin = functools.partial(_final_norm, b2c, h, moe1, mod(11), mod(14), mod(15), norm_out, tm_want=512)
    y_prompt = fin(row_off=0, n_rows=Tp).reshape(Bp, Lp, D)
    y_sample = fin(row_off=Tp, n_rows=Ts).reshape(Bs, Ls, D)

    kv_heads = KVW // ATT_HEAD_DIM
    tail = lambda a, lo: a[:, -W:, lo:lo + KVW].reshape(a.shape[0], W, kv_heads, ATT_HEAD_DIM)
    conv_tail = cnew[:, CONV_PAD - (CONV_W - 1):]
    ssm_new = jnp.transpose(snew.reshape(C, N, H, SSM_HEAD_DIM), (0, 2, 3, 1))
    return (y_prompt, y_sample, conv_tail[None, :Bp], ssm_new[None, :Bp], tail(kvp, 0), tail(kvp, KVW),
            conv_tail[None, Bp:], ssm_new[None, Bp:], tail(kvs, 0), tail(kvs, KVW))
```

```python
import functools

import numpy as np
import jax
import jax.numpy as jnp
from jax import lax
from jax.experimental import pallas as pl
from jax.experimental.pallas import tpu as pltpu
from jax.experimental.pallas import tpu_sc as plsc

F32 = jnp.float32
BF16 = jnp.bfloat16
I32 = jnp.int32
EPS = 1e-6
ROW_BLOCK = 64
WINDOW_BLOCKS = 2
LANES = 128
MIB = 1024 * 1024

SSM_HEAD_DIM = 64
SSM_GROUPS = 8
SSM_D_STATE = 128
CONV_W = 4
CONV_PAD = 8
ATT_HEAD_DIM = 64
ATT_GROUP = 8
MOE_GROUPS = 4
MOE_PER_GROUP = 4
MOE_PAIR_LO = (0, 0, 0, 1, 1, 2)
MOE_PAIR_HI = (1, 2, 3, 2, 3, 3)
MOE_PAIRS = len(MOE_PAIR_LO)
MOE_BUCKETS = MOE_GROUPS * MOE_PAIRS
MOE_TILE = 256
SC_CORES = 2
SC_SUBCORES = 16
SC_GATHER_ROWS = 16


def _pick_tile(n, want):
    t = min(want, n)
    t -= t % ROW_BLOCK
    while n % t:
        t -= ROW_BLOCK
    return t


def _silu(x):
    return x * (1.0 / (1.0 + jnp.exp(-x)))


def _softplus(x):
    return jnp.maximum(x, 0.0) + jnp.log1p(jnp.exp(-jnp.abs(x)))


def _rms_mod(hv, g, shift, scale):
    ms = jnp.mean(hv * hv, axis=-1, keepdims=True)
    return (hv * lax.rsqrt(ms + EPS) * g) * (1.0 + scale) + shift


def _dot(a, b):
    return jnp.dot(a, b, preferred_element_type=F32)


def _nt_dot(a, b):
    return lax.dot_general(a, b, (((1,), (1,)), ((), ())), preferred_element_type=F32)


def _split_bf16(x, parts):
    out = []
    r = x
    for _ in range(parts):
        p = r.astype(BF16)
        out.append(p)
        r = r - p.astype(F32)
    return out


def _row_block(s):
    return pl.ds(pl.multiple_of(s * ROW_BLOCK, ROW_BLOCK), ROW_BLOCK)


def _mods_kernel(c_ref, w_ref, b_ref, o_ref):
    a = _silu(c_ref[...]).astype(BF16)
    o_ref[...] = _dot(a, w_ref[...].astype(BF16)) + b_ref[...]


def _mods(c_all, ada_w, ada_b):
    C, D = c_all.shape
    N = ada_w.shape[1]
    tn = 1024
    return pl.pallas_call(
        _mods_kernel,
        out_shape=jax.ShapeDtypeStruct((C, N), F32),
        grid=(N // tn,),
        in_specs=[pl.BlockSpec((C, D), lambda j: (0, 0)),
                  pl.BlockSpec((D, tn), lambda j: (0, j)),
                  pl.BlockSpec((1, tn), lambda j: (0, j))],
        out_specs=pl.BlockSpec((C, tn), lambda j: (0, j)),
        compiler_params=pltpu.CompilerParams(dimension_semantics=("parallel",), vmem_limit_bytes=40 * MIB),
        name="ada_mods",
    )(c_all, ada_w, ada_b.reshape(1, N))


def _nmm_kernel(*refs, nsb, has_res):
    if has_res:
        b2c_ref, h_ref, res_ref, gate_ref, shift_ref, scale_ref, g_ref, w_ref, o_ref, hnew_ref, xn_ref = refs
    else:
        b2c_ref, h_ref, shift_ref, scale_ref, g_ref, w_ref, o_ref, xn_ref = refs
    i = pl.program_id(0)

    @pl.when(pl.program_id(1) == 0)
    def _():
        def body(s, carry):
            c = b2c_ref[i * nsb + s]
            rows = _row_block(s)
            hv = h_ref[rows, :]
            if has_res:
                hv = hv + gate_ref[c] * res_ref[rows, :]
                hnew_ref[rows, :] = hv
            xn_ref[rows, :] = _rms_mod(hv, g_ref[...], shift_ref[c], scale_ref[c]).astype(BF16)
            return carry
        lax.fori_loop(0, nsb, body, 0)

    o_ref[...] = _dot(xn_ref[...], w_ref[...]).astype(o_ref.dtype)


def _norm_matmul(b2c, h, g, shift, scale, w, *, out_dtype, tn, tm_want, res=None, gate=None):
    T, D = h.shape
    N = w.shape[1]
    C = shift.shape[0]
    tm = _pick_tile(T, tm_want)
    nsb = tm // ROW_BLOCK
    has_res = res is not None
    row_spec = pl.BlockSpec((tm, D), lambda i, j, b: (i, 0))
    mod_spec = pl.BlockSpec((C, 1, D), lambda i, j, b: (0, 0, 0))
    in_specs = [row_spec]
    args = [h]
    if has_res:
        in_specs += [row_spec, mod_spec]
        args += [res, gate]
    in_specs += [mod_spec, mod_spec, pl.BlockSpec((1, D), lambda i, j, b: (0, 0)),
                 pl.BlockSpec((D, tn), lambda i, j, b: (0, j))]
    args += [shift, scale, g.reshape(1, D), w]
    out_shape = [jax.ShapeDtypeStruct((T, N), out_dtype)]
    out_specs = [pl.BlockSpec((tm, tn), lambda i, j, b: (i, j))]
    if has_res:
        out_shape.append(jax.ShapeDtypeStruct((T, D), F32))
        out_specs.append(row_spec)
    n_row_bufs = 3 if has_res else 1
    vmem = (2 * n_row_bufs * tm * D * 4 + tm * D * 2 + 2 * D * tn * 2
            + 2 * tm * tn * jnp.dtype(out_dtype).itemsize + 8 * MIB)
    outs = pl.pallas_call(
        functools.partial(_nmm_kernel, nsb=nsb, has_res=has_res),
        out_shape=out_shape,
        grid_spec=pltpu.PrefetchScalarGridSpec(
            num_scalar_prefetch=1, grid=(T // tm, N // tn), in_specs=in_specs, out_specs=out_specs,
            scratch_shapes=[pltpu.VMEM((tm, D), BF16)]),
        compiler_params=pltpu.CompilerParams(dimension_semantics=("parallel", "arbitrary"),
                                             vmem_limit_bytes=int(vmem)),
        name="norm_matmul_res" if has_res else "norm_matmul",
    )(b2c, *args)
    return outs if has_res else outs[0]


def _mmres_kernel(b2c_ref, a_ref, w_ref, h_ref, gate_ref, o_ref, acc_ref, *, nsb):
    i = pl.program_id(0)
    acc_ref[...] = _dot(a_ref[...], w_ref[...])

    def body(s, carry):
        c = b2c_ref[i * nsb + s]
        rows = _row_block(s)
        o_ref[rows, :] = h_ref[rows, :] + gate_ref[c] * acc_ref[rows, :]
        return carry
    lax.fori_loop(0, nsb, body, 0)


def _matmul_residual(b2c, a, w, h, gate, *, tn, tm_want):
    T, K = a.shape
    D = w.shape[1]
    C = gate.shape[0]
    tm = _pick_tile(T, tm_want)
    nsb = tm // ROW_BLOCK
    vmem = 2 * tm * K * 2 + 2 * K * tn * 2 + 5 * tm * tn * 4 + 8 * MIB
    return pl.pallas_call(
        functools.partial(_mmres_kernel, nsb=nsb),
        out_shape=jax.ShapeDtypeStruct((T, D), F32),
        grid_spec=pltpu.PrefetchScalarGridSpec(
            num_scalar_prefetch=1, grid=(T // tm, D // tn),
            in_specs=[pl.BlockSpec((tm, K), lambda i, j, b: (i, 0)),
                      pl.BlockSpec((K, tn), lambda i, j, b: (0, j)),
                      pl.BlockSpec((tm, tn), lambda i, j, b: (i, j)),
                      pl.BlockSpec((C, 1, tn), lambda i, j, b: (0, 0, j))],
            out_specs=pl.BlockSpec((tm, tn), lambda i, j, b: (i, j)),
            scratch_shapes=[pltpu.VMEM((tm, tn), F32)]),
        compiler_params=pltpu.CompilerParams(dimension_semantics=("parallel", "arbitrary"),
                                             vmem_limit_bytes=int(vmem)),
        name="matmul_residual",
    )(b2c, a, w, h, gate)


def _final_kernel(b2c_ref, h_ref, res_ref, gate_ref, shift_ref, scale_ref, g_ref, o_ref, *, nsb, blk_off):
    i = pl.program_id(0) + blk_off

    def body(s, carry):
        c = b2c_ref[i * nsb + s]
        rows = _row_block(s)
        hv = h_ref[rows, :] + gate_ref[c] * res_ref[rows, :]
        o_ref[rows, :] = _rms_mod(hv, g_ref[...], shift_ref[c], scale_ref[c])
        return carry
    lax.fori_loop(0, nsb, body, 0)


def _final_norm(b2c, h, res, gate, shift, scale, g, *, row_off, n_rows, tm_want):
    T, D = h.shape
    C = shift.shape[0]
    tm = _pick_tile(int(np.gcd(row_off, n_rows)) if row_off else n_rows, tm_want)
    nsb = tm // ROW_BLOCK
    blk_off = row_off // tm
    row_spec = pl.BlockSpec((tm, D), lambda i, b: (i + blk_off, 0))
    mod_spec = pl.BlockSpec((C, 1, D), lambda i, b: (0, 0, 0))
    return pl.pallas_call(
        functools.partial(_final_kernel, nsb=nsb, blk_off=blk_off),
        out_shape=jax.ShapeDtypeStruct((n_rows, D), F32),
        grid_spec=pltpu.PrefetchScalarGridSpec(
            num_scalar_prefetch=1, grid=(n_rows // tm,),
            in_specs=[row_spec, row_spec, mod_spec, mod_spec, mod_spec,
                      pl.BlockSpec((1, D), lambda i, b: (0, 0))],
            out_specs=pl.BlockSpec((tm, D), lambda i, b: (i, 0))),
        compiler_params=pltpu.CompilerParams(dimension_semantics=("parallel",),
                                             vmem_limit_bytes=int(6 * tm * D * 4 + 8 * MIB)),
        name="final_norm",
    )(b2c, h, res, gate, shift, scale, g.reshape(1, D))


def _ssd_kernel(seq_ref, first_ref, last_ref,
                z_ref, x_ref, b_ref, c_ref, dt_ref,
                cpx_ref, cpb_ref, cpc_ref, sprev_ref,
                cwx_ref, cwb_ref, cwc_ref, cbx_ref, cbb_ref, cbc_ref,
                dtb_ref, a_ref, dsk_ref, ng_ref, expand_ref,
                y_ref, cnew_ref, snew_ref,
                extx_ref, extb_ref, extc_ref, st_ref, xs_ref, bm_ref, cm_ref, col_ref, rowa_ref, rowd_ref,
                ybuf_ref):
    Q = ROW_BLOCK
    N = SSM_D_STATE
    HP = x_ref.shape[1]
    GN = b_ref.shape[1]
    G = GN // N
    GW = HP // G
    PAIRS_PER_GROUP = GW // LANES
    k = pl.program_id(0)

    @pl.when(first_ref[k] == 1)
    def _():
        extx_ref[0:CONV_PAD, :] = cpx_ref[0]
        extb_ref[0:CONV_PAD, :] = cpb_ref[0]
        extc_ref[0:CONV_PAD, :] = cpc_ref[0]
        st_ref[...] = sprev_ref[0]

    def conv(ext_ref, cur_ref, w_ref, bias_ref, out_ref, new_off):
        for lo in range(0, cur_ref.shape[1], GW):
            cs = slice(lo, lo + GW)
            cur = cur_ref[:, cs].astype(F32)
            ext_ref[CONV_PAD:CONV_PAD + Q, cs] = cur
            acc = bias_ref[:, cs] + w_ref[CONV_W - 1:CONV_W, cs] * cur
            for j in range(CONV_W - 1):
                off = CONV_PAD - (CONV_W - 1) + j
                acc = acc + w_ref[j:j + 1, cs] * ext_ref[off:off + Q, cs]
            out_ref[:, cs] = _silu(acc).astype(out_ref.dtype)
            tail = ext_ref[Q:Q + CONV_PAD, cs]
            cnew_ref[0, :, new_off + lo:new_off + lo + GW] = tail
            ext_ref[0:CONV_PAD, cs] = tail

    conv(extx_ref, x_ref, cwx_ref, cbx_ref, xs_ref, 0)
    conv(extb_ref, b_ref, cwb_ref, cbb_ref, bm_ref, HP)
    conv(extc_ref, c_ref, cwc_ref, cbc_ref, cm_ref, HP + GN)

    dt = _softplus(dt_ref[...].astype(F32) + dtb_ref[...])
    da = dt * a_ref[...]
    rr = lax.broadcasted_iota(I32, (Q, Q), 0)
    cc = lax.broadcasted_iota(I32, (Q, Q), 1)
    tri = jnp.where(rr >= cc, 1.0, 0.0).astype(BF16)
    cs = _dot(tri, jnp.concatenate(_split_bf16(da, 3), axis=1))
    acs = cs[:, 0:LANES] + cs[:, LANES:2 * LANES] + cs[:, 2 * LANES:3 * LANES]

    both = _split_bf16(jnp.concatenate([acs, dt], axis=0), 2)
    col_ref[...] = _dot(both[0], expand_ref[...]) + _dot(both[1], expand_ref[...])

    pr = lax.broadcasted_iota(I32, (LANES // 2, 2 * LANES), 0)
    pk = lax.broadcasted_iota(I32, (LANES // 2, 2 * LANES), 1)
    esel = jnp.where(pk == jnp.where(pk < LANES, 2 * pr, 2 * pr + 1 + LANES), 1.0, 0.0).astype(BF16)
    zero = jnp.zeros((Q, LANES), BF16)

    def pair_rows(v, parts):
        out = None
        for piece in _split_bf16(v, parts):
            vbd = jnp.concatenate([jnp.concatenate([piece, zero], axis=1),
                                   jnp.concatenate([zero, piece], axis=1)], axis=0)
            r = _nt_dot(esel, vbd)
            out = r if out is None else out + r
        return out

    rowa_ref[...] = pair_rows(acs, 3)
    rowd_ref[...] = pair_rows(dt, 2)

    lane = lax.broadcasted_iota(I32, (Q, LANES), 1)
    trow = lax.broadcasted_iota(I32, (Q, LANES), 0)
    causal = trow >= (lane & (Q - 1))
    lo_half = lane < Q
    er = lax.broadcasted_iota(I32, (N, N), 0)
    ec = lax.broadcasted_iota(I32, (N, N), 1)
    eye = jnp.where(er == ec, 1.0, 0.0).astype(BF16)

    for g in range(G):
        gs = slice(g * GW, (g + 1) * GW)
        bg = bm_ref[:, g * N:(g + 1) * N]
        cg = cm_ref[:, g * N:(g + 1) * N]
        cb2 = _nt_dot(cg, jnp.concatenate([bg, bg], axis=0))
        y_state = _dot(cg, st_ref[:, gs].astype(BF16))
        for pp in range(PAIRS_PER_GROUP):
            p = g * PAIRS_PER_GROUP + pp
            ps = slice(p * LANES, (p + 1) * LANES)
            col = col_ref[0:Q, ps]
            seg = col - rowa_ref[p:p + 1, :]
            decay = jnp.exp(jnp.where(causal, seg, -jnp.inf))
            lmat = (cb2 * decay * rowd_ref[p:p + 1, :]).astype(BF16)
            xp = xs_ref[:, ps]
            xbd = jnp.concatenate([jnp.where(lo_half, xp, 0.0), jnp.where(lo_half, 0.0, xp)],
                                  axis=0).astype(BF16)
            y_in = _dot(lmat, xbd)
            ybuf_ref[:, ps] = (y_in + y_state[:, pp * LANES:(pp + 1) * LANES] * jnp.exp(col)
                               + dsk_ref[:, ps] * xp)
        colg = col_ref[0:Q, gs]
        last = col_ref[Q - 1:Q, gs]
        w_end = col_ref[Q:2 * Q, gs] * jnp.exp(last - colg)
        xw = (xs_ref[:, gs] * w_end).astype(BF16)
        bgt = _nt_dot(eye, bg).astype(BF16)
        st_ref[:, gs] = st_ref[:, gs] * jnp.exp(last) + _dot(bgt, xw)

    sq = jnp.zeros((Q, LANES), F32)
    for lo in range(0, HP, GW):
        cs = slice(lo, lo + GW)
        yv = ybuf_ref[:, cs] * _silu(z_ref[:, cs].astype(F32))
        ybuf_ref[:, cs] = yv
        y2 = yv * yv
        for c in range(GW // LANES):
            sq = sq + y2[:, c * LANES:(c + 1) * LANES]
    inv = lax.rsqrt(jnp.sum(sq, axis=-1, keepdims=True) * (1.0 / HP) + EPS)
    for lo in range(0, HP, GW):
        cs = slice(lo, lo + GW)
        y_ref[:, cs] = (ybuf_ref[:, cs] * inv * ng_ref[:, cs]).astype(BF16)

    @pl.when(last_ref[k] == 1)
    def _():
        snew_ref[0] = st_ref[...]


def _ssd(tabs, zx, cprev, sprev, conv_w, conv_b, dt_bias, a_neg, d_skip, norm_g, expand):
    T = zx.shape[0]
    S, N, HP = sprev.shape
    GN = SSM_GROUPS * N
    Q = ROW_BLOCK
    nblk = T // Q
    bb, cb = HP // GN, HP // GN + 1
    zb = (2 * HP) // GN

    def rows(width, col_blk):
        return pl.BlockSpec((Q, width), lambda k, s, f, l: (k, col_blk))

    def per_seq(shape, col_blk):
        return pl.BlockSpec((1,) + shape, lambda k, s, f, l: (s[k], 0, col_blk))

    def const(shape, col_blk=0):
        return pl.BlockSpec(shape, lambda k, s, f, l: (0, col_blk))

    in_specs = [
        rows(HP, 0), rows(HP, 1), rows(GN, zb), rows(GN, zb + 1), rows(LANES, (2 * HP + 2 * GN) // LANES),
        per_seq((CONV_PAD, HP), 0), per_seq((CONV_PAD, GN), bb), per_seq((CONV_PAD, GN), cb),
        per_seq((N, HP), 0),
        const((CONV_W, HP)), const((CONV_W, GN), bb), const((CONV_W, GN), cb),
        const((1, HP)), const((1, GN), bb), const((1, GN), cb),
        const((1, LANES)), const((1, LANES)), const((1, HP)), const((1, HP)), const((LANES, HP)),
    ]
    out_shape = [jax.ShapeDtypeStruct((T, HP), BF16),
                 jax.ShapeDtypeStruct((S, CONV_PAD, HP + 2 * GN), F32),
                 jax.ShapeDtypeStruct((S, N, HP), F32)]
    out_specs = [pl.BlockSpec((Q, HP), lambda k, s, f, l: (k, 0)),
                 pl.BlockSpec((1, CONV_PAD, HP + 2 * GN), lambda k, s, f, l: (s[k], 0, 0)),
                 pl.BlockSpec((1, N, HP), lambda k, s, f, l: (s[k], 0, 0))]
    scratch = [pltpu.VMEM((Q + CONV_PAD, HP), F32), pltpu.VMEM((Q + CONV_PAD, GN), F32),
               pltpu.VMEM((Q + CONV_PAD, GN), F32), pltpu.VMEM((N, HP), F32), pltpu.VMEM((Q, HP), F32),
               pltpu.VMEM((Q, GN), BF16), pltpu.VMEM((Q, GN), BF16), pltpu.VMEM((2 * Q, HP), F32),
               pltpu.VMEM((LANES // 2, 2 * Q), F32), pltpu.VMEM((LANES // 2, 2 * Q), F32),
               pltpu.VMEM((Q, HP), F32)]
    return pl.pallas_call(
        _ssd_kernel,
        out_shape=out_shape,
        grid_spec=pltpu.PrefetchScalarGridSpec(num_scalar_prefetch=3, grid=(nblk,), in_specs=in_specs,
                                               out_specs=out_specs, scratch_shapes=scratch),
        compiler_params=pltpu.CompilerParams(dimension_semantics=("arbitrary",), vmem_limit_bytes=48 * MIB),
        name="ssd_mixer",
    )(tabs["seq"], tabs["first"], tabs["last"],
      zx, zx, zx, zx, zx, cprev, cprev, cprev, sprev,
      conv_w, conv_w, conv_w, conv_b, conv_b, conv_b, dt_bias, a_neg, d_skip, norm_g, expand)


def _attn_kernel(wk0_ref, wk1_ref, wc0_ref, wc1_ref, uc0_ref, uc1_ref, ninv_ref,
                 q_ref, k0_ref, k1_ref, k2_ref, c0_ref, c1_ref, bias_ref, sink_ref, o_ref, sc_ref, p_ref):
    Q = ROW_BLOCK
    DH = ATT_HEAD_DIM
    KVW = k2_ref.shape[1] // 2
    n_kv = KVW // DH
    NK = bias_ref.shape[2]
    k = pl.program_id(0)
    s = lax.broadcasted_iota(I32, (Q, NK), 1)
    mask_bias = jnp.where(s >= ninv_ref[k] * Q, 0.0, -jnp.inf)
    kv = jnp.concatenate([jnp.where(uc0_ref[k] == 1, c0_ref[...], k0_ref[...]),
                          jnp.where(uc1_ref[k] == 1, c1_ref[...], k1_ref[...]), k2_ref[...]], axis=0).astype(BF16)

    for g in range(n_kv):
        qg = jnp.concatenate([q_ref[:, (g * ATT_GROUP + r) * DH:(g * ATT_GROUP + r + 1) * DH]
                              for r in range(ATT_GROUP)], axis=0)
        sc_ref[g] = _nt_dot(qg, kv[:, g * DH:(g + 1) * DH])
    sink_terms = []
    for g in range(n_kv):
        for r in range(ATT_GROUP):
            rows = slice(r * Q, (r + 1) * Q)
            sc = sc_ref[g, rows, :] * (DH ** -0.5) + bias_ref[g, rows, :] + mask_bias
            sink = sink_ref[g, rows, 0:1]
            m = jnp.maximum(jnp.max(sc, axis=-1, keepdims=True), sink)
            p_ref[g, rows, :] = jnp.exp(sc - m).astype(BF16)
            sink_terms.append(jnp.exp(sink - m))
    first_half = lax.broadcasted_iota(I32, (Q, 2 * DH), 1) < DH
    ones = jnp.ones((NK, 2 * DH), BF16)
    for g in range(n_kv):
        vg = kv[:, KVW + g * DH:KVW + (g + 1) * DH]
        og = _dot(p_ref[g], jnp.concatenate([vg, vg], axis=1))
        dn = _dot(p_ref[g], ones)
        for rp in range(ATT_GROUP // 2):
            ev, od = slice(2 * rp * Q, (2 * rp + 1) * Q), slice((2 * rp + 1) * Q, (2 * rp + 2) * Q)
            s_ev, s_od = sink_terms[g * ATT_GROUP + 2 * rp], sink_terms[g * ATT_GROUP + 2 * rp + 1]
            pair = jnp.where(first_half, og[ev] * (1.0 / (dn[ev] + s_ev)), og[od] * (1.0 / (dn[od] + s_od)))
            h0 = g * ATT_GROUP + 2 * rp
            o_ref[:, h0 * DH:(h0 + 2) * DH] = pair.astype(BF16)


def _attention(tabs, q, kv, cache_kv, sinks):
    T, HD = q.shape
    Q = ROW_BLOCK
    NK = (WINDOW_BLOCKS + 1) * Q
    n_heads = HD // ATT_HEAD_DIM
    KV2 = kv.shape[1]
    n_kv = KV2 // (2 * ATT_HEAD_DIM)
    GQ = ATT_GROUP * Q
    slopes = 2.0 ** (-8.0 * np.arange(1, n_heads + 1) / n_heads)
    dist = np.abs(np.arange(Q)[:, None] + WINDOW_BLOCKS * Q - np.arange(NK)[None, :])
    bias = jnp.asarray((-slopes[:, None, None] * dist[None]).reshape(n_kv, GQ, NK).astype(np.float32))
    sink_rows = jnp.broadcast_to(jnp.repeat(sinks.reshape(n_kv, ATT_GROUP), Q, axis=1)[:, :, None],
                                 (n_kv, GQ, LANES))

    def blk(pick):
        return pl.BlockSpec((Q, KV2), lambda k, wk0, wk1, wc0, wc1, uc0, uc1, n: (pick(k, wk0, wk1, wc0, wc1), 0))

    def const(shape):
        return pl.BlockSpec(shape, lambda k, *_: (0, 0, 0))

    return pl.pallas_call(
        _attn_kernel,
        out_shape=jax.ShapeDtypeStruct((T, HD), BF16),
        grid_spec=pltpu.PrefetchScalarGridSpec(
            num_scalar_prefetch=7, grid=(T // Q,),
            in_specs=[pl.BlockSpec((Q, HD), lambda k, *_: (k, 0)),
                      blk(lambda k, wk0, wk1, wc0, wc1: wk0[k]), blk(lambda k, wk0, wk1, wc0, wc1: wk1[k]),
                      blk(lambda k, wk0, wk1, wc0, wc1: k),
                      blk(lambda k, wk0, wk1, wc0, wc1: wc0[k]), blk(lambda k, wk0, wk1, wc0, wc1: wc1[k]),
                      const((n_kv, GQ, NK)), const((n_kv, GQ, LANES))],
            out_specs=pl.BlockSpec((Q, HD), lambda k, *_: (k, 0)),
            scratch_shapes=[pltpu.VMEM((n_kv, GQ, NK), F32), pltpu.VMEM((n_kv, GQ, NK), BF16)]),
        compiler_params=pltpu.CompilerParams(dimension_semantics=("parallel",), vmem_limit_bytes=32 * MIB),
        name="swa_attention",
    )(tabs["wk0"], tabs["wk1"], tabs["wc0"], tabs["wc1"], tabs["uc0"], tabs["uc1"], tabs["ninv"],
      q, kv, kv, kv, cache_kv, cache_kv, bias, sink_rows)


def _router_kernel(b2c_ref, h_ref, shift_ref, scale_ref, g_ref, wh_ref, wl_ref, rb_ref, hn_ref, meta_ref,
                   xh_ref, xl_ref, wt_ref, *, nsb):
    i = pl.program_id(0)

    def body(s, carry):
        c = b2c_ref[i * nsb + s]
        rows = _row_block(s)
        xn = _rms_mod(h_ref[rows, :], g_ref[...], shift_ref[c], scale_ref[c])
        hn_ref[rows, 0:h_ref.shape[1]] = xn
        hi, lo = _split_bf16(xn, 2)
        xh_ref[rows, :] = hi
        xl_ref[rows, :] = lo
        return carry
    lax.fori_loop(0, nsb, body, 0)

    lt = (_nt_dot(wh_ref[...], xh_ref[...]) + _nt_dot(wh_ref[...], xl_ref[...])
          + _nt_dot(wl_ref[...], xh_ref[...]) + rb_ref[...])

    def first_max(vals):
        m = vals[0]
        for v in vals[1:]:
            m = jnp.maximum(m, v)
        idx = jnp.full(m.shape, len(vals) - 1, I32)
        for j in range(len(vals) - 2, -1, -1):
            idx = jnp.where(vals[j] == m, j, idx)
        return m, idx

    lg = [lt[j:j + 1, :] for j in range(MOE_GROUPS)]
    mg, gi = first_max(lg)
    p_sel = 1.0 / sum(jnp.exp(v - mg) for v in lg)
    le = []
    for j in range(MOE_PER_GROUP):
        v = lt[MOE_GROUPS + (MOE_GROUPS - 1) * MOE_PER_GROUP + j:MOE_GROUPS + (MOE_GROUPS - 1) * MOE_PER_GROUP + j + 1, :]
        for grp in range(MOE_GROUPS - 2, -1, -1):
            row = MOE_GROUPS + grp * MOE_PER_GROUP + j
            v = jnp.where(gi == grp, lt[row:row + 1, :], v)
        le.append(v)
    m1, i1 = first_max(le)
    m2, i2 = first_max([jnp.where(i1 == j, -jnp.inf, le[j]) for j in range(MOE_PER_GROUP)])
    e2 = jnp.exp(m2 - m1)
    w1 = p_sel * (1.0 / (1.0 + e2))
    w2 = p_sel * (e2 / (1.0 + e2))
    lo_i = jnp.minimum(i1, i2)
    hi_i = jnp.maximum(i1, i2)
    first_is_lo = i1 < i2
    pair_base = jnp.where(lo_i == 0, 0, jnp.where(lo_i == 1, 3, 5))
    bucket = gi * MOE_PAIRS + pair_base + hi_i - lo_i - 1
    meta_ref[0:1, :] = bucket.astype(F32)
    meta_ref[1:8, :] = jnp.zeros((7, meta_ref.shape[1]), F32)
    wt_ref[...] = jnp.zeros(wt_ref.shape, F32)
    wt_ref[0:1, :] = jnp.where(first_is_lo, w1, w2)
    wt_ref[1:2, :] = jnp.where(first_is_lo, w2, w1)
    D = h_ref.shape[1]
    hn_ref[:, D:D + LANES] = wt_ref[...].T


def _router(b2c, h, g, shift, scale, w_hi, w_lo, rbias, *, tm_want):
    T, D = h.shape
    C = shift.shape[0]
    tm = _pick_tile(T, tm_want)
    nsb = tm // ROW_BLOCK
    row_spec = pl.BlockSpec((tm, D), lambda i, b: (i, 0))
    mod_spec = pl.BlockSpec((C, 1, D), lambda i, b: (0, 0, 0))
    rb = jnp.broadcast_to(rbias.reshape(LANES, 1), (LANES, tm))
    return pl.pallas_call(
        functools.partial(_router_kernel, nsb=nsb),
        out_shape=[jax.ShapeDtypeStruct((T, D + LANES), F32), jax.ShapeDtypeStruct((8, T), F32)],
        grid_spec=pltpu.PrefetchScalarGridSpec(
            num_scalar_prefetch=1, grid=(T // tm,),
            in_specs=[row_spec, mod_spec, mod_spec, pl.BlockSpec((1, D), lambda i, b: (0, 0)),
                      pl.BlockSpec((LANES, D), lambda i, b: (0, 0)), pl.BlockSpec((LANES, D), lambda i, b: (0, 0)),
                      pl.BlockSpec((LANES, tm), lambda i, b: (0, 0))],
            out_specs=[pl.BlockSpec((tm, D + LANES), lambda i, b: (i, 0)), pl.BlockSpec((8, tm), lambda i, b: (0, i))],
            scratch_shapes=[pltpu.VMEM((tm, D), BF16), pltpu.VMEM((tm, D), BF16), pltpu.VMEM((LANES, tm), F32)]),
        compiler_params=pltpu.CompilerParams(dimension_semantics=("parallel",),
                                             vmem_limit_bytes=int(5 * tm * D * 4 + 12 * MIB)),
        name="moe_router",
    )(b2c, h, shift, scale, g.reshape(1, D), w_hi, w_lo, rb)


def _gather_rows(src, idx):
    n = idx.shape[0]
    D = src.shape[1]
    n_workers = SC_CORES * SC_SUBCORES
    per_worker = n // n_workers
    n_chunks = per_worker // SC_GATHER_ROWS
    assert n == n_workers * n_chunks * SC_GATHER_ROWS, (n, n_workers, SC_GATHER_ROWS)
    mesh = plsc.VectorSubcoreMesh(core_axis_name="c", subcore_axis_name="s", num_cores=SC_CORES,
                                  num_subcores=SC_SUBCORES)

    @functools.partial(
        pl.kernel, mesh=mesh, out_type=jax.ShapeDtypeStruct((n, D), src.dtype),
        scratch_types=[pltpu.VMEM((SC_GATHER_ROWS,), I32), pltpu.VMEM((SC_GATHER_ROWS, D), src.dtype),
                       pltpu.SemaphoreType.DMA],
        name="sc_row_gather")
    def gather(src_hbm, idx_hbm, out_hbm, idx_v, rows_v, sem):
        base = (lax.axis_index("s") * SC_CORES + lax.axis_index("c")) * per_worker

        @pl.loop(0, n_chunks)
        def _(j):
            off = base + j * SC_GATHER_ROWS
            pltpu.sync_copy(idx_hbm.at[pl.ds(off, SC_GATHER_ROWS)], idx_v)
            pltpu.async_copy(src_hbm.at[idx_v], rows_v, sem).wait()
            pltpu.sync_copy(rows_v, out_hbm.at[pl.ds(off, SC_GATHER_ROWS)])

    return gather(src, idx)


def _ffn_kernel(ea_ref, eb_ref, nv_ref, x_ref, wga_ref, wua_ref, wda_ref, wgb_ref, wub_ref, wdb_ref, y_ref):
    i = pl.program_id(0)
    D = y_ref.shape[1]

    @pl.when(nv_ref[i] > 0)
    def _():
        x = x_ref[:, 0:D].astype(BF16)

        def expert(wg_ref, wu_ref, wd_ref):
            hid = _silu(_dot(x, wg_ref[0])) * _dot(x, wu_ref[0])
            return _dot(hid.astype(BF16), wd_ref[0])
        y_ref[...] = (x_ref[:, D:D + 1] * expert(wga_ref, wua_ref, wda_ref)
                      + x_ref[:, D + 1:D + 2] * expert(wgb_ref, wub_ref, wdb_ref))

    @pl.when(nv_ref[i] == 0)
    def _():
        y_ref[...] = jnp.zeros(y_ref.shape, F32)


def _moe_ffn(plan, xs, w_g, w_u, w_d):
    R, DX = xs.shape
    tm = MOE_TILE
    E, F, D = w_d.shape
    up_a = pl.BlockSpec((1, D, F), lambda i, a, b, n: (a[i], 0, 0))
    up_b = pl.BlockSpec((1, D, F), lambda i, a, b, n: (b[i], 0, 0))
    return pl.pallas_call(
        _ffn_kernel,
        out_shape=jax.ShapeDtypeStruct((R, D), F32),
        grid_spec=pltpu.PrefetchScalarGridSpec(
            num_scalar_prefetch=3, grid=(R // tm,),
            in_specs=[pl.BlockSpec((tm, DX), lambda i, a, b, n: (i, 0)),
                      up_a, up_a, pl.BlockSpec((1, F, D), lambda i, a, b, n: (a[i], 0, 0)),
                      up_b, up_b, pl.BlockSpec((1, F, D), lambda i, a, b, n: (b[i], 0, 0))],
            out_specs=pl.BlockSpec((tm, D), lambda i, a, b, n: (i, 0))),
        compiler_params=pltpu.CompilerParams(dimension_semantics=("arbitrary",), vmem_limit_bytes=48 * MIB),
        name="moe_experts",
    )(plan["ea"], plan["eb"], plan["nvalid"], xs, w_g, w_u, w_d, w_g, w_u, w_d)


def _moe_plan(meta, T):
    tm = MOE_TILE
    n_tiles = -(-T // tm) + MOE_BUCKETS
    bucket = meta[0].astype(I32)
    onehot = (bucket[:, None] == jnp.arange(MOE_BUCKETS, dtype=I32)[None, :]).astype(F32)
    blocks = onehot.reshape(T // LANES, LANES, MOE_BUCKETS)
    within = jnp.einsum("ij,bjk->bik", jnp.tril(jnp.ones((LANES, LANES), F32)), blocks)
    totals = within[:, -1, :]
    cum = (within + (jnp.cumsum(totals, axis=0) - totals)[:, None, :]).reshape(T, MOE_BUCKETS)
    rank = jnp.sum(cum * onehot, axis=1).astype(I32) - 1
    counts = jnp.sum(totals, axis=0).astype(I32)
    ntile_b = (counts + tm - 1) // tm
    tend_b = jnp.cumsum(ntile_b)
    tstart_b = tend_b - ntile_b
    dest = tstart_b[bucket] * tm + rank
    tok_of_pos = jnp.zeros((n_tiles * tm,), I32).at[dest].set(jnp.arange(T, dtype=I32))
    tiles = jnp.arange(n_tiles, dtype=I32)
    tile_b = jnp.minimum(jnp.searchsorted(tend_b, tiles, side="right").astype(I32), MOE_BUCKETS - 1)
    nvalid = jnp.where(tiles < tend_b[-1],
                       jnp.clip(counts[tile_b] - (tiles - tstart_b[tile_b]) * tm, 0, tm), 0).astype(I32)
    grp = tile_b // MOE_PAIRS
    pair = tile_b % MOE_PAIRS
    ea = grp * MOE_PER_GROUP + jnp.asarray(MOE_PAIR_LO, I32)[pair]
    eb = grp * MOE_PER_GROUP + jnp.asarray(MOE_PAIR_HI, I32)[pair]
    return dict(dest=dest, tok_of_pos=tok_of_pos, nvalid=nvalid, ea=ea.astype(I32), eb=eb.astype(I32))


def _hmoe(b2c, h, g, shift, scale, router, w_g, w_u, w_d):
    T = h.shape[0]
    hn, meta = _router(b2c, h, g, shift, scale, *router, tm_want=512)
    plan = _moe_plan(meta, T)
    xs = _gather_rows(hn, plan["tok_of_pos"])
    ys = _moe_ffn(plan, xs, w_g, w_u, w_d)
    return _gather_rows(ys, plan["dest"])


def _tables(Bp, Lp, Bs, Ls):
    nbp, nbs = Lp // ROW_BLOCK, Ls // ROW_BLOCK
    cols = dict(seq=[], first=[], last=[], ninv=[], wk0=[], wk1=[], wc0=[], wc1=[], uc0=[], uc1=[])
    k = 0
    for b in range(Bp + Bs):
        nb = nbp if b < Bp else nbs
        for c in range(nb):
            cols["seq"].append(b)
            cols["first"].append(int(c == 0))
            cols["last"].append(int(c == nb - 1))
            cols["ninv"].append(max(0, WINDOW_BLOCKS - c) if b < Bp else 0)
            for j in range(WINDOW_BLOCKS):
                hist = c - WINDOW_BLOCKS + j
                from_cache = b >= Bp and hist < 0
                cols[f"uc{j}"].append(int(from_cache))
                cols[f"wc{j}"].append((b - Bp) * WINDOW_BLOCKS + WINDOW_BLOCKS + hist if from_cache else 0)
                cols[f"wk{j}"].append(k if from_cache else max(k - WINDOW_BLOCKS + j, 0))
            k += 1
    return {name: jnp.asarray(np.asarray(v, np.int32)) for name, v in cols.items()}


def _router_weights(router_g, bias_g, router_e, bias_e):
    D = router_g.shape[0]
    w = jnp.zeros((LANES, D), F32)
    w = w.at[:MOE_GROUPS].set(router_g.T).at[MOE_GROUPS:MOE_GROUPS + router_e.shape[1]].set(router_e.T)
    w_hi = w.astype(BF16)
    w_lo = (w - w_hi.astype(F32)).astype(BF16)
    rb = jnp.zeros((LANES,), F32).at[:MOE_GROUPS].set(bias_g).at[MOE_GROUPS:MOE_GROUPS + bias_e.shape[0]].set(bias_e)
    return w_hi, w_lo, rb


def kernel(x_prompt, x_sample, state_conv, state_ssm, cache_k, cache_v, c_prompt, c_sample, ada_w, ada_b, norm_mix, norm_ffn, norm_kv, norm_out, ssm_w_in, ssm_conv_w, ssm_conv_b, ssm_dt_bias, ssm_a_log, ssm_d, ssm_norm, ssm_w_out, attn_w_kv, attn_w_q, attn_sinks, attn_w_o, moe_router_g, moe_bias_g, moe_router_e, moe_bias_e, moe_w_gate, moe_w_up, moe_w_down):
    Bp, Lp, D = x_prompt.shape
    Bs, Ls, _ = x_sample.shape
    Tp, Ts = Bp * Lp, Bs * Ls
    T = Tp + Ts
    C = Bp + Bs
    tabs = _tables(Bp, Lp, Bs, Ls)
    b2c = tabs["seq"]

    n_mod = ada_w.shape[1] // D
    mods = _mods(jnp.concatenate([c_prompt, c_sample], axis=0), ada_w, ada_b).reshape(C, n_mod, D)
    mod = lambda i: mods[:, i:i + 1, :]
    x = jnp.concatenate([x_prompt.reshape(Tp, D), x_sample.reshape(Ts, D)], axis=0)

    HP = ssm_w_out.shape[1]
    H = HP // SSM_HEAD_DIM
    N = SSM_D_STATE
    GN = SSM_GROUPS * N
    conv_dim = HP + 2 * GN
    w_in = jnp.pad(ssm_w_in[0], ((0, 0), (0, LANES - H))).astype(BF16)
    zx = _norm_matmul(b2c, x, norm_mix[0], mod(0), mod(1), w_in, out_dtype=BF16,
                      tn=(2 * HP + 2 * GN + LANES) // 9, tm_want=1024)
    cprev = jnp.zeros((C, CONV_PAD, conv_dim), F32).at[Bp:, CONV_PAD - (CONV_W - 1):].set(state_conv[0])
    sprev = jnp.concatenate([jnp.zeros((Bp, N, HP), F32),
                             jnp.transpose(state_ssm[0], (0, 3, 1, 2)).reshape(Bs, N, HP)], axis=0)
    pad_h = lambda v: jnp.pad(v.astype(F32), (0, LANES - H)).reshape(1, LANES)
    expand = (jnp.arange(LANES)[:, None] == (jnp.arange(HP) // SSM_HEAD_DIM)[None, :]).astype(BF16)
    y_ssd, cnew, snew = _ssd(
        tabs, zx, cprev, sprev, ssm_conv_w[0], ssm_conv_b[0].reshape(1, conv_dim), pad_h(ssm_dt_bias[0]),
        pad_h(-jnp.exp(ssm_a_log[0].astype(F32))), jnp.repeat(ssm_d[0].astype(F32), SSM_HEAD_DIM).reshape(1, HP),
        ssm_norm[0].reshape(1, HP), expand)
    h = _matmul_residual(b2c, y_ssd, ssm_w_out[0].astype(BF16), x, mod(2), tn=512, tm_want=1024)

    def experts(layer):
        return moe_w_gate[layer].astype(BF16), moe_w_up[layer].astype(BF16), moe_w_down[layer].astype(BF16)

    def router(layer):
        return _router_weights(moe_router_g[layer], moe_bias_g[layer], moe_router_e[layer], moe_bias_e[layer])

    moe0 = _hmoe(b2c, h, norm_ffn[0], mod(3), mod(4), router(0), *experts(0))

    kv, h = _norm_matmul(b2c, h, norm_kv, mod(12), mod(13), attn_w_kv.astype(BF16), out_dtype=F32,
                         tn=attn_w_kv.shape[1], tm_want=512, res=moe0, gate=mod(5))
    KV2 = kv.shape[1]
    KVW = KV2 // 2
    W = WINDOW_BLOCKS * ROW_BLOCK
    cache_kv = jnp.concatenate([cache_k.reshape(Bs, W, KVW), cache_v.reshape(Bs, W, KVW)], axis=-1)
    kvp = kv[:Tp].reshape(Bp, Lp, KV2)
    kvs = jnp.concatenate([cache_kv, kv[Tp:].reshape(Bs, Ls, KV2)], axis=1)
    q = _norm_matmul(b2c, h, norm_mix[1], mod(6), mod(7), attn_w_q[0].astype(BF16), out_dtype=BF16,
                     tn=1024, tm_want=1024)
    o = _attention(tabs, q, kv, cache_kv.reshape(Bs * W, KV2), attn_sinks[0].astype(F32))
    h = _matmul_residual(b2c, o, attn_w_o[0].astype(BF16), h, mod(8), tn=1024, tm_want=1024)
    moe1 = _hmoe(b2c, h, norm_ffn[1], mod(9), mod(10), router(1), *experts(1))

    fin = functools.partial(_final_norm, b2c, h, moe1, mod(11), mod(14), mod(15), norm_out, tm_want=512)
    y_prompt = fin(row_off=0, n_rows=Tp).reshape(Bp, Lp, D)
    y_sample = fin(row_off=Tp, n_rows=Ts).reshape(Bs, Ls, D)

    kv_heads = KVW // ATT_HEAD_DIM
    tail = lambda a, lo: a[:, -W:, lo:lo + KVW].reshape(a.shape[0], W, kv_heads, ATT_HEAD_DIM)
    conv_tail = cnew[:, CONV_PAD - (CONV_W - 1):]
    ssm_new = jnp.transpose(snew.reshape(C, N, H, SSM_HEAD_DIM), (0, 2, 3, 1))
    return (y_prompt, y_sample, conv_tail[None, :Bp], ssm_new[None, :Bp], tail(kvp, 0), tail(kvp, KVW),
            conv_tail[None, Bp:], ssm_new[None, Bp:], tail(kvs, 0), tail(kvs, KVW))
```

```python
import functools

import numpy as np
import jax
import jax.numpy as jnp
from jax import lax
from jax.experimental import pallas as pl
from jax.experimental.pallas import tpu as pltpu
from jax.experimental.pallas import tpu_sc as plsc

F32 = jnp.float32
BF16 = jnp.bfloat16
I32 = jnp.int32
EPS = 1e-6
ROW_BLOCK = 64
WINDOW_BLOCKS = 2
LANES = 128
MIB = 1024 * 1024

SSM_HEAD_DIM = 64
SSM_GROUPS = 8
SSM_D_STATE = 128
CONV_W = 4
CONV_PAD = 8
ATT_HEAD_DIM = 64
ATT_GROUP = 8
MOE_GROUPS = 4
MOE_PER_GROUP = 4
MOE_PAIR_LO = (0, 0, 0, 1, 1, 2)
MOE_PAIR_HI = (1, 2, 3, 2, 3, 3)
MOE_PAIRS = len(MOE_PAIR_LO)
MOE_BUCKETS = MOE_GROUPS * MOE_PAIRS
MOE_TILE = 256
SC_CORES = 2
SC_SUBCORES = 16
SC_GATHER_ROWS = 16


def _pick_tile(n, want):
    t = min(want, n)
    t -= t % ROW_BLOCK
    while n % t:
        t -= ROW_BLOCK
    return t


def _silu(x):
    return (0.5 * x) * (1.0 + jnp.tanh(0.5 * x))


def _softplus(x):
    return jnp.maximum(x, 0.0) + jnp.log1p(jnp.exp(-jnp.abs(x)))


def _rms_mod(hv, g, shift, scale):
    ms = jnp.mean(hv * hv, axis=-1, keepdims=True)
    return (hv * lax.rsqrt(ms + EPS) * g) * (1.0 + scale) + shift


def _dot(a, b):
    return jnp.dot(a, b, preferred_element_type=F32)


def _nt_dot(a, b):
    return lax.dot_general(a, b, (((1,), (1,)), ((), ())), preferred_element_type=F32)


def _split_bf16(x, parts):
    out = []
    r = x
    for _ in range(parts):
        p = r.astype(BF16)
        out.append(p)
        r = r - p.astype(F32)
    return out


def _row_block(s):
    return pl.ds(pl.multiple_of(s * ROW_BLOCK, ROW_BLOCK), ROW_BLOCK)


def _mods_kernel(c_ref, w_ref, b_ref, o_ref):
    a = _silu(c_ref[...]).astype(BF16)
    o_ref[...] = _dot(a, w_ref[...].astype(BF16)) + b_ref[...]


def _mods(c_all, ada_w, ada_b):
    C, D = c_all.shape
    N = ada_w.shape[1]
    tn = 1024
    return pl.pallas_call(
        _mods_kernel,
        out_shape=jax.ShapeDtypeStruct((C, N), F32),
        grid=(N // tn,),
        in_specs=[pl.BlockSpec((C, D), lambda j: (0, 0)),
                  pl.BlockSpec((D, tn), lambda j: (0, j)),
                  pl.BlockSpec((1, tn), lambda j: (0, j))],
        out_specs=pl.BlockSpec((C, tn), lambda j: (0, j)),
        compiler_params=pltpu.CompilerParams(dimension_semantics=("parallel",), vmem_limit_bytes=40 * MIB),
        name="ada_mods",
    )(c_all, ada_w, ada_b.reshape(1, N))


def _nmm_kernel(*refs, nsb, has_res):
    if has_res:
        b2c_ref, h_ref, res_ref, gate_ref, shift_ref, scale_ref, g_ref, w_ref, o_ref, hnew_ref, xn_ref = refs
    else:
        b2c_ref, h_ref, shift_ref, scale_ref, g_ref, w_ref, o_ref, xn_ref = refs
    i = pl.program_id(0)

    @pl.when(pl.program_id(1) == 0)
    def _():
        def body(s, carry):
            c = b2c_ref[i * nsb + s]
            rows = _row_block(s)
            hv = h_ref[rows, :]
            if has_res:
                hv = hv + gate_ref[c] * res_ref[rows, :]
                hnew_ref[rows, :] = hv
            xn_ref[rows, :] = _rms_mod(hv, g_ref[...], shift_ref[c], scale_ref[c]).astype(BF16)
            return carry
        lax.fori_loop(0, nsb, body, 0)

    o_ref[...] = _dot(xn_ref[...], w_ref[...]).astype(o_ref.dtype)


def _norm_matmul(b2c, h, g, shift, scale, w, *, out_dtype, tn, tm_want, res=None, gate=None):
    T, D = h.shape
    N = w.shape[1]
    C = shift.shape[0]
    tm = _pick_tile(T, tm_want)
    nsb = tm // ROW_BLOCK
    has_res = res is not None
    row_spec = pl.BlockSpec((tm, D), lambda i, j, b: (i, 0))
    mod_spec = pl.BlockSpec((C, 1, D), lambda i, j, b: (0, 0, 0))
    in_specs = [row_spec]
    args = [h]
    if has_res:
        in_specs += [row_spec, mod_spec]
        args += [res, gate]
    in_specs += [mod_spec, mod_spec, pl.BlockSpec((1, D), lambda i, j, b: (0, 0)),
                 pl.BlockSpec((D, tn), lambda i, j, b: (0, j))]
    args += [shift, scale, g.reshape(1, D), w]
    out_shape = [jax.ShapeDtypeStruct((T, N), out_dtype)]
    out_specs = [pl.BlockSpec((tm, tn), lambda i, j, b: (i, j))]
    if has_res:
        out_shape.append(jax.ShapeDtypeStruct((T, D), F32))
        out_specs.append(row_spec)
    n_row_bufs = 3 if has_res else 1
    vmem = (2 * n_row_bufs * tm * D * 4 + tm * D * 2 + 2 * D * tn * 2
            + 2 * tm * tn * jnp.dtype(out_dtype).itemsize + 8 * MIB)
    outs = pl.pallas_call(
        functools.partial(_nmm_kernel, nsb=nsb, has_res=has_res),
        out_shape=out_shape,
        grid_spec=pltpu.PrefetchScalarGridSpec(
            num_scalar_prefetch=1, grid=(T // tm, N // tn), in_specs=in_specs, out_specs=out_specs,
            scratch_shapes=[pltpu.VMEM((tm, D), BF16)]),
        compiler_params=pltpu.CompilerParams(dimension_semantics=("parallel", "arbitrary"),
                                             vmem_limit_bytes=int(vmem)),
        name="norm_matmul_res" if has_res else "norm_matmul",
    )(b2c, *args)
    return outs if has_res else outs[0]


def _mmres_kernel(b2c_ref, a_ref, w_ref, h_ref, gate_ref, o_ref, acc_ref, *, nsb):
    i = pl.program_id(0)
    acc_ref[...] = _dot(a_ref[...], w_ref[...])

    def body(s, carry):
        c = b2c_ref[i * nsb + s]
        rows = _row_block(s)
        o_ref[rows, :] = h_ref[rows, :] + gate_ref[c] * acc_ref[rows, :]
        return carry
    lax.fori_loop(0, nsb, body, 0)


def _matmul_residual(b2c, a, w, h, gate, *, tn, tm_want):
    T, K = a.shape
    D = w.shape[1]
    C = gate.shape[0]
    tm = _pick_tile(T, tm_want)
    nsb = tm // ROW_BLOCK
    vmem = 2 * tm * K * 2 + 2 * K * tn * 2 + 5 * tm * tn * 4 + 8 * MIB
    return pl.pallas_call(
        functools.partial(_mmres_kernel, nsb=nsb),
        out_shape=jax.ShapeDtypeStruct((T, D), F32),
        grid_spec=pltpu.PrefetchScalarGridSpec(
            num_scalar_prefetch=1, grid=(T // tm, D // tn),
            in_specs=[pl.BlockSpec((tm, K), lambda i, j, b: (i, 0)),
                      pl.BlockSpec((K, tn), lambda i, j, b: (0, j)),
                      pl.BlockSpec((tm, tn), lambda i, j, b: (i, j)),
                      pl.BlockSpec((C, 1, tn), lambda i, j, b: (0, 0, j))],
            out_specs=pl.BlockSpec((tm, tn), lambda i, j, b: (i, j)),
            scratch_shapes=[pltpu.VMEM((tm, tn), F32)]),
        compiler_params=pltpu.CompilerParams(dimension_semantics=("parallel", "arbitrary"),
                                             vmem_limit_bytes=int(vmem)),
        name="matmul_residual",
    )(b2c, a, w, h, gate)


def _final_kernel(b2c_ref, h_ref, res_ref, gate_ref, shift_ref, scale_ref, g_ref, o_ref, *, nsb, blk_off):
    i = pl.program_id(0) + blk_off

    def body(s, carry):
        c = b2c_ref[i * nsb + s]
        rows = _row_block(s)
        hv = h_ref[rows, :] + gate_ref[c] * res_ref[rows, :]
        o_ref[rows, :] = _rms_mod(hv, g_ref[...], shift_ref[c], scale_ref[c])
        return carry
    lax.fori_loop(0, nsb, body, 0)


def _final_norm(b2c, h, res, gate, shift, scale, g, *, row_off, n_rows, tm_want):
    T, D = h.shape
    C = shift.shape[0]
    tm = _pick_tile(int(np.gcd(row_off, n_rows)) if row_off else n_rows, tm_want)
    nsb = tm // ROW_BLOCK
    blk_off = row_off // tm
    row_spec = pl.BlockSpec((tm, D), lambda i, b: (i + blk_off, 0))
    mod_spec = pl.BlockSpec((C, 1, D), lambda i, b: (0, 0, 0))
    return pl.pallas_call(
        functools.partial(_final_kernel, nsb=nsb, blk_off=blk_off),
        out_shape=jax.ShapeDtypeStruct((n_rows, D), F32),
        grid_spec=pltpu.PrefetchScalarGridSpec(
            num_scalar_prefetch=1, grid=(n_rows // tm,),
            in_specs=[row_spec, row_spec, mod_spec, mod_spec, mod_spec,
                      pl.BlockSpec((1, D), lambda i, b: (0, 0))],
            out_specs=pl.BlockSpec((tm, D), lambda i, b: (i, 0))),
        compiler_params=pltpu.CompilerParams(dimension_semantics=("parallel",),
                                             vmem_limit_bytes=int(6 * tm * D * 4 + 8 * MIB)),
        name="final_norm",
    )(b2c, h, res, gate, shift, scale, g.reshape(1, D))


def _ssd_kernel(seq_ref, first_ref, last_ref,
                z_ref, x_ref, b_ref, c_ref, dt_ref,
                cpx_ref, cpb_ref, cpc_ref, sprev_ref,
                cwx_ref, cwb_ref, cwc_ref, cbx_ref, cbb_ref, cbc_ref,
                dtb_ref, a_ref, dsk_ref, ng_ref, expand_ref,
                y_ref, cnew_ref, snew_ref,
                extx_ref, extb_ref, extc_ref, st_ref, xs_ref, bm_ref, cm_ref, col_ref, rowa_ref, rowd_ref,
                ybuf_ref, cb_ref, bt_ref, lm_ref):
    Q = ROW_BLOCK
    N = SSM_D_STATE
    HP = x_ref.shape[1]
    GN = b_ref.shape[1]
    G = GN // N
    GW = HP // G
    PAIRS_PER_GROUP = GW // LANES
    CH = 2 * LANES
    k = pl.program_id(0)

    @pl.when(first_ref[k] == 1)
    def _():
        extx_ref[0:CONV_PAD, :] = cpx_ref[0]
        extb_ref[0:CONV_PAD, :] = cpb_ref[0]
        extc_ref[0:CONV_PAD, :] = cpc_ref[0]
        st_ref[...] = sprev_ref[0]

    sr = lax.broadcasted_iota(I32, ((CONV_W - 1) * Q, Q), 0)
    sc_ = lax.broadcasted_iota(I32, ((CONV_W - 1) * Q, Q), 1)
    shift = jnp.where(sc_ == (sr & (Q - 1)) + (sr // Q) - (CONV_W - 1), 1.0, 0.0).astype(BF16)

    def conv(ext_ref, cur_ref, w_ref, bias_ref, out_ref, new_off):
        for lo in range(0, cur_ref.shape[1], CH):
            cs = slice(lo, lo + CH)
            cur_bf = cur_ref[:, cs]
            cur = cur_bf.astype(F32)
            shifted = _dot(shift, cur_bf)
            acc = bias_ref[:, cs] + w_ref[CONV_W - 1:CONV_W, cs] * cur
            for j in range(CONV_W - 1):
                acc = acc + w_ref[j:j + 1, cs] * shifted[j * Q:(j + 1) * Q]
            ext_ref[CONV_PAD:2 * CONV_PAD, cs] = cur[0:CONV_PAD]
            head = bias_ref[:, cs] + w_ref[CONV_W - 1:CONV_W, cs] * cur[0:CONV_PAD]
            for j in range(CONV_W - 1):
                off = CONV_PAD - (CONV_W - 1) + j
                head = head + w_ref[j:j + 1, cs] * ext_ref[off:off + CONV_PAD, cs]
            out_ref[:, cs] = _silu(jnp.concatenate([head, acc[CONV_PAD:]], axis=0)).astype(out_ref.dtype)
            tail = cur[Q - CONV_PAD:Q]
            cnew_ref[0, :, new_off + lo:new_off + lo + CH] = tail
            ext_ref[0:CONV_PAD, cs] = tail

    conv(extx_ref, x_ref, cwx_ref, cbx_ref, xs_ref, 0)
    conv(extb_ref, b_ref, cwb_ref, cbb_ref, bm_ref, HP)
    conv(extc_ref, c_ref, cwc_ref, cbc_ref, cm_ref, HP + GN)

    dt = _softplus(dt_ref[...].astype(F32) + dtb_ref[...])
    da = dt * a_ref[...]
    rr = lax.broadcasted_iota(I32, (Q, Q), 0)
    cc = lax.broadcasted_iota(I32, (Q, Q), 1)
    tri = jnp.where(rr >= cc, 1.0, 0.0).astype(BF16)
    cs = _dot(tri, jnp.concatenate(_split_bf16(da, 3), axis=1))
    acs = cs[:, 0:LANES] + cs[:, LANES:2 * LANES] + cs[:, 2 * LANES:3 * LANES]

    both = _split_bf16(jnp.concatenate([acs, dt], axis=0), 2)
    col_ref[...] = _dot(both[0], expand_ref[...]) + _dot(both[1], expand_ref[...])

    pr = lax.broadcasted_iota(I32, (LANES // 2, 2 * LANES), 0)
    pk = lax.broadcasted_iota(I32, (LANES // 2, 2 * LANES), 1)
    esel = jnp.where(pk == jnp.where(pk < LANES, 2 * pr, 2 * pr + 1 + LANES), 1.0, 0.0).astype(BF16)
    zero = jnp.zeros((Q, LANES), BF16)

    def pair_rows(v, parts):
        out = None
        for piece in _split_bf16(v, parts):
            vbd = jnp.concatenate([jnp.concatenate([piece, zero], axis=1),
                                   jnp.concatenate([zero, piece], axis=1)], axis=0)
            r = _nt_dot(esel, vbd)
            out = r if out is None else out + r
        return out

    rowa_ref[...] = pair_rows(acs, 3)
    rowd_ref[...] = pair_rows(dt, 2)

    lane = lax.broadcasted_iota(I32, (Q, LANES), 1)
    trow = lax.broadcasted_iota(I32, (Q, LANES), 0)
    causal = trow >= (lane & (Q - 1))
    lo_half = lane < Q
    er = lax.broadcasted_iota(I32, (N, N), 0)
    ec = lax.broadcasted_iota(I32, (N, N), 1)
    eye = jnp.where(er == ec, 1.0, 0.0).astype(BF16)

    for g in range(G):
        gs = slice(g * GW, (g + 1) * GW)
        bg = bm_ref[:, g * N:(g + 1) * N]
        cg = cm_ref[:, g * N:(g + 1) * N]
        cb_ref[g] = _nt_dot(cg, jnp.concatenate([bg, bg], axis=0))
        ybuf_ref[:, gs] = _dot(cg, st_ref[:, gs].astype(BF16))
        bt_ref[g] = _nt_dot(eye, bg).astype(BF16)
    for p in range(G * PAIRS_PER_GROUP):
        ps = slice(p * LANES, (p + 1) * LANES)
        seg = col_ref[0:Q, ps] - rowa_ref[p:p + 1, :]
        decay = jnp.exp(jnp.where(causal, seg, -jnp.inf))
        lm_ref[p] = (cb_ref[p // PAIRS_PER_GROUP] * decay * rowd_ref[p:p + 1, :]).astype(BF16)
    for p in range(G * PAIRS_PER_GROUP):
        ps = slice(p * LANES, (p + 1) * LANES)
        xp = xs_ref[:, ps]
        xbd = jnp.concatenate([jnp.where(lo_half, xp, 0.0), jnp.where(lo_half, 0.0, xp)],
                              axis=0).astype(BF16)
        ybuf_ref[:, ps] = (_dot(lm_ref[p], xbd) + ybuf_ref[:, ps] * jnp.exp(col_ref[0:Q, ps])
                           + dsk_ref[:, ps] * xp)
    for g in range(G):
        gs = slice(g * GW, (g + 1) * GW)
        last = col_ref[Q - 1:Q, gs]
        w_end = col_ref[Q:2 * Q, gs] * jnp.exp(last - col_ref[0:Q, gs])
        xw = (xs_ref[:, gs] * w_end).astype(BF16)
        st_ref[:, gs] = st_ref[:, gs] * jnp.exp(last) + _dot(bt_ref[g], xw)

    sq = jnp.zeros((Q, LANES), F32)
    for lo in range(0, HP, CH):
        cs = slice(lo, lo + CH)
        yv = ybuf_ref[:, cs] * _silu(z_ref[:, cs].astype(F32))
        ybuf_ref[:, cs] = yv
        y2 = yv * yv
        for c in range(CH // LANES):
            sq = sq + y2[:, c * LANES:(c + 1) * LANES]
    inv = lax.rsqrt(jnp.sum(sq, axis=-1, keepdims=True) * (1.0 / HP) + EPS)
    for lo in range(0, HP, CH):
        cs = slice(lo, lo + CH)
        y_ref[:, cs] = (ybuf_ref[:, cs] * inv * ng_ref[:, cs]).astype(BF16)

    @pl.when(last_ref[k] == 1)
    def _():
        snew_ref[0] = st_ref[...]


def _ssd(tabs, zx, cprev, sprev, conv_w, conv_b, dt_bias, a_neg, d_skip, norm_g, expand):
    T = zx.shape[0]
    S, N, HP = sprev.shape
    GN = SSM_GROUPS * N
    Q = ROW_BLOCK
    nblk = T // Q
    bb, cb = HP // GN, HP // GN + 1
    zb = (2 * HP) // GN

    def rows(width, col_blk):
        return pl.BlockSpec((Q, width), lambda k, s, f, l: (k, col_blk))

    def per_seq(shape, col_blk):
        return pl.BlockSpec((1,) + shape, lambda k, s, f, l: (s[k], 0, col_blk))

    def const(shape, col_blk=0):
        return pl.BlockSpec(shape, lambda k, s, f, l: (0, col_blk))

    in_specs = [
        rows(HP, 0), rows(HP, 1), rows(GN, zb), rows(GN, zb + 1), rows(LANES, (2 * HP + 2 * GN) // LANES),
        per_seq((CONV_PAD, HP), 0), per_seq((CONV_PAD, GN), bb), per_seq((CONV_PAD, GN), cb),
        per_seq((N, HP), 0),
        const((CONV_W, HP)), const((CONV_W, GN), bb), const((CONV_W, GN), cb),
        const((1, HP)), const((1, GN), bb), const((1, GN), cb),
        const((1, LANES)), const((1, LANES)), const((1, HP)), const((1, HP)), const((LANES, HP)),
    ]
    out_shape = [jax.ShapeDtypeStruct((T, HP), BF16),
                 jax.ShapeDtypeStruct((S, CONV_PAD, HP + 2 * GN), F32),
                 jax.ShapeDtypeStruct((S, N, HP), F32)]
    out_specs = [pl.BlockSpec((Q, HP), lambda k, s, f, l: (k, 0)),
                 pl.BlockSpec((1, CONV_PAD, HP + 2 * GN), lambda k, s, f, l: (s[k], 0, 0)),
                 pl.BlockSpec((1, N, HP), lambda k, s, f, l: (s[k], 0, 0))]
    scratch = [pltpu.VMEM((2 * CONV_PAD, HP), F32), pltpu.VMEM((2 * CONV_PAD, GN), F32),
               pltpu.VMEM((2 * CONV_PAD, GN), F32), pltpu.VMEM((N, HP), F32), pltpu.VMEM((Q, HP), F32),
               pltpu.VMEM((Q, GN), BF16), pltpu.VMEM((Q, GN), BF16), pltpu.VMEM((2 * Q, HP), F32),
               pltpu.VMEM((LANES // 2, 2 * Q), F32), pltpu.VMEM((LANES // 2, 2 * Q), F32),
               pltpu.VMEM((Q, HP), F32), pltpu.VMEM((SSM_GROUPS, Q, 2 * Q), F32),
               pltpu.VMEM((SSM_GROUPS, N, Q), BF16), pltpu.VMEM((HP // LANES, Q, 2 * Q), BF16)]
    return pl.pallas_call(
        _ssd_kernel,
        out_shape=out_shape,
        grid_spec=pltpu.PrefetchScalarGridSpec(num_scalar_prefetch=3, grid=(nblk,), in_specs=in_specs,
                                               out_specs=out_specs, scratch_shapes=scratch),
        compiler_params=pltpu.CompilerParams(dimension_semantics=("arbitrary",), vmem_limit_bytes=48 * MIB),
        name="ssd_mixer",
    )(tabs["seq"], tabs["first"], tabs["last"],
      zx, zx, zx, zx, zx, cprev, cprev, cprev, sprev,
      conv_w, conv_w, conv_w, conv_b, conv_b, conv_b, dt_bias, a_neg, d_skip, norm_g, expand)


def _attn_kernel(wk0_ref, wk1_ref, wc0_ref, wc1_ref, uc0_ref, uc1_ref, ninv_ref,
                 q_ref, k0_ref, k1_ref, k2_ref, c0_ref, c1_ref, bias_ref, sink_ref, o_ref, sc_ref, p_ref):
    Q = ROW_BLOCK
    DH = ATT_HEAD_DIM
    KVW = k2_ref.shape[1] // 2
    n_kv = KVW // DH
    NK = bias_ref.shape[2]
    k = pl.program_id(0)
    s = lax.broadcasted_iota(I32, (Q, NK), 1)
    mask_bias = jnp.where(s >= ninv_ref[k] * Q, 0.0, -jnp.inf)
    kv = jnp.concatenate([jnp.where(uc0_ref[k] == 1, c0_ref[...], k0_ref[...]),
                          jnp.where(uc1_ref[k] == 1, c1_ref[...], k1_ref[...]), k2_ref[...]], axis=0).astype(BF16)

    for g in range(n_kv):
        qg = jnp.concatenate([q_ref[:, (g * ATT_GROUP + r) * DH:(g * ATT_GROUP + r + 1) * DH]
                              for r in range(ATT_GROUP)], axis=0)
        sc_ref[g] = _nt_dot(qg, kv[:, g * DH:(g + 1) * DH])
    sink_terms = []
    for g in range(n_kv):
        for r in range(ATT_GROUP):
            rows = slice(r * Q, (r + 1) * Q)
            sc = sc_ref[g, rows, :] * (DH ** -0.5) + bias_ref[g, rows, :] + mask_bias
            sink = sink_ref[g, rows, 0:1]
            m = jnp.maximum(jnp.max(sc, axis=-1, keepdims=True), sink)
            p_ref[g, rows, :] = jnp.exp(sc - m).astype(BF16)
            sink_terms.append(jnp.exp(sink - m))
    first_half = lax.broadcasted_iota(I32, (Q, 2 * DH), 1) < DH
    ones = jnp.ones((NK, 2 * DH), BF16)
    for g in range(n_kv):
        vg = kv[:, KVW + g * DH:KVW + (g + 1) * DH]
        og = _dot(p_ref[g], jnp.concatenate([vg, vg], axis=1))
        dn = _dot(p_ref[g], ones)
        for rp in range(ATT_GROUP // 2):
            ev, od = slice(2 * rp * Q, (2 * rp + 1) * Q), slice((2 * rp + 1) * Q, (2 * rp + 2) * Q)
            s_ev, s_od = sink_terms[g * ATT_GROUP + 2 * rp], sink_terms[g * ATT_GROUP + 2 * rp + 1]
            pair = jnp.where(first_half, og[ev] * (1.0 / (dn[ev] + s_ev)), og[od] * (1.0 / (dn[od] + s_od)))
            h0 = g * ATT_GROUP + 2 * rp
            o_ref[:, h0 * DH:(h0 + 2) * DH] = pair.astype(BF16)


def _attention(tabs, q, kv, cache_kv, sinks):
    T, HD = q.shape
    Q = ROW_BLOCK
    NK = (WINDOW_BLOCKS + 1) * Q
    n_heads = HD // ATT_HEAD_DIM
    KV2 = kv.shape[1]
    n_kv = KV2 // (2 * ATT_HEAD_DIM)
    GQ = ATT_GROUP * Q
    slopes = 2.0 ** (-8.0 * np.arange(1, n_heads + 1) / n_heads)
    dist = np.abs(np.arange(Q)[:, None] + WINDOW_BLOCKS * Q - np.arange(NK)[None, :])
    bias = jnp.asarray((-slopes[:, None, None] * dist[None]).reshape(n_kv, GQ, NK).astype(np.float32))
    sink_rows = jnp.broadcast_to(jnp.repeat(sinks.reshape(n_kv, ATT_GROUP), Q, axis=1)[:, :, None],
                                 (n_kv, GQ, LANES))

    def blk(pick):
        return pl.BlockSpec((Q, KV2), lambda k, wk0, wk1, wc0, wc1, uc0, uc1, n: (pick(k, wk0, wk1, wc0, wc1), 0))

    def const(shape):
        return pl.BlockSpec(shape, lambda k, *_: (0, 0, 0))

    return pl.pallas_call(
        _attn_kernel,
        out_shape=jax.ShapeDtypeStruct((T, HD), BF16),
        grid_spec=pltpu.PrefetchScalarGridSpec(
            num_scalar_prefetch=7, grid=(T // Q,),
            in_specs=[pl.BlockSpec((Q, HD), lambda k, *_: (k, 0)),
                      blk(lambda k, wk0, wk1, wc0, wc1: wk0[k]), blk(lambda k, wk0, wk1, wc0, wc1: wk1[k]),
                      blk(lambda k, wk0, wk1, wc0, wc1: k),
                      blk(lambda k, wk0, wk1, wc0, wc1: wc0[k]), blk(lambda k, wk0, wk1, wc0, wc1: wc1[k]),
                      const((n_kv, GQ, NK)), const((n_kv, GQ, LANES))],
            out_specs=pl.BlockSpec((Q, HD), lambda k, *_: (k, 0)),
            scratch_shapes=[pltpu.VMEM((n_kv, GQ, NK), F32), pltpu.VMEM((n_kv, GQ, NK), BF16)]),
        compiler_params=pltpu.CompilerParams(dimension_semantics=("parallel",), vmem_limit_bytes=32 * MIB),
        name="swa_attention",
    )(tabs["wk0"], tabs["wk1"], tabs["wc0"], tabs["wc1"], tabs["uc0"], tabs["uc1"], tabs["ninv"],
      q, kv, kv, kv, cache_kv, cache_kv, bias, sink_rows)


def _router_kernel(b2c_ref, h_ref, shift_ref, scale_ref, g_ref, wh_ref, wl_ref, rb_ref, hn_ref, meta_ref,
                   xh_ref, xl_ref, wt_ref, *, nsb):
    i = pl.program_id(0)

    def body(s, carry):
        c = b2c_ref[i * nsb + s]
        rows = _row_block(s)
        xn = _rms_mod(h_ref[rows, :], g_ref[...], shift_ref[c], scale_ref[c])
        hn_ref[rows, 0:h_ref.shape[1]] = xn
        hi, lo = _split_bf16(xn, 2)
        xh_ref[rows, :] = hi
        xl_ref[rows, :] = lo
        return carry
    lax.fori_loop(0, nsb, body, 0)

    lt = (_nt_dot(wh_ref[...], xh_ref[...]) + _nt_dot(wh_ref[...], xl_ref[...])
          + _nt_dot(wl_ref[...], xh_ref[...]) + rb_ref[...])

    def first_max(vals):
        m = vals[0]
        for v in vals[1:]:
            m = jnp.maximum(m, v)
        idx = jnp.full(m.shape, len(vals) - 1, I32)
        for j in range(len(vals) - 2, -1, -1):
            idx = jnp.where(vals[j] == m, j, idx)
        return m, idx

    lg = [lt[j:j + 1, :] for j in range(MOE_GROUPS)]
    mg, gi = first_max(lg)
    p_sel = 1.0 / sum(jnp.exp(v - mg) for v in lg)
    le = []
    for j in range(MOE_PER_GROUP):
        v = lt[MOE_GROUPS + (MOE_GROUPS - 1) * MOE_PER_GROUP + j:MOE_GROUPS + (MOE_GROUPS - 1) * MOE_PER_GROUP + j + 1, :]
        for grp in range(MOE_GROUPS - 2, -1, -1):
            row = MOE_GROUPS + grp * MOE_PER_GROUP + j
            v = jnp.where(gi == grp, lt[row:row + 1, :], v)
        le.append(v)
    m1, i1 = first_max(le)
    m2, i2 = first_max([jnp.where(i1 == j, -jnp.inf, le[j]) for j in range(MOE_PER_GROUP)])
    e2 = jnp.exp(m2 - m1)
    w1 = p_sel * (1.0 / (1.0 + e2))
    w2 = p_sel * (e2 / (1.0 + e2))
    lo_i = jnp.minimum(i1, i2)
    hi_i = jnp.maximum(i1, i2)
    first_is_lo = i1 < i2
    pair_base = jnp.where(lo_i == 0, 0, jnp.where(lo_i == 1, 3, 5))
    bucket = gi * MOE_PAIRS + pair_base + hi_i - lo_i - 1
    meta_ref[0:1, :] = bucket.astype(F32)
    meta_ref[1:8, :] = jnp.zeros((7, meta_ref.shape[1]), F32)
    wt_ref[...] = jnp.zeros(wt_ref.shape, F32)
    wt_ref[0:1, :] = jnp.where(first_is_lo, w1, w2)
    wt_ref[1:2, :] = jnp.where(first_is_lo, w2, w1)
    D = h_ref.shape[1]
    hn_ref[:, D:D + LANES] = wt_ref[...].T


def _router(b2c, h, g, shift, scale, w_hi, w_lo, rbias, *, tm_want):
    T, D = h.shape
    C = shift.shape[0]
    tm = _pick_tile(T, tm_want)
    nsb = tm // ROW_BLOCK
    row_spec = pl.BlockSpec((tm, D), lambda i, b: (i, 0))
    mod_spec = pl.BlockSpec((C, 1, D), lambda i, b: (0, 0, 0))
    rb = jnp.broadcast_to(rbias.reshape(LANES, 1), (LANES, tm))
    return pl.pallas_call(
        functools.partial(_router_kernel, nsb=nsb),
        out_shape=[jax.ShapeDtypeStruct((T, D + LANES), F32), jax.ShapeDtypeStruct((8, T), F32)],
        grid_spec=pltpu.PrefetchScalarGridSpec(
            num_scalar_prefetch=1, grid=(T // tm,),
            in_specs=[row_spec, mod_spec, mod_spec, pl.BlockSpec((1, D), lambda i, b: (0, 0)),
                      pl.BlockSpec((LANES, D), lambda i, b: (0, 0)), pl.BlockSpec((LANES, D), lambda i, b: (0, 0)),
                      pl.BlockSpec((LANES, tm), lambda i, b: (0, 0))],
            out_specs=[pl.BlockSpec((tm, D + LANES), lambda i, b: (i, 0)), pl.BlockSpec((8, tm), lambda i, b: (0, i))],
            scratch_shapes=[pltpu.VMEM((tm, D), BF16), pltpu.VMEM((tm, D), BF16), pltpu.VMEM((LANES, tm), F32)]),
        compiler_params=pltpu.CompilerParams(dimension_semantics=("parallel",),
                                             vmem_limit_bytes=int(5 * tm * D * 4 + 12 * MIB)),
        name="moe_router",
    )(b2c, h, shift, scale, g.reshape(1, D), w_hi, w_lo, rb)


def _gather_rows(src, idx):
    n = idx.shape[0]
    D = src.shape[1]
    n_workers = SC_CORES * SC_SUBCORES
    per_worker = n // n_workers
    n_chunks = per_worker // SC_GATHER_ROWS
    assert n == n_workers * n_chunks * SC_GATHER_ROWS, (n, n_workers, SC_GATHER_ROWS)
    mesh = plsc.VectorSubcoreMesh(core_axis_name="c", subcore_axis_name="s", num_cores=SC_CORES,
                                  num_subcores=SC_SUBCORES)

    @functools.partial(
        pl.kernel, mesh=mesh, out_type=jax.ShapeDtypeStruct((n, D), src.dtype),
        scratch_types=[pltpu.VMEM((SC_GATHER_ROWS,), I32), pltpu.VMEM((SC_GATHER_ROWS, D), src.dtype),
                       pltpu.SemaphoreType.DMA],
        name="sc_row_gather")
    def gather(src_hbm, idx_hbm, out_hbm, idx_v, rows_v, sem):
        base = (lax.axis_index("s") * SC_CORES + lax.axis_index("c")) * per_worker

        @pl.loop(0, n_chunks)
        def _(j):
            off = base + j * SC_GATHER_ROWS
            pltpu.sync_copy(idx_hbm.at[pl.ds(off, SC_GATHER_ROWS)], idx_v)
            pltpu.async_copy(src_hbm.at[idx_v], rows_v, sem).wait()
            pltpu.sync_copy(rows_v, out_hbm.at[pl.ds(off, SC_GATHER_ROWS)])

    return gather(src, idx)


def _ffn_kernel(ea_ref, eb_ref, nv_ref, x_ref, wga_ref, wua_ref, wda_ref, wgb_ref, wub_ref, wdb_ref, y_ref):
    i = pl.program_id(0)
    D = y_ref.shape[1]

    @pl.when(nv_ref[i] > 0)
    def _():
        x = x_ref[:, 0:D].astype(BF16)

        def expert(wg_ref, wu_ref, wd_ref):
            hid = _silu(_dot(x, wg_ref[0])) * _dot(x, wu_ref[0])
            return _dot(hid.astype(BF16), wd_ref[0])
        y_ref[...] = (x_ref[:, D:D + 1] * expert(wga_ref, wua_ref, wda_ref)
                      + x_ref[:, D + 1:D + 2] * expert(wgb_ref, wub_ref, wdb_ref))

    @pl.when(nv_ref[i] == 0)
    def _():
        y_ref[...] = jnp.zeros(y_ref.shape, F32)


def _moe_ffn(plan, xs, w_g, w_u, w_d):
    R, DX = xs.shape
    tm = MOE_TILE
    E, F, D = w_d.shape
    up_a = pl.BlockSpec((1, D, F), lambda i, a, b, n: (a[i], 0, 0))
    up_b = pl.BlockSpec((1, D, F), lambda i, a, b, n: (b[i], 0, 0))
    return pl.pallas_call(
        _ffn_kernel,
        out_shape=jax.ShapeDtypeStruct((R, D), F32),
        grid_spec=pltpu.PrefetchScalarGridSpec(
            num_scalar_prefetch=3, grid=(R // tm,),
            in_specs=[pl.BlockSpec((tm, DX), lambda i, a, b, n: (i, 0)),
                      up_a, up_a, pl.BlockSpec((1, F, D), lambda i, a, b, n: (a[i], 0, 0)),
                      up_b, up_b, pl.BlockSpec((1, F, D), lambda i, a, b, n: (b[i], 0, 0))],
            out_specs=pl.BlockSpec((tm, D), lambda i, a, b, n: (i, 0))),
        compiler_params=pltpu.CompilerParams(dimension_semantics=("arbitrary",), vmem_limit_bytes=48 * MIB),
        name="moe_experts",
    )(plan["ea"], plan["eb"], plan["nvalid"], xs, w_g, w_u, w_d, w_g, w_u, w_d)


def _moe_plan(meta, T):
    tm = MOE_TILE
    n_tiles = -(-T // tm) + MOE_BUCKETS
    bucket = meta[0].astype(I32)
    onehot = (bucket[:, None] == jnp.arange(MOE_BUCKETS, dtype=I32)[None, :]).astype(F32)
    blocks = onehot.reshape(T // LANES, LANES, MOE_BUCKETS)
    within = jnp.einsum("ij,bjk->bik", jnp.tril(jnp.ones((LANES, LANES), F32)), blocks)
    totals = within[:, -1, :]
    n_blk = T // LANES
    before = jnp.einsum("ij,jk->ik", jnp.tril(jnp.ones((n_blk, n_blk), F32), -1), totals,
                        precision=lax.Precision.HIGHEST)
    cum = (within + before[:, None, :]).reshape(T, MOE_BUCKETS)
    rank = jnp.sum(cum * onehot, axis=1).astype(I32) - 1
    counts = jnp.sum(totals, axis=0).astype(I32)
    ntile_b = (counts + tm - 1) // tm
    tend_b = jnp.cumsum(ntile_b)
    tstart_b = tend_b - ntile_b
    dest = tstart_b[bucket] * tm + rank
    tok_of_pos = jnp.zeros((n_tiles * tm,), I32).at[dest].set(jnp.arange(T, dtype=I32))
    tiles = jnp.arange(n_tiles, dtype=I32)
    tile_b = jnp.minimum(jnp.searchsorted(tend_b, tiles, side="right").astype(I32), MOE_BUCKETS - 1)
    nvalid = jnp.where(tiles < tend_b[-1],
                       jnp.clip(counts[tile_b] - (tiles - tstart_b[tile_b]) * tm, 0, tm), 0).astype(I32)
    grp = tile_b // MOE_PAIRS
    pair = tile_b % MOE_PAIRS
    ea = grp * MOE_PER_GROUP + jnp.asarray(MOE_PAIR_LO, I32)[pair]
    eb = grp * MOE_PER_GROUP + jnp.asarray(MOE_PAIR_HI, I32)[pair]
    return dict(dest=dest, tok_of_pos=tok_of_pos, nvalid=nvalid, ea=ea.astype(I32), eb=eb.astype(I32))


def _hmoe(b2c, h, g, shift, scale, router, w_g, w_u, w_d):
    T = h.shape[0]
    hn, meta = _router(b2c, h, g, shift, scale, *router, tm_want=512)
    plan = _moe_plan(meta, T)
    xs = _gather_rows(hn, plan["tok_of_pos"])
    ys = _moe_ffn(plan, xs, w_g, w_u, w_d)
    return _gather_rows(ys, plan["dest"])


def _tables(Bp, Lp, Bs, Ls):
    nbp, nbs = Lp // ROW_BLOCK, Ls // ROW_BLOCK
    cols = dict(seq=[], first=[], last=[], ninv=[], wk0=[], wk1=[], wc0=[], wc1=[], uc0=[], uc1=[])
    k = 0
    for b in range(Bp + Bs):
        nb = nbp if b < Bp else nbs
        for c in range(nb):
            cols["seq"].append(b)
            cols["first"].append(int(c == 0))
            cols["last"].append(int(c == nb - 1))
            cols["ninv"].append(max(0, WINDOW_BLOCKS - c) if b < Bp else 0)
            for j in range(WINDOW_BLOCKS):
                hist = c - WINDOW_BLOCKS + j
                from_cache = b >= Bp and hist < 0
                cols[f"uc{j}"].append(int(from_cache))
                cols[f"wc{j}"].append((b - Bp) * WINDOW_BLOCKS + WINDOW_BLOCKS + hist if from_cache else 0)
                cols[f"wk{j}"].append(k if from_cache else max(k - WINDOW_BLOCKS + j, 0))
            k += 1
    return {name: jnp.asarray(np.asarray(v, np.int32)) for name, v in cols.items()}


def _router_weights(router_g, bias_g, router_e, bias_e):
    D = router_g.shape[0]
    w = jnp.zeros((LANES, D), F32)
    w = w.at[:MOE_GROUPS].set(router_g.T).at[MOE_GROUPS:MOE_GROUPS + router_e.shape[1]].set(router_e.T)
    w_hi = w.astype(BF16)
    w_lo = (w - w_hi.astype(F32)).astype(BF16)
    rb = jnp.zeros((LANES,), F32).at[:MOE_GROUPS].set(bias_g).at[MOE_GROUPS:MOE_GROUPS + bias_e.shape[0]].set(bias_e)
    return w_hi, w_lo, rb


def kernel(x_prompt, x_sample, state_conv, state_ssm, cache_k, cache_v, c_prompt, c_sample, ada_w, ada_b, norm_mix, norm_ffn, norm_kv, norm_out, ssm_w_in, ssm_conv_w, ssm_conv_b, ssm_dt_bias, ssm_a_log, ssm_d, ssm_norm, ssm_w_out, attn_w_kv, attn_w_q, attn_sinks, attn_w_o, moe_router_g, moe_bias_g, moe_router_e, moe_bias_e, moe_w_gate, moe_w_up, moe_w_down):
    Bp, Lp, D = x_prompt.shape
    Bs, Ls, _ = x_sample.shape
    Tp, Ts = Bp * Lp, Bs * Ls
    T = Tp + Ts
    C = Bp + Bs
    tabs = _tables(Bp, Lp, Bs, Ls)
    b2c = tabs["seq"]

    n_mod = ada_w.shape[1] // D
    mods = _mods(jnp.concatenate([c_prompt, c_sample], axis=0), ada_w, ada_b).reshape(C, n_mod, D)
    mod = lambda i: mods[:, i:i + 1, :]
    x = jnp.concatenate([x_prompt.reshape(Tp, D), x_sample.reshape(Ts, D)], axis=0)

    HP = ssm_w_out.shape[1]
    H = HP // SSM_HEAD_DIM
    N = SSM_D_STATE
    GN = SSM_GROUPS * N
    conv_dim = HP + 2 * GN
    w_in = jnp.pad(ssm_w_in[0], ((0, 0), (0, LANES - H))).astype(BF16)
    zx = _norm_matmul(b2c, x, norm_mix[0], mod(0), mod(1), w_in, out_dtype=BF16,
                      tn=(2 * HP + 2 * GN + LANES) // 9, tm_want=1024)
    cprev = jnp.zeros((C, CONV_PAD, conv_dim), F32).at[Bp:, CONV_PAD - (CONV_W - 1):].set(state_conv[0])
    sprev = jnp.concatenate([jnp.zeros((Bp, N, HP), F32),
                             jnp.transpose(state_ssm[0], (0, 3, 1, 2)).reshape(Bs, N, HP)], axis=0)
    pad_h = lambda v: jnp.pad(v.astype(F32), (0, LANES - H)).reshape(1, LANES)
    expand = (jnp.arange(LANES)[:, None] == (jnp.arange(HP) // SSM_HEAD_DIM)[None, :]).astype(BF16)
    y_ssd, cnew, snew = _ssd(
        tabs, zx, cprev, sprev, ssm_conv_w[0], ssm_conv_b[0].reshape(1, conv_dim), pad_h(ssm_dt_bias[0]),
        pad_h(-jnp.exp(ssm_a_log[0].astype(F32))), jnp.repeat(ssm_d[0].astype(F32), SSM_HEAD_DIM).reshape(1, HP),
        ssm_norm[0].reshape(1, HP), expand)
    h = _matmul_residual(b2c, y_ssd, ssm_w_out[0].astype(BF16), x, mod(2), tn=512, tm_want=1024)

    def experts(layer):
        return moe_w_gate[layer].astype(BF16), moe_w_up[layer].astype(BF16), moe_w_down[layer].astype(BF16)

    def router(layer):
        return _router_weights(moe_router_g[layer], moe_bias_g[layer], moe_router_e[layer], moe_bias_e[layer])

    moe0 = _hmoe(b2c, h, norm_ffn[0], mod(3), mod(4), router(0), *experts(0))

    kv, h = _norm_matmul(b2c, h, norm_kv, mod(12), mod(13), attn_w_kv.astype(BF16), out_dtype=F32,
                         tn=attn_w_kv.shape[1], tm_want=512, res=moe0, gate=mod(5))
    KV2 = kv.shape[1]
    KVW = KV2 // 2
    W = WINDOW_BLOCKS * ROW_BLOCK
    cache_kv = jnp.concatenate([cache_k.reshape(Bs, W, KVW), cache_v.reshape(Bs, W, KVW)], axis=-1)
    kvp = kv[:Tp].reshape(Bp, Lp, KV2)
    kvs = jnp.concatenate([cache_kv, kv[Tp:].reshape(Bs, Ls, KV2)], axis=1)
    q = _norm_matmul(b2c, h, norm_mix[1], mod(6), mod(7), attn_w_q[0].astype(BF16), out_dtype=BF16,
                     tn=1024, tm_want=1024)
    o = _attention(tabs, q, kv, cache_kv.reshape(Bs * W, KV2), attn_sinks[0].astype(F32))
    h = _matmul_residual(b2c, o, attn_w_o[0].astype(BF16), h, mod(8), tn=1024, tm_want=1024)
    moe1 = _hmoe(b2c, h, norm_ffn[1], mod(9), mod(10), router(1), *experts(1))

    fin = functools.partial(_final_norm, b2c, h, moe1, mod(11), mod(14), mod(15), norm_out, tm_want=512)
    y_prompt = fin(row_off=0, n_rows=Tp).reshape(Bp, Lp, D)
    y_sample = fin(row_off=Tp, n_rows=Ts).reshape(Bs, Ls, D)

    kv_heads = KVW // ATT_HEAD_DIM
    tail = lambda a, lo: a[:, -W:, lo:lo + KVW].reshape(a.shape[0], W, kv_heads, ATT_HEAD_DIM)
    conv_tail = cnew[:, CONV_PAD - (CONV_W - 1):]
    ssm_new = jnp.transpose(snew.reshape(C, N, H, SSM_HEAD_DIM), (0, 2, 3, 1))
    return (y_prompt, y_sample, conv_tail[None, :Bp], ssm_new[None, :Bp], tail(kvp, 0), tail(kvp, KVW),
            conv_tail[None, Bp:], ssm_new[None, Bp:], tail(kvs, 0), tail(kvs, KVW))
```

```python
import functools

import numpy as np
import jax
import jax.numpy as jnp
from jax import lax
from jax.experimental import pallas as pl
from jax.experimental.pallas import tpu as pltpu
from jax.experimental.pallas import tpu_sc as plsc

F32 = jnp.float32
BF16 = jnp.bfloat16
I32 = jnp.int32
EPS = 1e-6
ROW_BLOCK = 64
WINDOW_BLOCKS = 2
LANES = 128
MIB = 1024 * 1024

SSM_HEAD_DIM = 64
SSM_GROUPS = 8
SSM_D_STATE = 128
CONV_W = 4
CONV_PAD = 8
ATT_HEAD_DIM = 64
ATT_GROUP = 8
MOE_GROUPS = 4
MOE_PER_GROUP = 4
MOE_PAIR_LO = (0, 0, 0, 1, 1, 2)
MOE_PAIR_HI = (1, 2, 3, 2, 3, 3)
MOE_PAIRS = len(MOE_PAIR_LO)
MOE_BUCKETS = MOE_GROUPS * MOE_PAIRS
MOE_TILE = 256
SC_CORES = 2
SC_SUBCORES = 16
SC_GATHER_ROWS = 16


def _pick_tile(n, want):
    t = min(want, n)
    t -= t % ROW_BLOCK
    while n % t:
        t -= ROW_BLOCK
    return t


def _silu(x):
    return (0.5 * x) * (1.0 + jnp.tanh(0.5 * x))


def _softplus(x):
    return jnp.maximum(x, 0.0) + jnp.log1p(jnp.exp(-jnp.abs(x)))


def _rms_mod(hv, g, shift, scale):
    ms = jnp.mean(hv * hv, axis=-1, keepdims=True)
    return (hv * lax.rsqrt(ms + EPS) * g) * (1.0 + scale) + shift


def _dot(a, b):
    return jnp.dot(a, b, preferred_element_type=F32)


def _nt_dot(a, b):
    return lax.dot_general(a, b, (((1,), (1,)), ((), ())), preferred_element_type=F32)


def _split_bf16(x, parts):
    out = []
    r = x
    for _ in range(parts):
        p = r.astype(BF16)
        out.append(p)
        r = r - p.astype(F32)
    return out


def _row_block(s):
    return pl.ds(pl.multiple_of(s * ROW_BLOCK, ROW_BLOCK), ROW_BLOCK)


def _cast_kernel(x_ref, o_ref, *, valid_cols):
    x = x_ref[...]
    if valid_cols is not None:
        col = pl.program_id(1) * x.shape[1] + lax.broadcasted_iota(I32, x.shape, 1)
        x = jnp.where(col < valid_cols, x, 0.0)
    o_ref[...] = x.astype(BF16)


def _to_bf16(w, *, tn=None, out_cols=None):
    w = w.reshape(-1, w.shape[-1])
    R, cols = w.shape
    out_cols = out_cols or cols
    tn = tn or out_cols
    tm = _pick_tile(R, max(ROW_BLOCK, (8 * MIB) // (tn * 4)))
    return pl.pallas_call(
        functools.partial(_cast_kernel, valid_cols=cols if out_cols != cols else None),
        out_shape=jax.ShapeDtypeStruct((R, out_cols), BF16),
        grid=(R // tm, out_cols // tn),
        in_specs=[pl.BlockSpec((tm, tn), lambda i, j: (i, j))],
        out_specs=pl.BlockSpec((tm, tn), lambda i, j: (i, j)),
        compiler_params=pltpu.CompilerParams(dimension_semantics=("parallel", "parallel"),
                                             vmem_limit_bytes=int(12 * tm * tn + 4 * MIB)),
        name="cast_bf16",
    )(w)


def _mods_kernel(c_ref, w_ref, b_ref, o_ref):
    a = _silu(c_ref[...]).astype(BF16)
    o_ref[...] = _dot(a, w_ref[...].astype(BF16)) + b_ref[...]


def _mods(c_all, ada_w, ada_b):
    C, D = c_all.shape
    N = ada_w.shape[1]
    tn = 1024
    return pl.pallas_call(
        _mods_kernel,
        out_shape=jax.ShapeDtypeStruct((C, N), F32),
        grid=(N // tn,),
        in_specs=[pl.BlockSpec((C, D), lambda j: (0, 0)),
                  pl.BlockSpec((D, tn), lambda j: (0, j)),
                  pl.BlockSpec((1, tn), lambda j: (0, j))],
        out_specs=pl.BlockSpec((C, tn), lambda j: (0, j)),
        compiler_params=pltpu.CompilerParams(dimension_semantics=("parallel",), vmem_limit_bytes=40 * MIB),
        name="ada_mods",
    )(c_all, ada_w, ada_b.reshape(1, N))


def _nmm_kernel(*refs, nsb, has_res):
    if has_res:
        b2c_ref, h_ref, res_ref, gate_ref, shift_ref, scale_ref, g_ref, w_ref, o_ref, hnew_ref, xn_ref = refs
    else:
        b2c_ref, h_ref, shift_ref, scale_ref, g_ref, w_ref, o_ref, xn_ref = refs
    i = pl.program_id(0)

    @pl.when(pl.program_id(1) == 0)
    def _():
        def body(s, carry):
            c = b2c_ref[i * nsb + s]
            rows = _row_block(s)
            hv = h_ref[rows, :]
            if has_res:
                hv = hv + gate_ref[c] * res_ref[rows, :]
                hnew_ref[rows, :] = hv
            xn_ref[rows, :] = _rms_mod(hv, g_ref[...], shift_ref[c], scale_ref[c]).astype(BF16)
            return carry
        lax.fori_loop(0, nsb, body, 0)

    o_ref[...] = _dot(xn_ref[...], w_ref[...]).astype(o_ref.dtype)


def _norm_matmul(b2c, h, g, shift, scale, w, *, out_dtype, tn, tm_want, res=None, gate=None):
    T, D = h.shape
    N = w.shape[1]
    C = shift.shape[0]
    tm = _pick_tile(T, tm_want)
    nsb = tm // ROW_BLOCK
    has_res = res is not None
    row_spec = pl.BlockSpec((tm, D), lambda i, j, b: (i, 0))
    mod_spec = pl.BlockSpec((C, 1, D), lambda i, j, b: (0, 0, 0))
    in_specs = [row_spec]
    args = [h]
    if has_res:
        in_specs += [row_spec, mod_spec]
        args += [res, gate]
    in_specs += [mod_spec, mod_spec, pl.BlockSpec((1, D), lambda i, j, b: (0, 0)),
                 pl.BlockSpec((D, tn), lambda i, j, b: (0, j))]
    args += [shift, scale, g.reshape(1, D), w]
    out_shape = [jax.ShapeDtypeStruct((T, N), out_dtype)]
    out_specs = [pl.BlockSpec((tm, tn), lambda i, j, b: (i, j))]
    if has_res:
        out_shape.append(jax.ShapeDtypeStruct((T, D), F32))
        out_specs.append(row_spec)
    n_row_bufs = 3 if has_res else 1
    vmem = (2 * n_row_bufs * tm * D * 4 + tm * D * 2 + 2 * D * tn * 2
            + 2 * tm * tn * jnp.dtype(out_dtype).itemsize + 8 * MIB)
    outs = pl.pallas_call(
        functools.partial(_nmm_kernel, nsb=nsb, has_res=has_res),
        out_shape=out_shape,
        grid_spec=pltpu.PrefetchScalarGridSpec(
            num_scalar_prefetch=1, grid=(T // tm, N // tn), in_specs=in_specs, out_specs=out_specs,
            scratch_shapes=[pltpu.VMEM((tm, D), BF16)]),
        compiler_params=pltpu.CompilerParams(dimension_semantics=("parallel", "arbitrary"),
                                             vmem_limit_bytes=int(vmem)),
        name="norm_matmul_res" if has_res else "norm_matmul",
    )(b2c, *args)
    return outs if has_res else outs[0]


def _mmres_kernel(b2c_ref, a_ref, w_ref, h_ref, gate_ref, o_ref, acc_ref, *, nsb):
    i = pl.program_id(0)
    acc_ref[...] = _dot(a_ref[...], w_ref[...])

    def body(s, carry):
        c = b2c_ref[i * nsb + s]
        rows = _row_block(s)
        o_ref[rows, :] = h_ref[rows, :] + gate_ref[c] * acc_ref[rows, :]
        return carry
    lax.fori_loop(0, nsb, body, 0)


def _matmul_residual(b2c, a, w, h, gate, *, tn, tm_want):
    T, K = a.shape
    D = w.shape[1]
    C = gate.shape[0]
    tm = _pick_tile(T, tm_want)
    nsb = tm // ROW_BLOCK
    vmem = 2 * tm * K * 2 + 2 * K * tn * 2 + 5 * tm * tn * 4 + 8 * MIB
    return pl.pallas_call(
        functools.partial(_mmres_kernel, nsb=nsb),
        out_shape=jax.ShapeDtypeStruct((T, D), F32),
        grid_spec=pltpu.PrefetchScalarGridSpec(
            num_scalar_prefetch=1, grid=(T // tm, D // tn),
            in_specs=[pl.BlockSpec((tm, K), lambda i, j, b: (i, 0)),
                      pl.BlockSpec((K, tn), lambda i, j, b: (0, j)),
                      pl.BlockSpec((tm, tn), lambda i, j, b: (i, j)),
                      pl.BlockSpec((C, 1, tn), lambda i, j, b: (0, 0, j))],
            out_specs=pl.BlockSpec((tm, tn), lambda i, j, b: (i, j)),
            scratch_shapes=[pltpu.VMEM((tm, tn), F32)]),
        compiler_params=pltpu.CompilerParams(dimension_semantics=("parallel", "arbitrary"),
                                             vmem_limit_bytes=int(vmem)),
        name="matmul_residual",
    )(b2c, a, w, h, gate)


def _final_kernel(b2c_ref, h_ref, res_ref, gate_ref, shift_ref, scale_ref, g_ref, o_ref, *, nsb, blk_off):
    i = pl.program_id(0) + blk_off

    def body(s, carry):
        c = b2c_ref[i * nsb + s]
        rows = _row_block(s)
        hv = h_ref[rows, :] + gate_ref[c] * res_ref[rows, :]
        o_ref[rows, :] = _rms_mod(hv, g_ref[...], shift_ref[c], scale_ref[c])
        return carry
    lax.fori_loop(0, nsb, body, 0)


def _final_norm(b2c, h, res, gate, shift, scale, g, *, row_off, n_rows, tm_want):
    T, D = h.shape
    C = shift.shape[0]
    tm = _pick_tile(int(np.gcd(row_off, n_rows)) if row_off else n_rows, tm_want)
    nsb = tm // ROW_BLOCK
    blk_off = row_off // tm
    row_spec = pl.BlockSpec((tm, D), lambda i, b: (i + blk_off, 0))
    mod_spec = pl.BlockSpec((C, 1, D), lambda i, b: (0, 0, 0))
    return pl.pallas_call(
        functools.partial(_final_kernel, nsb=nsb, blk_off=blk_off),
        out_shape=jax.ShapeDtypeStruct((n_rows, D), F32),
        grid_spec=pltpu.PrefetchScalarGridSpec(
            num_scalar_prefetch=1, grid=(n_rows // tm,),
            in_specs=[row_spec, row_spec, mod_spec, mod_spec, mod_spec,
                      pl.BlockSpec((1, D), lambda i, b: (0, 0))],
            out_specs=pl.BlockSpec((tm, D), lambda i, b: (i, 0))),
        compiler_params=pltpu.CompilerParams(dimension_semantics=("parallel",),
                                             vmem_limit_bytes=int(6 * tm * D * 4 + 8 * MIB)),
        name="final_norm",
    )(b2c, h, res, gate, shift, scale, g.reshape(1, D))


def _ssd_kernel(seq_ref, first_ref, last_ref,
                z_ref, x_ref, b_ref, c_ref, dt_ref,
                cpx_ref, cpb_ref, cpc_ref, sprev_ref,
                cwx_ref, cwb_ref, cwc_ref, cbx_ref, cbb_ref, cbc_ref,
                dtb_ref, a_ref, dsk_ref, ng_ref, expand_ref,
                y_ref, cnew_ref, snew_ref,
                extx_ref, extb_ref, extc_ref, st_ref, xs_ref, bm_ref, cm_ref, col_ref, rowa_ref, rowd_ref,
                ybuf_ref, cb_ref, bt_ref, lm_ref):
    Q = ROW_BLOCK
    N = SSM_D_STATE
    HP = x_ref.shape[1]
    GN = b_ref.shape[1]
    G = GN // N
    GW = HP // G
    PAIRS_PER_GROUP = GW // LANES
    CH = 2 * LANES
    k = pl.program_id(0)

    @pl.when(first_ref[k] == 1)
    def _():
        extx_ref[0:CONV_PAD, :] = cpx_ref[0]
        extb_ref[0:CONV_PAD, :] = cpb_ref[0]
        extc_ref[0:CONV_PAD, :] = cpc_ref[0]
        st_ref[...] = sprev_ref[0]

    sr = lax.broadcasted_iota(I32, ((CONV_W - 1) * Q, Q), 0)
    sc_ = lax.broadcasted_iota(I32, ((CONV_W - 1) * Q, Q), 1)
    shift = jnp.where(sc_ == (sr & (Q - 1)) + (sr // Q) - (CONV_W - 1), 1.0, 0.0).astype(BF16)

    def conv(ext_ref, cur_ref, w_ref, bias_ref, out_ref, new_off):
        for lo in range(0, cur_ref.shape[1], CH):
            cs = slice(lo, lo + CH)
            cur_bf = cur_ref[:, cs]
            cur = cur_bf.astype(F32)
            shifted = _dot(shift, cur_bf)
            acc = bias_ref[:, cs] + w_ref[CONV_W - 1:CONV_W, cs] * cur
            for j in range(CONV_W - 1):
                acc = acc + w_ref[j:j + 1, cs] * shifted[j * Q:(j + 1) * Q]
            ext_ref[CONV_PAD:2 * CONV_PAD, cs] = cur[0:CONV_PAD]
            head = bias_ref[:, cs] + w_ref[CONV_W - 1:CONV_W, cs] * cur[0:CONV_PAD]
            for j in range(CONV_W - 1):
                off = CONV_PAD - (CONV_W - 1) + j
                head = head + w_ref[j:j + 1, cs] * ext_ref[off:off + CONV_PAD, cs]
            out_ref[:, cs] = _silu(jnp.concatenate([head, acc[CONV_PAD:]], axis=0)).astype(out_ref.dtype)
            tail = cur[Q - CONV_PAD:Q]
            cnew_ref[0, :, new_off + lo:new_off + lo + CH] = tail
            ext_ref[0:CONV_PAD, cs] = tail

    conv(extx_ref, x_ref, cwx_ref, cbx_ref, xs_ref, 0)
    conv(extb_ref, b_ref, cwb_ref, cbb_ref, bm_ref, HP)
    conv(extc_ref, c_ref, cwc_ref, cbc_ref, cm_ref, HP + GN)

    dt = _softplus(dt_ref[...].astype(F32) + dtb_ref[...])
    da = dt * a_ref[...]
    rr = lax.broadcasted_iota(I32, (Q, Q), 0)
    cc = lax.broadcasted_iota(I32, (Q, Q), 1)
    tri = jnp.where(rr >= cc, 1.0, 0.0).astype(BF16)
    cs = _dot(tri, jnp.concatenate(_split_bf16(da, 3), axis=1))
    acs = cs[:, 0:LANES] + cs[:, LANES:2 * LANES] + cs[:, 2 * LANES:3 * LANES]

    both = _split_bf16(jnp.concatenate([acs, dt], axis=0), 2)
    col_ref[...] = _dot(both[0], expand_ref[...]) + _dot(both[1], expand_ref[...])

    pr = lax.broadcasted_iota(I32, (LANES // 2, 2 * LANES), 0)
    pk = lax.broadcasted_iota(I32, (LANES // 2, 2 * LANES), 1)
    esel = jnp.where(pk == jnp.where(pk < LANES, 2 * pr, 2 * pr + 1 + LANES), 1.0, 0.0).astype(BF16)
    zero = jnp.zeros((Q, LANES), BF16)

    def pair_rows(v, parts):
        out = None
        for piece in _split_bf16(v, parts):
            vbd = jnp.concatenate([jnp.concatenate([piece, zero], axis=1),
                                   jnp.concatenate([zero, piece], axis=1)], axis=0)
            r = _nt_dot(esel, vbd)
            out = r if out is None else out + r
        return out

    rowa_ref[...] = pair_rows(acs, 3)
    rowd_ref[...] = pair_rows(dt, 2)

    lane = lax.broadcasted_iota(I32, (Q, LANES), 1)
    trow = lax.broadcasted_iota(I32, (Q, LANES), 0)
    causal = trow >= (lane & (Q - 1))
    lo_half = lane < Q
    er = lax.broadcasted_iota(I32, (N, N), 0)
    ec = lax.broadcasted_iota(I32, (N, N), 1)
    eye = jnp.where(er == ec, 1.0, 0.0).astype(BF16)

    for g in range(G):
        gs = slice(g * GW, (g + 1) * GW)
        bg = bm_ref[:, g * N:(g + 1) * N]
        cg = cm_ref[:, g * N:(g + 1) * N]
        cb_ref[g] = _nt_dot(cg, jnp.concatenate([bg, bg], axis=0))
        ybuf_ref[:, gs] = _dot(cg, st_ref[:, gs].astype(BF16))
        bt_ref[g] = _nt_dot(eye, bg).astype(BF16)
    for p in range(G * PAIRS_PER_GROUP):
        ps = slice(p * LANES, (p + 1) * LANES)
        seg = col_ref[0:Q, ps] - rowa_ref[p:p + 1, :]
        decay = jnp.exp(jnp.where(causal, seg, -jnp.inf))
        lm_ref[p] = (cb_ref[p // PAIRS_PER_GROUP] * decay * rowd_ref[p:p + 1, :]).astype(BF16)
    for p in range(G * PAIRS_PER_GROUP):
        ps = slice(p * LANES, (p + 1) * LANES)
        xp = xs_ref[:, ps]
        xbd = jnp.concatenate([jnp.where(lo_half, xp, 0.0), jnp.where(lo_half, 0.0, xp)],
                              axis=0).astype(BF16)
        ybuf_ref[:, ps] = (_dot(lm_ref[p], xbd) + ybuf_ref[:, ps] * jnp.exp(col_ref[0:Q, ps])
                           + dsk_ref[:, ps] * xp)
    for g in range(G):
        gs = slice(g * GW, (g + 1) * GW)
        last = col_ref[Q - 1:Q, gs]
        w_end = col_ref[Q:2 * Q, gs] * jnp.exp(last - col_ref[0:Q, gs])
        xw = (xs_ref[:, gs] * w_end).astype(BF16)
        st_ref[:, gs] = st_ref[:, gs] * jnp.exp(last) + _dot(bt_ref[g], xw)

    sq = jnp.zeros((Q, LANES), F32)
    for lo in range(0, HP, CH):
        cs = slice(lo, lo + CH)
        yv = ybuf_ref[:, cs] * _silu(z_ref[:, cs].astype(F32))
        ybuf_ref[:, cs] = yv
        y2 = yv * yv
        for c in range(CH // LANES):
            sq = sq + y2[:, c * LANES:(c + 1) * LANES]
    inv = lax.rsqrt(jnp.sum(sq, axis=-1, keepdims=True) * (1.0 / HP) + EPS)
    for lo in range(0, HP, CH):
        cs = slice(lo, lo + CH)
        y_ref[:, cs] = (ybuf_ref[:, cs] * inv * ng_ref[:, cs]).astype(BF16)

    @pl.when(last_ref[k] == 1)
    def _():
        snew_ref[0] = st_ref[...]


def _ssd(tabs, zx, cprev, sprev, conv_w, conv_b, dt_bias, a_neg, d_skip, norm_g, expand):
    T = zx.shape[0]
    S, N, HP = sprev.shape
    GN = SSM_GROUPS * N
    Q = ROW_BLOCK
    nblk = T // Q
    bb, cb = HP // GN, HP // GN + 1
    zb = (2 * HP) // GN

    def rows(width, col_blk):
        return pl.BlockSpec((Q, width), lambda k, s, f, l: (k, col_blk))

    def per_seq(shape, col_blk):
        return pl.BlockSpec((1,) + shape, lambda k, s, f, l: (s[k], 0, col_blk))

    def const(shape, col_blk=0):
        return pl.BlockSpec(shape, lambda k, s, f, l: (0, col_blk))

    in_specs = [
        rows(HP, 0), rows(HP, 1), rows(GN, zb), rows(GN, zb + 1), rows(LANES, (2 * HP + 2 * GN) // LANES),
        per_seq((CONV_PAD, HP), 0), per_seq((CONV_PAD, GN), bb), per_seq((CONV_PAD, GN), cb),
        per_seq((N, HP), 0),
        const((CONV_W, HP)), const((CONV_W, GN), bb), const((CONV_W, GN), cb),
        const((1, HP)), const((1, GN), bb), const((1, GN), cb),
        const((1, LANES)), const((1, LANES)), const((1, HP)), const((1, HP)), const((LANES, HP)),
    ]
    out_shape = [jax.ShapeDtypeStruct((T, HP), BF16),
                 jax.ShapeDtypeStruct((S, CONV_PAD, HP + 2 * GN), F32),
                 jax.ShapeDtypeStruct((S, N, HP), F32)]
    out_specs = [pl.BlockSpec((Q, HP), lambda k, s, f, l: (k, 0)),
                 pl.BlockSpec((1, CONV_PAD, HP + 2 * GN), lambda k, s, f, l: (s[k], 0, 0)),
                 pl.BlockSpec((1, N, HP), lambda k, s, f, l: (s[k], 0, 0))]
    scratch = [pltpu.VMEM((2 * CONV_PAD, HP), F32), pltpu.VMEM((2 * CONV_PAD, GN), F32),
               pltpu.VMEM((2 * CONV_PAD, GN), F32), pltpu.VMEM((N, HP), F32), pltpu.VMEM((Q, HP), F32),
               pltpu.VMEM((Q, GN), BF16), pltpu.VMEM((Q, GN), BF16), pltpu.VMEM((2 * Q, HP), F32),
               pltpu.VMEM((LANES // 2, 2 * Q), F32), pltpu.VMEM((LANES // 2, 2 * Q), F32),
               pltpu.VMEM((Q, HP), F32), pltpu.VMEM((SSM_GROUPS, Q, 2 * Q), F32),
               pltpu.VMEM((SSM_GROUPS, N, Q), BF16), pltpu.VMEM((HP // LANES, Q, 2 * Q), BF16)]
    return pl.pallas_call(
        _ssd_kernel,
        out_shape=out_shape,
        grid_spec=pltpu.PrefetchScalarGridSpec(num_scalar_prefetch=3, grid=(nblk,), in_specs=in_specs,
                                               out_specs=out_specs, scratch_shapes=scratch),
        compiler_params=pltpu.CompilerParams(dimension_semantics=("arbitrary",), vmem_limit_bytes=48 * MIB),
        name="ssd_mixer",
    )(tabs["seq"], tabs["first"], tabs["last"],
      zx, zx, zx, zx, zx, cprev, cprev, cprev, sprev,
      conv_w, conv_w, conv_w, conv_b, conv_b, conv_b, dt_bias, a_neg, d_skip, norm_g, expand)


def _attn_kernel(wk0_ref, wk1_ref, wc0_ref, wc1_ref, uc0_ref, uc1_ref, ninv_ref,
                 q_ref, k0_ref, k1_ref, k2_ref, c0_ref, c1_ref, bias_ref, sink_ref, o_ref, sc_ref, p_ref):
    Q = ROW_BLOCK
    DH = ATT_HEAD_DIM
    KVW = k2_ref.shape[1] // 2
    n_kv = KVW // DH
    NK = bias_ref.shape[2]
    k = pl.program_id(0)
    s = lax.broadcasted_iota(I32, (Q, NK), 1)
    mask_bias = jnp.where(s >= ninv_ref[k] * Q, 0.0, -jnp.inf)
    kv = jnp.concatenate([jnp.where(uc0_ref[k] == 1, c0_ref[...], k0_ref[...]),
                          jnp.where(uc1_ref[k] == 1, c1_ref[...], k1_ref[...]), k2_ref[...]], axis=0).astype(BF16)

    for g in range(n_kv):
        qg = jnp.concatenate([q_ref[:, (g * ATT_GROUP + r) * DH:(g * ATT_GROUP + r + 1) * DH]
                              for r in range(ATT_GROUP)], axis=0)
        sc_ref[g] = _nt_dot(qg, kv[:, g * DH:(g + 1) * DH])
    sink_terms = []
    for g in range(n_kv):
        for r in range(ATT_GROUP):
            rows = slice(r * Q, (r + 1) * Q)
            sc = sc_ref[g, rows, :] * (DH ** -0.5) + bias_ref[g, rows, :] + mask_bias
            sink = sink_ref[g, rows, 0:1]
            m = jnp.maximum(jnp.max(sc, axis=-1, keepdims=True), sink)
            p_ref[g, rows, :] = jnp.exp(sc - m).astype(BF16)
            sink_terms.append(jnp.exp(sink - m))
    first_half = lax.broadcasted_iota(I32, (Q, 2 * DH), 1) < DH
    ones = jnp.ones((NK, 2 * DH), BF16)
    for g in range(n_kv):
        vg = kv[:, KVW + g * DH:KVW + (g + 1) * DH]
        og = _dot(p_ref[g], jnp.concatenate([vg, vg], axis=1))
        dn = _dot(p_ref[g], ones)
        for rp in range(ATT_GROUP // 2):
            ev, od = slice(2 * rp * Q, (2 * rp + 1) * Q), slice((2 * rp + 1) * Q, (2 * rp + 2) * Q)
            s_ev, s_od = sink_terms[g * ATT_GROUP + 2 * rp], sink_terms[g * ATT_GROUP + 2 * rp + 1]
            pair = jnp.where(first_half, og[ev] * (1.0 / (dn[ev] + s_ev)), og[od] * (1.0 / (dn[od] + s_od)))
            h0 = g * ATT_GROUP + 2 * rp
            o_ref[:, h0 * DH:(h0 + 2) * DH] = pair.astype(BF16)


def _attention(tabs, q, kv, cache_kv, sinks):
    T, HD = q.shape
    Q = ROW_BLOCK
    NK = (WINDOW_BLOCKS + 1) * Q
    n_heads = HD // ATT_HEAD_DIM
    KV2 = kv.shape[1]
    n_kv = KV2 // (2 * ATT_HEAD_DIM)
    GQ = ATT_GROUP * Q
    slopes = 2.0 ** (-8.0 * np.arange(1, n_heads + 1) / n_heads)
    dist = np.abs(np.arange(Q)[:, None] + WINDOW_BLOCKS * Q - np.arange(NK)[None, :])
    bias = jnp.asarray((-slopes[:, None, None] * dist[None]).reshape(n_kv, GQ, NK).astype(np.float32))
    sink_rows = jnp.broadcast_to(jnp.repeat(sinks.reshape(n_kv, ATT_GROUP), Q, axis=1)[:, :, None],
                                 (n_kv, GQ, LANES))

    def blk(pick):
        return pl.BlockSpec((Q, KV2), lambda k, wk0, wk1, wc0, wc1, uc0, uc1, n: (pick(k, wk0, wk1, wc0, wc1), 0))

    def const(shape):
        return pl.BlockSpec(shape, lambda k, *_: (0, 0, 0))

    return pl.pallas_call(
        _attn_kernel,
        out_shape=jax.ShapeDtypeStruct((T, HD), BF16),
        grid_spec=pltpu.PrefetchScalarGridSpec(
            num_scalar_prefetch=7, grid=(T // Q,),
            in_specs=[pl.BlockSpec((Q, HD), lambda k, *_: (k, 0)),
                      blk(lambda k, wk0, wk1, wc0, wc1: wk0[k]), blk(lambda k, wk0, wk1, wc0, wc1: wk1[k]),
                      blk(lambda k, wk0, wk1, wc0, wc1: k),
                      blk(lambda k, wk0, wk1, wc0, wc1: wc0[k]), blk(lambda k, wk0, wk1, wc0, wc1: wc1[k]),
                      const((n_kv, GQ, NK)), const((n_kv, GQ, LANES))],
            out_specs=pl.BlockSpec((Q, HD), lambda k, *_: (k, 0)),
            scratch_shapes=[pltpu.VMEM((n_kv, GQ, NK), F32), pltpu.VMEM((n_kv, GQ, NK), BF16)]),
        compiler_params=pltpu.CompilerParams(dimension_semantics=("parallel",), vmem_limit_bytes=32 * MIB),
        name="swa_attention",
    )(tabs["wk0"], tabs["wk1"], tabs["wc0"], tabs["wc1"], tabs["uc0"], tabs["uc1"], tabs["ninv"],
      q, kv, kv, kv, cache_kv, cache_kv, bias, sink_rows)


def _router_kernel(b2c_ref, h_ref, shift_ref, scale_ref, g_ref, wh_ref, wl_ref, rb_ref, hn_ref, meta_ref,
                   xh_ref, xl_ref, wt_ref, *, nsb):
    i = pl.program_id(0)

    def body(s, carry):
        c = b2c_ref[i * nsb + s]
        rows = _row_block(s)
        xn = _rms_mod(h_ref[rows, :], g_ref[...], shift_ref[c], scale_ref[c])
        hn_ref[rows, 0:h_ref.shape[1]] = xn
        hi, lo = _split_bf16(xn, 2)
        xh_ref[rows, :] = hi
        xl_ref[rows, :] = lo
        return carry
    lax.fori_loop(0, nsb, body, 0)

    lt = (_nt_dot(wh_ref[...], xh_ref[...]) + _nt_dot(wh_ref[...], xl_ref[...])
          + _nt_dot(wl_ref[...], xh_ref[...]) + rb_ref[...])

    def first_max(vals):
        m = vals[0]
        for v in vals[1:]:
            m = jnp.maximum(m, v)
        idx = jnp.full(m.shape, len(vals) - 1, I32)
        for j in range(len(vals) - 2, -1, -1):
            idx = jnp.where(vals[j] == m, j, idx)
        return m, idx

    lg = [lt[j:j + 1, :] for j in range(MOE_GROUPS)]
    mg, gi = first_max(lg)
    p_sel = 1.0 / sum(jnp.exp(v - mg) for v in lg)
    le = []
    for j in range(MOE_PER_GROUP):
        v = lt[MOE_GROUPS + (MOE_GROUPS - 1) * MOE_PER_GROUP + j:MOE_GROUPS + (MOE_GROUPS - 1) * MOE_PER_GROUP + j + 1, :]
        for grp in range(MOE_GROUPS - 2, -1, -1):
            row = MOE_GROUPS + grp * MOE_PER_GROUP + j
            v = jnp.where(gi == grp, lt[row:row + 1, :], v)
        le.append(v)
    m1, i1 = first_max(le)
    m2, i2 = first_max([jnp.where(i1 == j, -jnp.inf, le[j]) for j in range(MOE_PER_GROUP)])
    e2 = jnp.exp(m2 - m1)
    w1 = p_sel * (1.0 / (1.0 + e2))
    w2 = p_sel * (e2 / (1.0 + e2))
    lo_i = jnp.minimum(i1, i2)
    hi_i = jnp.maximum(i1, i2)
    first_is_lo = i1 < i2
    pair_base = jnp.where(lo_i == 0, 0, jnp.where(lo_i == 1, 3, 5))
    bucket = gi * MOE_PAIRS + pair_base + hi_i - lo_i - 1
    meta_ref[0:1, :] = bucket.astype(F32)
    meta_ref[1:8, :] = jnp.zeros((7, meta_ref.shape[1]), F32)
    wt_ref[...] = jnp.zeros(wt_ref.shape, F32)
    wt_ref[0:1, :] = jnp.where(first_is_lo, w1, w2)
    wt_ref[1:2, :] = jnp.where(first_is_lo, w2, w1)
    D = h_ref.shape[1]
    hn_ref[:, D:D + LANES] = wt_ref[...].T


def _router(b2c, h, g, shift, scale, w_hi, w_lo, rbias, *, tm_want):
    T, D = h.shape
    C = shift.shape[0]
    tm = _pick_tile(T, tm_want)
    nsb = tm // ROW_BLOCK
    row_spec = pl.BlockSpec((tm, D), lambda i, b: (i, 0))
    mod_spec = pl.BlockSpec((C, 1, D), lambda i, b: (0, 0, 0))
    rb = jnp.broadcast_to(rbias.reshape(LANES, 1), (LANES, tm))
    return pl.pallas_call(
        functools.partial(_router_kernel, nsb=nsb),
        out_shape=[jax.ShapeDtypeStruct((T, D + LANES), F32), jax.ShapeDtypeStruct((8, T), F32)],
        grid_spec=pltpu.PrefetchScalarGridSpec(
            num_scalar_prefetch=1, grid=(T // tm,),
            in_specs=[row_spec, mod_spec, mod_spec, pl.BlockSpec((1, D), lambda i, b: (0, 0)),
                      pl.BlockSpec((LANES, D), lambda i, b: (0, 0)), pl.BlockSpec((LANES, D), lambda i, b: (0, 0)),
                      pl.BlockSpec((LANES, tm), lambda i, b: (0, 0))],
            out_specs=[pl.BlockSpec((tm, D + LANES), lambda i, b: (i, 0)), pl.BlockSpec((8, tm), lambda i, b: (0, i))],
            scratch_shapes=[pltpu.VMEM((tm, D), BF16), pltpu.VMEM((tm, D), BF16), pltpu.VMEM((LANES, tm), F32)]),
        compiler_params=pltpu.CompilerParams(dimension_semantics=("parallel",),
                                             vmem_limit_bytes=int(5 * tm * D * 4 + 12 * MIB)),
        name="moe_router",
    )(b2c, h, shift, scale, g.reshape(1, D), w_hi, w_lo, rb)


def _gather_rows(src, idx):
    n = idx.shape[0]
    D = src.shape[1]
    n_workers = SC_CORES * SC_SUBCORES
    per_worker = n // n_workers
    n_chunks = per_worker // SC_GATHER_ROWS
    assert n == n_workers * n_chunks * SC_GATHER_ROWS, (n, n_workers, SC_GATHER_ROWS)
    mesh = plsc.VectorSubcoreMesh(core_axis_name="c", subcore_axis_name="s", num_cores=SC_CORES,
                                  num_subcores=SC_SUBCORES)

    @functools.partial(
        pl.kernel, mesh=mesh, out_type=jax.ShapeDtypeStruct((n, D), src.dtype),
        scratch_types=[pltpu.VMEM((SC_GATHER_ROWS,), I32), pltpu.VMEM((SC_GATHER_ROWS, D), src.dtype),
                       pltpu.SemaphoreType.DMA],
        name="sc_row_gather")
    def gather(src_hbm, idx_hbm, out_hbm, idx_v, rows_v, sem):
        base = (lax.axis_index("s") * SC_CORES + lax.axis_index("c")) * per_worker

        @pl.loop(0, n_chunks)
        def _(j):
            off = base + j * SC_GATHER_ROWS
            pltpu.sync_copy(idx_hbm.at[pl.ds(off, SC_GATHER_ROWS)], idx_v)
            pltpu.async_copy(src_hbm.at[idx_v], rows_v, sem).wait()
            pltpu.sync_copy(rows_v, out_hbm.at[pl.ds(off, SC_GATHER_ROWS)])

    return gather(src, idx)


def _ffn_kernel(ea_ref, eb_ref, nv_ref, x_ref, wga_ref, wua_ref, wda_ref, wgb_ref, wub_ref, wdb_ref, y_ref):
    i = pl.program_id(0)
    D = y_ref.shape[1]

    @pl.when(nv_ref[i] > 0)
    def _():
        x = x_ref[:, 0:D].astype(BF16)

        def expert(wg_ref, wu_ref, wd_ref):
            hid = _silu(_dot(x, wg_ref[0])) * _dot(x, wu_ref[0])
            return _dot(hid.astype(BF16), wd_ref[0])
        y_ref[...] = (x_ref[:, D:D + 1] * expert(wga_ref, wua_ref, wda_ref)
                      + x_ref[:, D + 1:D + 2] * expert(wgb_ref, wub_ref, wdb_ref))

    @pl.when(nv_ref[i] == 0)
    def _():
        y_ref[...] = jnp.zeros(y_ref.shape, F32)


def _moe_ffn(plan, xs, w_g, w_u, w_d):
    R, DX = xs.shape
    tm = MOE_TILE
    E, F, D = w_d.shape
    up_a = pl.BlockSpec((1, D, F), lambda i, a, b, n: (a[i], 0, 0))
    up_b = pl.BlockSpec((1, D, F), lambda i, a, b, n: (b[i], 0, 0))
    return pl.pallas_call(
        _ffn_kernel,
        out_shape=jax.ShapeDtypeStruct((R, D), F32),
        grid_spec=pltpu.PrefetchScalarGridSpec(
            num_scalar_prefetch=3, grid=(R // tm,),
            in_specs=[pl.BlockSpec((tm, DX), lambda i, a, b, n: (i, 0)),
                      up_a, up_a, pl.BlockSpec((1, F, D), lambda i, a, b, n: (a[i], 0, 0)),
                      up_b, up_b, pl.BlockSpec((1, F, D), lambda i, a, b, n: (b[i], 0, 0))],
            out_specs=pl.BlockSpec((tm, D), lambda i, a, b, n: (i, 0))),
        compiler_params=pltpu.CompilerParams(dimension_semantics=("arbitrary",), vmem_limit_bytes=48 * MIB),
        name="moe_experts",
    )(plan["ea"], plan["eb"], plan["nvalid"], xs, w_g, w_u, w_d, w_g, w_u, w_d)


def _moe_plan(meta, T, expert_off):
    tm = MOE_TILE
    n_tiles = -(-T // tm) + MOE_BUCKETS
    bucket = meta[0].astype(I32)
    onehot = (bucket[:, None] == jnp.arange(MOE_BUCKETS, dtype=I32)[None, :]).astype(F32)
    blocks = onehot.reshape(T // LANES, LANES, MOE_BUCKETS)
    within = jnp.einsum("ij,bjk->bik", jnp.tril(jnp.ones((LANES, LANES), F32)), blocks)
    totals = within[:, -1, :]
    n_blk = T // LANES
    before = jnp.einsum("ij,jk->ik", jnp.tril(jnp.ones((n_blk, n_blk), F32), -1), totals,
                        precision=lax.Precision.HIGHEST)
    cum = (within + before[:, None, :]).reshape(T, MOE_BUCKETS)
    rank = jnp.sum(cum * onehot, axis=1).astype(I32) - 1
    counts = jnp.sum(totals, axis=0).astype(I32)
    ntile_b = (counts + tm - 1) // tm
    tend_b = jnp.cumsum(ntile_b)
    tstart_b = tend_b - ntile_b
    dest = tstart_b[bucket] * tm + rank
    tok_of_pos = jnp.zeros((n_tiles * tm,), I32).at[dest].set(jnp.arange(T, dtype=I32))
    tiles = jnp.arange(n_tiles, dtype=I32)
    tile_b = jnp.minimum(jnp.sum((tiles[:, None] >= tend_b[None, :]).astype(I32), axis=1), MOE_BUCKETS - 1)
    nvalid = jnp.where(tiles < tend_b[-1],
                       jnp.clip(counts[tile_b] - (tiles - tstart_b[tile_b]) * tm, 0, tm), 0).astype(I32)
    grp = tile_b // MOE_PAIRS
    pair = tile_b % MOE_PAIRS
    ea = expert_off + grp * MOE_PER_GROUP + jnp.asarray(MOE_PAIR_LO, I32)[pair]
    eb = expert_off + grp * MOE_PER_GROUP + jnp.asarray(MOE_PAIR_HI, I32)[pair]
    return dict(dest=dest, tok_of_pos=tok_of_pos, nvalid=nvalid, ea=ea.astype(I32), eb=eb.astype(I32))


def _hmoe(b2c, h, g, shift, scale, router, w_g, w_u, w_d, expert_off):
    T = h.shape[0]
    hn, meta = _router(b2c, h, g, shift, scale, *router, tm_want=512)
    plan = _moe_plan(meta, T, expert_off)
    xs = _gather_rows(hn, plan["tok_of_pos"])
    ys = _moe_ffn(plan, xs, w_g, w_u, w_d)
    return _gather_rows(ys, plan["dest"])


def _tables(Bp, Lp, Bs, Ls):
    nbp, nbs = Lp // ROW_BLOCK, Ls // ROW_BLOCK
    cols = dict(seq=[], first=[], last=[], ninv=[], wk0=[], wk1=[], wc0=[], wc1=[], uc0=[], uc1=[])
    k = 0
    for b in range(Bp + Bs):
        nb = nbp if b < Bp else nbs
        for c in range(nb):
            cols["seq"].append(b)
            cols["first"].append(int(c == 0))
            cols["last"].append(int(c == nb - 1))
            cols["ninv"].append(max(0, WINDOW_BLOCKS - c) if b < Bp else 0)
            for j in range(WINDOW_BLOCKS):
                hist = c - WINDOW_BLOCKS + j
                from_cache = b >= Bp and hist < 0
                cols[f"uc{j}"].append(int(from_cache))
                cols[f"wc{j}"].append((b - Bp) * WINDOW_BLOCKS + WINDOW_BLOCKS + hist if from_cache else 0)
                cols[f"wk{j}"].append(k if from_cache else max(k - WINDOW_BLOCKS + j, 0))
            k += 1
    return {name: jnp.asarray(np.asarray(v, np.int32)) for name, v in cols.items()}


def _router_weights(router_g, bias_g, router_e, bias_e):
    D = router_g.shape[0]
    w = jnp.zeros((LANES, D), F32)
    w = w.at[:MOE_GROUPS].set(router_g.T).at[MOE_GROUPS:MOE_GROUPS + router_e.shape[1]].set(router_e.T)
    w_hi = w.astype(BF16)
    w_lo = (w - w_hi.astype(F32)).astype(BF16)
    rb = jnp.zeros((LANES,), F32).at[:MOE_GROUPS].set(bias_g).at[MOE_GROUPS:MOE_GROUPS + bias_e.shape[0]].set(bias_e)
    return w_hi, w_lo, rb


def kernel(x_prompt, x_sample, state_conv, state_ssm, cache_k, cache_v, c_prompt, c_sample, ada_w, ada_b, norm_mix, norm_ffn, norm_kv, norm_out, ssm_w_in, ssm_conv_w, ssm_conv_b, ssm_dt_bias, ssm_a_log, ssm_d, ssm_norm, ssm_w_out, attn_w_kv, attn_w_q, attn_sinks, attn_w_o, moe_router_g, moe_bias_g, moe_router_e, moe_bias_e, moe_w_gate, moe_w_up, moe_w_down):
    Bp, Lp, D = x_prompt.shape
    Bs, Ls, _ = x_sample.shape
    Tp, Ts = Bp * Lp, Bs * Ls
    T = Tp + Ts
    C = Bp + Bs
    tabs = _tables(Bp, Lp, Bs, Ls)
    b2c = tabs["seq"]

    n_mod = ada_w.shape[1] // D
    mods = _mods(jnp.concatenate([c_prompt, c_sample], axis=0), ada_w, ada_b).reshape(C, n_mod, D)
    mod = lambda i: mods[:, i:i + 1, :]
    x = jnp.concatenate([x_prompt.reshape(Tp, D), x_sample.reshape(Ts, D)], axis=0)

    HP = ssm_w_out.shape[1]
    H = HP // SSM_HEAD_DIM
    N = SSM_D_STATE
    GN = SSM_GROUPS * N
    conv_dim = HP + 2 * GN
    in_cols = 2 * HP + 2 * GN + LANES
    w_in = _to_bf16(ssm_w_in[0], tn=in_cols // 9, out_cols=in_cols)
    zx = _norm_matmul(b2c, x, norm_mix[0], mod(0), mod(1), w_in, out_dtype=BF16, tn=in_cols // 9, tm_want=1024)
    cprev = jnp.zeros((C, CONV_PAD, conv_dim), F32).at[Bp:, CONV_PAD - (CONV_W - 1):].set(state_conv[0])
    sprev = jnp.concatenate([jnp.zeros((Bp, N, HP), F32),
                             jnp.transpose(state_ssm[0], (0, 3, 1, 2)).reshape(Bs, N, HP)], axis=0)
    pad_h = lambda v: jnp.pad(v.astype(F32), (0, LANES - H)).reshape(1, LANES)
    expand = (jnp.arange(LANES)[:, None] == (jnp.arange(HP) // SSM_HEAD_DIM)[None, :]).astype(BF16)
    y_ssd, cnew, snew = _ssd(
        tabs, zx, cprev, sprev, ssm_conv_w[0], ssm_conv_b[0].reshape(1, conv_dim), pad_h(ssm_dt_bias[0]),
        pad_h(-jnp.exp(ssm_a_log[0].astype(F32))), jnp.repeat(ssm_d[0].astype(F32), SSM_HEAD_DIM).reshape(1, HP),
        ssm_norm[0].reshape(1, HP), expand)
    h = _matmul_residual(b2c, y_ssd, _to_bf16(ssm_w_out[0]), x, mod(2), tn=512, tm_want=1024)

    n_exp, _, moe_ff = moe_w_gate.shape[1:]
    w_gate = _to_bf16(moe_w_gate).reshape(-1, D, moe_ff)
    w_up = _to_bf16(moe_w_up).reshape(-1, D, moe_ff)
    w_down = _to_bf16(moe_w_down).reshape(-1, moe_ff, D)

    def experts(layer):
        return w_gate, w_up, w_down, layer * n_exp

    def router(layer):
        return _router_weights(moe_router_g[layer], moe_bias_g[layer], moe_router_e[layer], moe_bias_e[layer])

    moe0 = _hmoe(b2c, h, norm_ffn[0], mod(3), mod(4), router(0), *experts(0))

    kv, h = _norm_matmul(b2c, h, norm_kv, mod(12), mod(13), _to_bf16(attn_w_kv), out_dtype=F32,
                         tn=attn_w_kv.shape[1], tm_want=512, res=moe0, gate=mod(5))
    KV2 = kv.shape[1]
    KVW = KV2 // 2
    W = WINDOW_BLOCKS * ROW_BLOCK
    cache_kv = jnp.concatenate([cache_k.reshape(Bs, W, KVW), cache_v.reshape(Bs, W, KVW)], axis=-1)
    kvp = kv[:Tp].reshape(Bp, Lp, KV2)
    kvs = jnp.concatenate([cache_kv, kv[Tp:].reshape(Bs, Ls, KV2)], axis=1)
    q = _norm_matmul(b2c, h, norm_mix[1], mod(6), mod(7), _to_bf16(attn_w_q[0]), out_dtype=BF16,
                     tn=1024, tm_want=1024)
    o = _attention(tabs, q, kv, cache_kv.reshape(Bs * W, KV2), attn_sinks[0].astype(F32))
    h = _matmul_residual(b2c, o, _to_bf16(attn_w_o[0]), h, mod(8), tn=1024, tm_want=1024)
    moe1 = _hmoe(b2c, h, norm_ffn[1], mod(9), mod(10), router(1), *experts(1))

    fin = functools.partial(_final_norm, b2c, h, moe1, mod(11), mod(14), mod(15), norm_out, tm_want=512)
    y_prompt = fin(row_off=0, n_rows=Tp).reshape(Bp, Lp, D)
    y_sample = fin(row_off=Tp, n_rows=Ts).reshape(Bs, Ls, D)

    kv_heads = KVW // ATT_HEAD_DIM
    tail = lambda a, lo: a[:, -W:, lo:lo + KVW].reshape(a.shape[0], W, kv_heads, ATT_HEAD_DIM)
    conv_tail = cnew[:, CONV_PAD - (CONV_W - 1):]
    ssm_new = jnp.transpose(snew.reshape(C, N, H, SSM_HEAD_DIM), (0, 2, 3, 1))
    return (y_prompt, y_sample, conv_tail[None, :Bp], ssm_new[None, :Bp], tail(kvp, 0), tail(kvp, KVW),
            conv_tail[None, Bp:], ssm_new[None, Bp:], tail(kvs, 0), tail(kvs, KVW))
```

```python
import functools

import numpy as np
import jax
import jax.numpy as jnp
from jax import lax
from jax.experimental import pallas as pl
from jax.experimental.pallas import tpu as pltpu
from jax.experimental.pallas import tpu_sc as plsc

F32 = jnp.float32
BF16 = jnp.bfloat16
I32 = jnp.int32
EPS = 1e-6
ROW_BLOCK = 64
WINDOW_BLOCKS = 2
LANES = 128
MIB = 1024 * 1024

SSM_HEAD_DIM = 64
SSM_GROUPS = 8
SSM_D_STATE = 128
CONV_W = 4
CONV_PAD = 8
ATT_HEAD_DIM = 64
ATT_GROUP = 8
MOE_GROUPS = 4
MOE_PER_GROUP = 4
MOE_PAIR_LO = (0, 0, 0, 1, 1, 2)
MOE_PAIR_HI = (1, 2, 3, 2, 3, 3)
MOE_PAIRS = len(MOE_PAIR_LO)
MOE_BUCKETS = MOE_GROUPS * MOE_PAIRS
MOE_TILE = 256
SC_CORES = 2
SC_SUBCORES = 16
SC_GATHER_ROWS = 16


def _pick_tile(n, want):
    t = min(want, n)
    t -= t % ROW_BLOCK
    while n % t:
        t -= ROW_BLOCK
    return t


def _silu(x):
    return (0.5 * x) * (1.0 + jnp.tanh(0.5 * x))


def _softplus(x):
    return jnp.maximum(x, 0.0) + jnp.log1p(jnp.exp(-jnp.abs(x)))


def _rms_mod(hv, g, shift, scale):
    ms = jnp.mean(hv * hv, axis=-1, keepdims=True)
    return (hv * lax.rsqrt(ms + EPS) * g) * (1.0 + scale) + shift


def _dot(a, b):
    return jnp.dot(a, b, preferred_element_type=F32)


def _nt_dot(a, b):
    return lax.dot_general(a, b, (((1,), (1,)), ((), ())), preferred_element_type=F32)


def _split_bf16(x, parts):
    out = []
    r = x
    for _ in range(parts):
        p = r.astype(BF16)
        out.append(p)
        r = r - p.astype(F32)
    return out


def _pack_bf16_pairs(x):
    c = x.shape[1] // 2
    xb = x.astype(BF16).astype(F32)
    hi = lax.bitcast_convert_type(xb[:, :c], I32)
    lo = lax.bitcast_convert_type(xb[:, c:], I32)
    return hi | lax.shift_right_logical(lo, jnp.int32(16))


def _unpack_bf16_pairs(w):
    hi = lax.bitcast_convert_type(w & jnp.int32(-65536), F32)
    lo = lax.bitcast_convert_type(lax.shift_left(w, jnp.int32(16)), F32)
    return jnp.concatenate([hi, lo], axis=1)


def _row_block(s):
    return pl.ds(pl.multiple_of(s * ROW_BLOCK, ROW_BLOCK), ROW_BLOCK)


def _cast_kernel(x_ref, o_ref, *, valid_cols):
    x = x_ref[...]
    if valid_cols is not None:
        col = pl.program_id(1) * x.shape[1] + lax.broadcasted_iota(I32, x.shape, 1)
        x = jnp.where(col < valid_cols, x, 0.0)
    o_ref[...] = x.astype(BF16)


def _to_bf16(w, *, tn=None, out_cols=None):
    w = w.reshape(-1, w.shape[-1])
    R, cols = w.shape
    out_cols = out_cols or cols
    tn = tn or out_cols
    tm = _pick_tile(R, max(ROW_BLOCK, (8 * MIB) // (tn * 4)))
    return pl.pallas_call(
        functools.partial(_cast_kernel, valid_cols=cols if out_cols != cols else None),
        out_shape=jax.ShapeDtypeStruct((R, out_cols), BF16),
        grid=(R // tm, out_cols // tn),
        in_specs=[pl.BlockSpec((tm, tn), lambda i, j: (i, j))],
        out_specs=pl.BlockSpec((tm, tn), lambda i, j: (i, j)),
        compiler_params=pltpu.CompilerParams(dimension_semantics=("parallel", "parallel"),
                                             vmem_limit_bytes=int(12 * tm * tn + 4 * MIB)),
        name="cast_bf16",
    )(w)


def _mods_kernel(c_ref, w_ref, b_ref, o_ref):
    a = _silu(c_ref[...]).astype(BF16)
    o_ref[...] = _dot(a, w_ref[...].astype(BF16)) + b_ref[...]


def _mods(c_all, ada_w, ada_b):
    C, D = c_all.shape
    N = ada_w.shape[1]
    tn = 1024
    return pl.pallas_call(
        _mods_kernel,
        out_shape=jax.ShapeDtypeStruct((C, N), F32),
        grid=(N // tn,),
        in_specs=[pl.BlockSpec((C, D), lambda j: (0, 0)),
                  pl.BlockSpec((D, tn), lambda j: (0, j)),
                  pl.BlockSpec((1, tn), lambda j: (0, j))],
        out_specs=pl.BlockSpec((C, tn), lambda j: (0, j)),
        compiler_params=pltpu.CompilerParams(dimension_semantics=("parallel",), vmem_limit_bytes=40 * MIB),
        name="ada_mods",
    )(c_all, ada_w, ada_b.reshape(1, N))


def _nmm_kernel(*refs, nsb, has_res):
    if has_res:
        b2c_ref, h_ref, res_ref, gate_ref, shift_ref, scale_ref, g_ref, w_ref, o_ref, hnew_ref, xn_ref = refs
    else:
        b2c_ref, h_ref, shift_ref, scale_ref, g_ref, w_ref, o_ref, xn_ref = refs
    i = pl.program_id(0)

    @pl.when(pl.program_id(1) == 0)
    def _():
        def body(s, carry):
            c = b2c_ref[i * nsb + s]
            rows = _row_block(s)
            hv = h_ref[rows, :]
            if has_res:
                hv = hv + gate_ref[c] * _unpack_bf16_pairs(res_ref[rows, :])
                hnew_ref[rows, :] = hv
            xn_ref[rows, :] = _rms_mod(hv, g_ref[...], shift_ref[c], scale_ref[c]).astype(BF16)
            return carry
        lax.fori_loop(0, nsb, body, 0)

    o_ref[...] = _dot(xn_ref[...], w_ref[...]).astype(o_ref.dtype)


def _norm_matmul(b2c, h, g, shift, scale, w, *, out_dtype, tn, tm_want, res=None, gate=None):
    T, D = h.shape
    N = w.shape[1]
    C = shift.shape[0]
    tm = _pick_tile(T, tm_want)
    nsb = tm // ROW_BLOCK
    has_res = res is not None
    row_spec = pl.BlockSpec((tm, D), lambda i, j, b: (i, 0))
    mod_spec = pl.BlockSpec((C, 1, D), lambda i, j, b: (0, 0, 0))
    in_specs = [row_spec]
    args = [h]
    if has_res:
        in_specs += [pl.BlockSpec((tm, D // 2), lambda i, j, b: (i, 0)), mod_spec]
        args += [res, gate]
    in_specs += [mod_spec, mod_spec, pl.BlockSpec((1, D), lambda i, j, b: (0, 0)),
                 pl.BlockSpec((D, tn), lambda i, j, b: (0, j))]
    args += [shift, scale, g.reshape(1, D), w]
    out_shape = [jax.ShapeDtypeStruct((T, N), out_dtype)]
    out_specs = [pl.BlockSpec((tm, tn), lambda i, j, b: (i, j))]
    if has_res:
        out_shape.append(jax.ShapeDtypeStruct((T, D), F32))
        out_specs.append(row_spec)
    n_row_bufs = 3 if has_res else 1
    vmem = (2 * n_row_bufs * tm * D * 4 + tm * D * 2 + 2 * D * tn * 2
            + 2 * tm * tn * jnp.dtype(out_dtype).itemsize + 8 * MIB)
    outs = pl.pallas_call(
        functools.partial(_nmm_kernel, nsb=nsb, has_res=has_res),
        out_shape=out_shape,
        grid_spec=pltpu.PrefetchScalarGridSpec(
            num_scalar_prefetch=1, grid=(T // tm, N // tn), in_specs=in_specs, out_specs=out_specs,
            scratch_shapes=[pltpu.VMEM((tm, D), BF16)]),
        compiler_params=pltpu.CompilerParams(dimension_semantics=("parallel", "arbitrary"),
                                             vmem_limit_bytes=int(vmem)),
        name="norm_matmul_res" if has_res else "norm_matmul",
    )(b2c, *args)
    return outs if has_res else outs[0]


def _mmres_kernel(b2c_ref, a_ref, w_ref, h_ref, gate_ref, o_ref, acc_ref, *, nsb):
    i = pl.program_id(0)
    acc_ref[...] = _dot(a_ref[...], w_ref[...])

    def body(s, carry):
        c = b2c_ref[i * nsb + s]
        rows = _row_block(s)
        o_ref[rows, :] = h_ref[rows, :] + gate_ref[c] * acc_ref[rows, :]
        return carry
    lax.fori_loop(0, nsb, body, 0)


def _matmul_residual(b2c, a, w, h, gate, *, tn, tm_want):
    T, K = a.shape
    D = w.shape[1]
    C = gate.shape[0]
    tm = _pick_tile(T, tm_want)
    nsb = tm // ROW_BLOCK
    vmem = 2 * tm * K * 2 + 2 * K * tn * 2 + 5 * tm * tn * 4 + 8 * MIB
    return pl.pallas_call(
        functools.partial(_mmres_kernel, nsb=nsb),
        out_shape=jax.ShapeDtypeStruct((T, D), F32),
        grid_spec=pltpu.PrefetchScalarGridSpec(
            num_scalar_prefetch=1, grid=(T // tm, D // tn),
            in_specs=[pl.BlockSpec((tm, K), lambda i, j, b: (i, 0)),
                      pl.BlockSpec((K, tn), lambda i, j, b: (0, j)),
                      pl.BlockSpec((tm, tn), lambda i, j, b: (i, j)),
                      pl.BlockSpec((C, 1, tn), lambda i, j, b: (0, 0, j))],
            out_specs=pl.BlockSpec((tm, tn), lambda i, j, b: (i, j)),
            scratch_shapes=[pltpu.VMEM((tm, tn), F32)]),
        compiler_params=pltpu.CompilerParams(dimension_semantics=("parallel", "arbitrary"),
                                             vmem_limit_bytes=int(vmem)),
        name="matmul_residual",
    )(b2c, a, w, h, gate)


def _final_kernel(b2c_ref, h_ref, res_ref, gate_ref, shift_ref, scale_ref, g_ref, o_ref, *, nsb, blk_off):
    i = pl.program_id(0) + blk_off

    def body(s, carry):
        c = b2c_ref[i * nsb + s]
        rows = _row_block(s)
        hv = h_ref[rows, :] + gate_ref[c] * _unpack_bf16_pairs(res_ref[rows, :])
        o_ref[rows, :] = _rms_mod(hv, g_ref[...], shift_ref[c], scale_ref[c])
        return carry
    lax.fori_loop(0, nsb, body, 0)


def _final_norm(b2c, h, res, gate, shift, scale, g, *, row_off, n_rows, tm_want):
    T, D = h.shape
    C = shift.shape[0]
    tm = _pick_tile(int(np.gcd(row_off, n_rows)) if row_off else n_rows, tm_want)
    nsb = tm // ROW_BLOCK
    blk_off = row_off // tm
    row_spec = pl.BlockSpec((tm, D), lambda i, b: (i + blk_off, 0))
    mod_spec = pl.BlockSpec((C, 1, D), lambda i, b: (0, 0, 0))
    return pl.pallas_call(
        functools.partial(_final_kernel, nsb=nsb, blk_off=blk_off),
        out_shape=jax.ShapeDtypeStruct((n_rows, D), F32),
        grid_spec=pltpu.PrefetchScalarGridSpec(
            num_scalar_prefetch=1, grid=(n_rows // tm,),
            in_specs=[row_spec, pl.BlockSpec((tm, D // 2), lambda i, b: (i + blk_off, 0)), mod_spec, mod_spec, mod_spec,
                      pl.BlockSpec((1, D), lambda i, b: (0, 0))],
            out_specs=pl.BlockSpec((tm, D), lambda i, b: (i, 0))),
        compiler_params=pltpu.CompilerParams(dimension_semantics=("parallel",),
                                             vmem_limit_bytes=int(6 * tm * D * 4 + 8 * MIB)),
        name="final_norm",
    )(b2c, h, res, gate, shift, scale, g.reshape(1, D))


def _ssd_kernel(seq_ref, first_ref, last_ref,
                z_ref, x_ref, b_ref, c_ref, dt_ref,
                cpx_ref, cpb_ref, cpc_ref, sprev_ref,
                cwx_ref, cwb_ref, cwc_ref, cbx_ref, cbb_ref, cbc_ref,
                dtb_ref, a_ref, dsk_ref, ng_ref, expand_ref,
                y_ref, cnew_ref, snew_ref,
                extx_ref, extb_ref, extc_ref, st_ref, xs_ref, bm_ref, cm_ref, col_ref, rowa_ref, rowd_ref,
                ybuf_ref, cb_ref, bt_ref, lm_ref):
    Q = ROW_BLOCK
    N = SSM_D_STATE
    HP = x_ref.shape[1]
    GN = b_ref.shape[1]
    G = GN // N
    GW = HP // G
    PAIRS_PER_GROUP = GW // LANES
    CH = 2 * LANES
    k = pl.program_id(0)

    @pl.when(first_ref[k] == 1)
    def _():
        extx_ref[0:CONV_PAD, :] = cpx_ref[0]
        extb_ref[0:CONV_PAD, :] = cpb_ref[0]
        extc_ref[0:CONV_PAD, :] = cpc_ref[0]
        st_ref[...] = sprev_ref[0]

    sr = lax.broadcasted_iota(I32, ((CONV_W - 1) * Q, Q), 0)
    sc_ = lax.broadcasted_iota(I32, ((CONV_W - 1) * Q, Q), 1)
    shift = jnp.where(sc_ == (sr & (Q - 1)) + (sr // Q) - (CONV_W - 1), 1.0, 0.0).astype(BF16)

    def conv(ext_ref, cur_ref, w_ref, bias_ref, out_ref, new_off):
        for lo in range(0, cur_ref.shape[1], CH):
            cs = slice(lo, lo + CH)
            cur_bf = cur_ref[:, cs]
            cur = cur_bf.astype(F32)
            shifted = _dot(shift, cur_bf)
            acc = bias_ref[:, cs] + w_ref[CONV_W - 1:CONV_W, cs] * cur
            for j in range(CONV_W - 1):
                acc = acc + w_ref[j:j + 1, cs] * shifted[j * Q:(j + 1) * Q]
            ext_ref[CONV_PAD:2 * CONV_PAD, cs] = cur[0:CONV_PAD]
            head = bias_ref[:, cs] + w_ref[CONV_W - 1:CONV_W, cs] * cur[0:CONV_PAD]
            for j in range(CONV_W - 1):
                off = CONV_PAD - (CONV_W - 1) + j
                head = head + w_ref[j:j + 1, cs] * ext_ref[off:off + CONV_PAD, cs]
            out_ref[:, cs] = _silu(jnp.concatenate([head, acc[CONV_PAD:]], axis=0)).astype(out_ref.dtype)
            tail = cur[Q - CONV_PAD:Q]
            cnew_ref[0, :, new_off + lo:new_off + lo + CH] = tail
            ext_ref[0:CONV_PAD, cs] = tail

    conv(extx_ref, x_ref, cwx_ref, cbx_ref, xs_ref, 0)
    conv(extb_ref, b_ref, cwb_ref, cbb_ref, bm_ref, HP)
    conv(extc_ref, c_ref, cwc_ref, cbc_ref, cm_ref, HP + GN)

    dt = _softplus(dt_ref[...].astype(F32) + dtb_ref[...])
    da = dt * a_ref[...]
    rr = lax.broadcasted_iota(I32, (Q, Q), 0)
    cc = lax.broadcasted_iota(I32, (Q, Q), 1)
    tri = jnp.where(rr >= cc, 1.0, 0.0).astype(BF16)
    cs = _dot(tri, jnp.concatenate(_split_bf16(da, 3), axis=1))
    acs = cs[:, 0:LANES] + cs[:, LANES:2 * LANES] + cs[:, 2 * LANES:3 * LANES]

    both = _split_bf16(jnp.concatenate([acs, dt], axis=0), 2)
    col_ref[...] = _dot(both[0], expand_ref[...]) + _dot(both[1], expand_ref[...])

    pr = lax.broadcasted_iota(I32, (LANES // 2, 2 * LANES), 0)
    pk = lax.broadcasted_iota(I32, (LANES // 2, 2 * LANES), 1)
    esel = jnp.where(pk == jnp.where(pk < LANES, 2 * pr, 2 * pr + 1 + LANES), 1.0, 0.0).astype(BF16)
    zero = jnp.zeros((Q, LANES), BF16)

    def pair_rows(v, parts):
        out = None
        for piece in _split_bf16(v, parts):
            vbd = jnp.concatenate([jnp.concatenate([piece, zero], axis=1),
                                   jnp.concatenate([zero, piece], axis=1)], axis=0)
            r = _nt_dot(esel, vbd)
            out = r if out is None else out + r
        return out

    rowa_ref[...] = pair_rows(acs, 3)
    rowd_ref[...] = pair_rows(dt, 2)

    lane = lax.broadcasted_iota(I32, (Q, LANES), 1)
    trow = lax.broadcasted_iota(I32, (Q, LANES), 0)
    causal = trow >= (lane & (Q - 1))
    lo_half = lane < Q
    er = lax.broadcasted_iota(I32, (N, N), 0)
    ec = lax.broadcasted_iota(I32, (N, N), 1)
    eye = jnp.where(er == ec, 1.0, 0.0).astype(BF16)

    for g in range(G):
        gs = slice(g * GW, (g + 1) * GW)
        bg = bm_ref[:, g * N:(g + 1) * N]
        cg = cm_ref[:, g * N:(g + 1) * N]
        cb_ref[g] = _nt_dot(cg, jnp.concatenate([bg, bg], axis=0))
        ybuf_ref[:, gs] = _dot(cg, st_ref[:, gs].astype(BF16))
        bt_ref[g] = _nt_dot(eye, bg).astype(BF16)
    for p in range(G * PAIRS_PER_GROUP):
        ps = slice(p * LANES, (p + 1) * LANES)
        seg = col_ref[0:Q, ps] - rowa_ref[p:p + 1, :]
        decay = jnp.exp(jnp.where(causal, seg, -jnp.inf))
        lm_ref[p] = (cb_ref[p // PAIRS_PER_GROUP] * decay * rowd_ref[p:p + 1, :]).astype(BF16)
    for p in range(G * PAIRS_PER_GROUP):
        ps = slice(p * LANES, (p + 1) * LANES)
        xp = xs_ref[:, ps]
        xbd = jnp.concatenate([jnp.where(lo_half, xp, 0.0), jnp.where(lo_half, 0.0, xp)],
                              axis=0).astype(BF16)
        ybuf_ref[:, ps] = (_dot(lm_ref[p], xbd) + ybuf_ref[:, ps] * jnp.exp(col_ref[0:Q, ps])
                           + dsk_ref[:, ps] * xp)
    for g in range(G):
        gs = slice(g * GW, (g + 1) * GW)
        last = col_ref[Q - 1:Q, gs]
        w_end = col_ref[Q:2 * Q, gs] * jnp.exp(last - col_ref[0:Q, gs])
        xw = (xs_ref[:, gs] * w_end).astype(BF16)
        st_ref[:, gs] = st_ref[:, gs] * jnp.exp(last) + _dot(bt_ref[g], xw)

    sq = jnp.zeros((Q, LANES), F32)
    for lo in range(0, HP, CH):
        cs = slice(lo, lo + CH)
        yv = ybuf_ref[:, cs] * _silu(z_ref[:, cs].astype(F32))
        ybuf_ref[:, cs] = yv
        y2 = yv * yv
        for c in range(CH // LANES):
            sq = sq + y2[:, c * LANES:(c + 1) * LANES]
    inv = lax.rsqrt(jnp.sum(sq, axis=-1, keepdims=True) * (1.0 / HP) + EPS)
    for lo in range(0, HP, CH):
        cs = slice(lo, lo + CH)
        y_ref[:, cs] = (ybuf_ref[:, cs] * inv * ng_ref[:, cs]).astype(BF16)

    @pl.when(last_ref[k] == 1)
    def _():
        snew_ref[0] = st_ref[...]


def _ssd(tabs, zx, cprev, sprev, conv_w, conv_b, dt_bias, a_neg, d_skip, norm_g, expand):
    T = zx.shape[0]
    S, N, HP = sprev.shape
    GN = SSM_GROUPS * N
    Q = ROW_BLOCK
    nblk = T // Q
    bb, cb = HP // GN, HP // GN + 1
    zb = (2 * HP) // GN

    def rows(width, col_blk):
        return pl.BlockSpec((Q, width), lambda k, s, f, l: (k, col_blk))

    def per_seq(shape, col_blk):
        return pl.BlockSpec((1,) + shape, lambda k, s, f, l: (s[k], 0, col_blk))

    def const(shape, col_blk=0):
        return pl.BlockSpec(shape, lambda k, s, f, l: (0, col_blk))

    in_specs = [
        rows(HP, 0), rows(HP, 1), rows(GN, zb), rows(GN, zb + 1), rows(LANES, (2 * HP + 2 * GN) // LANES),
        per_seq((CONV_PAD, HP), 0), per_seq((CONV_PAD, GN), bb), per_seq((CONV_PAD, GN), cb),
        per_seq((N, HP), 0),
        const((CONV_W, HP)), const((CONV_W, GN), bb), const((CONV_W, GN), cb),
        const((1, HP)), const((1, GN), bb), const((1, GN), cb),
        const((1, LANES)), const((1, LANES)), const((1, HP)), const((1, HP)), const((LANES, HP)),
    ]
    out_shape = [jax.ShapeDtypeStruct((T, HP), BF16),
                 jax.ShapeDtypeStruct((S, CONV_PAD, HP + 2 * GN), F32),
                 jax.ShapeDtypeStruct((S, N, HP), F32)]
    out_specs = [pl.BlockSpec((Q, HP), lambda k, s, f, l: (k, 0)),
                 pl.BlockSpec((1, CONV_PAD, HP + 2 * GN), lambda k, s, f, l: (s[k], 0, 0)),
                 pl.BlockSpec((1, N, HP), lambda k, s, f, l: (s[k], 0, 0))]
    scratch = [pltpu.VMEM((2 * CONV_PAD, HP), F32), pltpu.VMEM((2 * CONV_PAD, GN), F32),
               pltpu.VMEM((2 * CONV_PAD, GN), F32), pltpu.VMEM((N, HP), F32), pltpu.VMEM((Q, HP), F32),
               pltpu.VMEM((Q, GN), BF16), pltpu.VMEM((Q, GN), BF16), pltpu.VMEM((2 * Q, HP), F32),
               pltpu.VMEM((LANES // 2, 2 * Q), F32), pltpu.VMEM((LANES // 2, 2 * Q), F32),
               pltpu.VMEM((Q, HP), F32), pltpu.VMEM((SSM_GROUPS, Q, 2 * Q), F32),
               pltpu.VMEM((SSM_GROUPS, N, Q), BF16), pltpu.VMEM((HP // LANES, Q, 2 * Q), BF16)]
    return pl.pallas_call(
        _ssd_kernel,
        out_shape=out_shape,
        grid_spec=pltpu.PrefetchScalarGridSpec(num_scalar_prefetch=3, grid=(nblk,), in_specs=in_specs,
                                               out_specs=out_specs, scratch_shapes=scratch),
        compiler_params=pltpu.CompilerParams(dimension_semantics=("arbitrary",), vmem_limit_bytes=48 * MIB),
        name="ssd_mixer",
    )(tabs["seq"], tabs["first"], tabs["last"],
      zx, zx, zx, zx, zx, cprev, cprev, cprev, sprev,
      conv_w, conv_w, conv_w, conv_b, conv_b, conv_b, dt_bias, a_neg, d_skip, norm_g, expand)


def _attn_kernel(wk0_ref, wk1_ref, wc0_ref, wc1_ref, uc0_ref, uc1_ref, ninv_ref,
                 q_ref, k0_ref, k1_ref, k2_ref, c0_ref, c1_ref, bias_ref, sink_ref, o_ref, sc_ref, p_ref):
    Q = ROW_BLOCK
    DH = ATT_HEAD_DIM
    KVW = k2_ref.shape[1] // 2
    n_kv = KVW // DH
    NK = bias_ref.shape[2]
    k = pl.program_id(0)
    s = lax.broadcasted_iota(I32, (Q, NK), 1)
    mask_bias = jnp.where(s >= ninv_ref[k] * Q, 0.0, -jnp.inf)
    kv = jnp.concatenate([jnp.where(uc0_ref[k] == 1, c0_ref[...], k0_ref[...]),
                          jnp.where(uc1_ref[k] == 1, c1_ref[...], k1_ref[...]), k2_ref[...]], axis=0).astype(BF16)

    for g in range(n_kv):
        qg = jnp.concatenate([q_ref[:, (g * ATT_GROUP + r) * DH:(g * ATT_GROUP + r + 1) * DH]
                              for r in range(ATT_GROUP)], axis=0)
        sc_ref[g] = _nt_dot(qg, kv[:, g * DH:(g + 1) * DH])
    sink_terms = []
    for g in range(n_kv):
        for r in range(ATT_GROUP):
            rows = slice(r * Q, (r + 1) * Q)
            sc = sc_ref[g, rows, :] * (DH ** -0.5) + bias_ref[g, rows, :] + mask_bias
            sink = sink_ref[g, rows, 0:1]
            m = jnp.maximum(jnp.max(sc, axis=-1, keepdims=True), sink)
            p_ref[g, rows, :] = jnp.exp(sc - m).astype(BF16)
            sink_terms.append(jnp.exp(sink - m))
    first_half = lax.broadcasted_iota(I32, (Q, 2 * DH), 1) < DH
    ones = jnp.ones((NK, 2 * DH), BF16)
    for g in range(n_kv):
        vg = kv[:, KVW + g * DH:KVW + (g + 1) * DH]
        og = _dot(p_ref[g], jnp.concatenate([vg, vg], axis=1))
        dn = _dot(p_ref[g], ones)
        for rp in range(ATT_GROUP // 2):
            ev, od = slice(2 * rp * Q, (2 * rp + 1) * Q), slice((2 * rp + 1) * Q, (2 * rp + 2) * Q)
            s_ev, s_od = sink_terms[g * ATT_GROUP + 2 * rp], sink_terms[g * ATT_GROUP + 2 * rp + 1]
            pair = jnp.where(first_half, og[ev] * (1.0 / (dn[ev] + s_ev)), og[od] * (1.0 / (dn[od] + s_od)))
            h0 = g * ATT_GROUP + 2 * rp
            o_ref[:, h0 * DH:(h0 + 2) * DH] = pair.astype(BF16)


def _attention(tabs, q, kv, cache_kv, sinks):
    T, HD = q.shape
    Q = ROW_BLOCK
    NK = (WINDOW_BLOCKS + 1) * Q
    n_heads = HD // ATT_HEAD_DIM
    KV2 = kv.shape[1]
    n_kv = KV2 // (2 * ATT_HEAD_DIM)
    GQ = ATT_GROUP * Q
    slopes = 2.0 ** (-8.0 * np.arange(1, n_heads + 1) / n_heads)
    dist = np.abs(np.arange(Q)[:, None] + WINDOW_BLOCKS * Q - np.arange(NK)[None, :])
    bias = jnp.asarray((-slopes[:, None, None] * dist[None]).reshape(n_kv, GQ, NK).astype(np.float32))
    sink_rows = jnp.broadcast_to(jnp.repeat(sinks.reshape(n_kv, ATT_GROUP), Q, axis=1)[:, :, None],
                                 (n_kv, GQ, LANES))

    def blk(pick):
        return pl.BlockSpec((Q, KV2), lambda k, wk0, wk1, wc0, wc1, uc0, uc1, n: (pick(k, wk0, wk1, wc0, wc1), 0))

    def const(shape):
        return pl.BlockSpec(shape, lambda k, *_: (0, 0, 0))

    return pl.pallas_call(
        _attn_kernel,
        out_shape=jax.ShapeDtypeStruct((T, HD), BF16),
        grid_spec=pltpu.PrefetchScalarGridSpec(
            num_scalar_prefetch=7, grid=(T // Q,),
            in_specs=[pl.BlockSpec((Q, HD), lambda k, *_: (k, 0)),
                      blk(lambda k, wk0, wk1, wc0, wc1: wk0[k]), blk(lambda k, wk0, wk1, wc0, wc1: wk1[k]),
                      blk(lambda k, wk0, wk1, wc0, wc1: k),
                      blk(lambda k, wk0, wk1, wc0, wc1: wc0[k]), blk(lambda k, wk0, wk1, wc0, wc1: wc1[k]),
                      const((n_kv, GQ, NK)), const((n_kv, GQ, LANES))],
            out_specs=pl.BlockSpec((Q, HD), lambda k, *_: (k, 0)),
            scratch_shapes=[pltpu.VMEM((n_kv, GQ, NK), F32), pltpu.VMEM((n_kv, GQ, NK), BF16)]),
        compiler_params=pltpu.CompilerParams(dimension_semantics=("parallel",), vmem_limit_bytes=32 * MIB),
        name="swa_attention",
    )(tabs["wk0"], tabs["wk1"], tabs["wc0"], tabs["wc1"], tabs["uc0"], tabs["uc1"], tabs["ninv"],
      q, kv, kv, kv, cache_kv, cache_kv, bias, sink_rows)


def _router_kernel(b2c_ref, h_ref, shift_ref, scale_ref, g_ref, wh_ref, wl_ref, rb_ref, hn_ref, meta_ref,
                   xh_ref, xl_ref, wt_ref, *, nsb):
    i = pl.program_id(0)

    def body(s, carry):
        c = b2c_ref[i * nsb + s]
        rows = _row_block(s)
        xn = _rms_mod(h_ref[rows, :], g_ref[...], shift_ref[c], scale_ref[c])
        hn_ref[rows, 0:h_ref.shape[1] // 2] = _pack_bf16_pairs(xn)
        hi, lo = _split_bf16(xn, 2)
        xh_ref[rows, :] = hi
        xl_ref[rows, :] = lo
        return carry
    lax.fori_loop(0, nsb, body, 0)

    lt = (_nt_dot(wh_ref[...], xh_ref[...]) + _nt_dot(wh_ref[...], xl_ref[...])
          + _nt_dot(wl_ref[...], xh_ref[...]) + rb_ref[...])

    def first_max(vals):
        m = vals[0]
        for v in vals[1:]:
            m = jnp.maximum(m, v)
        idx = jnp.full(m.shape, len(vals) - 1, I32)
        for j in range(len(vals) - 2, -1, -1):
            idx = jnp.where(vals[j] == m, j, idx)
        return m, idx

    lg = [lt[j:j + 1, :] for j in range(MOE_GROUPS)]
    mg, gi = first_max(lg)
    p_sel = 1.0 / sum(jnp.exp(v - mg) for v in lg)
    le = []
    for j in range(MOE_PER_GROUP):
        v = lt[MOE_GROUPS + (MOE_GROUPS - 1) * MOE_PER_GROUP + j:MOE_GROUPS + (MOE_GROUPS - 1) * MOE_PER_GROUP + j + 1, :]
        for grp in range(MOE_GROUPS - 2, -1, -1):
            row = MOE_GROUPS + grp * MOE_PER_GROUP + j
            v = jnp.where(gi == grp, lt[row:row + 1, :], v)
        le.append(v)
    m1, i1 = first_max(le)
    m2, i2 = first_max([jnp.where(i1 == j, -jnp.inf, le[j]) for j in range(MOE_PER_GROUP)])
    e2 = jnp.exp(m2 - m1)
    w1 = p_sel * (1.0 / (1.0 + e2))
    w2 = p_sel * (e2 / (1.0 + e2))
    lo_i = jnp.minimum(i1, i2)
    hi_i = jnp.maximum(i1, i2)
    first_is_lo = i1 < i2
    pair_base = jnp.where(lo_i == 0, 0, jnp.where(lo_i == 1, 3, 5))
    bucket = gi * MOE_PAIRS + pair_base + hi_i - lo_i - 1
    meta_ref[0:1, :] = bucket.astype(F32)
    meta_ref[1:8, :] = jnp.zeros((7, meta_ref.shape[1]), F32)
    wt_ref[...] = jnp.zeros(wt_ref.shape, F32)
    wt_ref[0:1, :] = jnp.where(first_is_lo, w1, w2)
    wt_ref[1:2, :] = jnp.where(first_is_lo, w2, w1)
    DW = h_ref.shape[1] // 2
    hn_ref[:, DW:DW + LANES] = lax.bitcast_convert_type(wt_ref[...].T, I32)


def _router(b2c, h, g, shift, scale, w_hi, w_lo, rbias, *, tm_want):
    T, D = h.shape
    C = shift.shape[0]
    tm = _pick_tile(T, tm_want)
    nsb = tm // ROW_BLOCK
    row_spec = pl.BlockSpec((tm, D), lambda i, b: (i, 0))
    mod_spec = pl.BlockSpec((C, 1, D), lambda i, b: (0, 0, 0))
    rb = jnp.broadcast_to(rbias.reshape(LANES, 1), (LANES, tm))
    return pl.pallas_call(
        functools.partial(_router_kernel, nsb=nsb),
        out_shape=[jax.ShapeDtypeStruct((T, D // 2 + LANES), I32), jax.ShapeDtypeStruct((8, T), F32)],
        grid_spec=pltpu.PrefetchScalarGridSpec(
            num_scalar_prefetch=1, grid=(T // tm,),
            in_specs=[row_spec, mod_spec, mod_spec, pl.BlockSpec((1, D), lambda i, b: (0, 0)),
                      pl.BlockSpec((LANES, D), lambda i, b: (0, 0)), pl.BlockSpec((LANES, D), lambda i, b: (0, 0)),
                      pl.BlockSpec((LANES, tm), lambda i, b: (0, 0))],
            out_specs=[pl.BlockSpec((tm, D // 2 + LANES), lambda i, b: (i, 0)),
                       pl.BlockSpec((8, tm), lambda i, b: (0, i))],
            scratch_shapes=[pltpu.VMEM((tm, D), BF16), pltpu.VMEM((tm, D), BF16), pltpu.VMEM((LANES, tm), F32)]),
        compiler_params=pltpu.CompilerParams(dimension_semantics=("parallel",),
                                             vmem_limit_bytes=int(5 * tm * D * 4 + 12 * MIB)),
        name="moe_router",
    )(b2c, h, shift, scale, g.reshape(1, D), w_hi, w_lo, rb)


def _gather_rows(src, idx):
    n = idx.shape[0]
    D = src.shape[1]
    n_workers = SC_CORES * SC_SUBCORES
    per_worker = n // n_workers
    n_chunks = per_worker // SC_GATHER_ROWS
    assert n == n_workers * n_chunks * SC_GATHER_ROWS, (n, n_workers, SC_GATHER_ROWS)
    mesh = plsc.VectorSubcoreMesh(core_axis_name="c", subcore_axis_name="s", num_cores=SC_CORES,
                                  num_subcores=SC_SUBCORES)

    @functools.partial(
        pl.kernel, mesh=mesh, out_type=jax.ShapeDtypeStruct((n, D), src.dtype),
        scratch_types=[pltpu.VMEM((SC_GATHER_ROWS,), I32), pltpu.VMEM((SC_GATHER_ROWS, D), src.dtype),
                       pltpu.SemaphoreType.DMA],
        name="sc_row_gather")
    def gather(src_hbm, idx_hbm, out_hbm, idx_v, rows_v, sem):
        base = (lax.axis_index("s") * SC_CORES + lax.axis_index("c")) * per_worker

        @pl.loop(0, n_chunks)
        def _(j):
            off = base + j * SC_GATHER_ROWS
            pltpu.sync_copy(idx_hbm.at[pl.ds(off, SC_GATHER_ROWS)], idx_v)
            pltpu.async_copy(src_hbm.at[idx_v], rows_v, sem).wait()
            pltpu.sync_copy(rows_v, out_hbm.at[pl.ds(off, SC_GATHER_ROWS)])

    return gather(src, idx)


def _ffn_kernel(ea_ref, eb_ref, nv_ref, x_ref, wga_ref, wua_ref, wda_ref, wgb_ref, wub_ref, wdb_ref, y_ref):
    i = pl.program_id(0)
    DW = y_ref.shape[1]

    @pl.when(nv_ref[i] > 0)
    def _():
        x = _unpack_bf16_pairs(x_ref[:, 0:DW]).astype(BF16)

        def expert(wg_ref, wu_ref, wd_ref):
            hid = _silu(_dot(x, wg_ref[0])) * _dot(x, wu_ref[0])
            return _dot(hid.astype(BF16), wd_ref[0])
        wa = lax.bitcast_convert_type(x_ref[:, DW:DW + 1], F32)
        wb = lax.bitcast_convert_type(x_ref[:, DW + 1:DW + 2], F32)
        y_ref[...] = _pack_bf16_pairs(wa * expert(wga_ref, wua_ref, wda_ref) + wb * expert(wgb_ref, wub_ref, wdb_ref))

    @pl.when(nv_ref[i] == 0)
    def _():
        y_ref[...] = jnp.zeros(y_ref.shape, I32)


def _moe_ffn(plan, xs, w_g, w_u, w_d):
    R, DX = xs.shape
    tm = MOE_TILE
    E, F, D = w_d.shape
    up_a = pl.BlockSpec((1, D, F), lambda i, a, b, n: (a[i], 0, 0))
    up_b = pl.BlockSpec((1, D, F), lambda i, a, b, n: (b[i], 0, 0))
    return pl.pallas_call(
        _ffn_kernel,
        out_shape=jax.ShapeDtypeStruct((R, D // 2), I32),
        grid_spec=pltpu.PrefetchScalarGridSpec(
            num_scalar_prefetch=3, grid=(R // tm,),
            in_specs=[pl.BlockSpec((tm, DX), lambda i, a, b, n: (i, 0)),
                      up_a, up_a, pl.BlockSpec((1, F, D), lambda i, a, b, n: (a[i], 0, 0)),
                      up_b, up_b, pl.BlockSpec((1, F, D), lambda i, a, b, n: (b[i], 0, 0))],
            out_specs=pl.BlockSpec((tm, D // 2), lambda i, a, b, n: (i, 0))),
        compiler_params=pltpu.CompilerParams(dimension_semantics=("arbitrary",), vmem_limit_bytes=48 * MIB),
        name="moe_experts",
    )(plan["ea"], plan["eb"], plan["nvalid"], xs, w_g, w_u, w_d, w_g, w_u, w_d)


def _moe_plan(meta, T, expert_off):
    tm = MOE_TILE
    n_tiles = -(-T // tm) + MOE_BUCKETS
    bucket = meta[0].astype(I32)
    onehot = (bucket[:, None] == jnp.arange(MOE_BUCKETS, dtype=I32)[None, :]).astype(F32)
    blocks = onehot.reshape(T // LANES, LANES, MOE_BUCKETS)
    within = jnp.einsum("ij,bjk->bik", jnp.tril(jnp.ones((LANES, LANES), F32)), blocks)
    totals = within[:, -1, :]
    n_blk = T // LANES
    before = jnp.einsum("ij,jk->ik", jnp.tril(jnp.ones((n_blk, n_blk), F32), -1), totals,
                        precision=lax.Precision.HIGHEST)
    cum = (within + before[:, None, :]).reshape(T, MOE_BUCKETS)
    rank = jnp.sum(cum * onehot, axis=1).astype(I32) - 1
    counts = jnp.sum(totals, axis=0).astype(I32)
    ntile_b = (counts + tm - 1) // tm
    tend_b = jnp.cumsum(ntile_b)
    tstart_b = tend_b - ntile_b
    dest = tstart_b[bucket] * tm + rank
    tok_of_pos = jnp.zeros((n_tiles * tm,), I32).at[dest].set(jnp.arange(T, dtype=I32))
    tiles = jnp.arange(n_tiles, dtype=I32)
    tile_b = jnp.minimum(jnp.sum((tiles[:, None] >= tend_b[None, :]).astype(I32), axis=1), MOE_BUCKETS - 1)
    nvalid = jnp.where(tiles < tend_b[-1],
                       jnp.clip(counts[tile_b] - (tiles - tstart_b[tile_b]) * tm, 0, tm), 0).astype(I32)
    grp = tile_b // MOE_PAIRS
    pair = tile_b % MOE_PAIRS
    ea = expert_off + grp * MOE_PER_GROUP + jnp.asarray(MOE_PAIR_LO, I32)[pair]
    eb = expert_off + grp * MOE_PER_GROUP + jnp.asarray(MOE_PAIR_HI, I32)[pair]
    return dict(dest=dest, tok_of_pos=tok_of_pos, nvalid=nvalid, ea=ea.astype(I32), eb=eb.astype(I32))


def _hmoe(b2c, h, g, shift, scale, router, w_g, w_u, w_d, expert_off):
    T = h.shape[0]
    hn, meta = _router(b2c, h, g, shift, scale, *router, tm_want=512)
    plan = _moe_plan(meta, T, expert_off)
    xs = _gather_rows(hn, plan["tok_of_pos"])
    ys = _moe_ffn(plan, xs, w_g, w_u, w_d)
    return _gather_rows(ys, plan["dest"])


def _tables(Bp, Lp, Bs, Ls):
    nbp, nbs = Lp // ROW_BLOCK, Ls // ROW_BLOCK
    cols = dict(seq=[], first=[], last=[], ninv=[], wk0=[], wk1=[], wc0=[], wc1=[], uc0=[], uc1=[])
    k = 0
    for b in range(Bp + Bs):
        nb = nbp if b < Bp else nbs
        for c in range(nb):
            cols["seq"].append(b)
            cols["first"].append(int(c == 0))
            cols["last"].append(int(c == nb - 1))
            cols["ninv"].append(max(0, WINDOW_BLOCKS - c) if b < Bp else 0)
            for j in range(WINDOW_BLOCKS):
                hist = c - WINDOW_BLOCKS + j
                from_cache = b >= Bp and hist < 0
                cols[f"uc{j}"].append(int(from_cache))
                cols[f"wc{j}"].append((b - Bp) * WINDOW_BLOCKS + WINDOW_BLOCKS + hist if from_cache else 0)
                cols[f"wk{j}"].append(k if from_cache else max(k - WINDOW_BLOCKS + j, 0))
            k += 1
    return {name: jnp.asarray(np.asarray(v, np.int32)) for name, v in cols.items()}


def _router_weights(router_g, bias_g, router_e, bias_e):
    D = router_g.shape[0]
    w = jnp.zeros((LANES, D), F32)
    w = w.at[:MOE_GROUPS].set(router_g.T).at[MOE_GROUPS:MOE_GROUPS + router_e.shape[1]].set(router_e.T)
    w_hi = w.astype(BF16)
    w_lo = (w - w_hi.astype(F32)).astype(BF16)
    rb = jnp.zeros((LANES,), F32).at[:MOE_GROUPS].set(bias_g).at[MOE_GROUPS:MOE_GROUPS + bias_e.shape[0]].set(bias_e)
    return w_hi, w_lo, rb


def kernel(x_prompt, x_sample, state_conv, state_ssm, cache_k, cache_v, c_prompt, c_sample, ada_w, ada_b, norm_mix, norm_ffn, norm_kv, norm_out, ssm_w_in, ssm_conv_w, ssm_conv_b, ssm_dt_bias, ssm_a_log, ssm_d, ssm_norm, ssm_w_out, attn_w_kv, attn_w_q, attn_sinks, attn_w_o, moe_router_g, moe_bias_g, moe_router_e, moe_bias_e, moe_w_gate, moe_w_up, moe_w_down):
    Bp, Lp, D = x_prompt.shape
    Bs, Ls, _ = x_sample.shape
    Tp, Ts = Bp * Lp, Bs * Ls
    T = Tp + Ts
    C = Bp + Bs
    tabs = _tables(Bp, Lp, Bs, Ls)
    b2c = tabs["seq"]

    n_mod = ada_w.shape[1] // D
    mods = _mods(jnp.concatenate([c_prompt, c_sample], axis=0), ada_w, ada_b).reshape(C, n_mod, D)
    mod = lambda i: mods[:, i:i + 1, :]
    x = jnp.concatenate([x_prompt.reshape(Tp, D), x_sample.reshape(Ts, D)], axis=0)

    HP = ssm_w_out.shape[1]
    H = HP // SSM_HEAD_DIM
    N = SSM_D_STATE
    GN = SSM_GROUPS * N
    conv_dim = HP + 2 * GN
    in_cols = 2 * HP + 2 * GN + LANES
    w_in = _to_bf16(ssm_w_in[0], tn=in_cols // 9, out_cols=in_cols)
    zx = _norm_matmul(b2c, x, norm_mix[0], mod(0), mod(1), w_in, out_dtype=BF16, tn=in_cols // 9, tm_want=1024)
    cprev = jnp.zeros((C, CONV_PAD, conv_dim), F32).at[Bp:, CONV_PAD - (CONV_W - 1):].set(state_conv[0])
    sprev = jnp.concatenate([jnp.zeros((Bp, N, HP), F32),
                             jnp.transpose(state_ssm[0], (0, 3, 1, 2)).reshape(Bs, N, HP)], axis=0)
    pad_h = lambda v: jnp.pad(v.astype(F32), (0, LANES - H)).reshape(1, LANES)
    expand = (jnp.arange(LANES)[:, None] == (jnp.arange(HP) // SSM_HEAD_DIM)[None, :]).astype(BF16)
    y_ssd, cnew, snew = _ssd(
        tabs, zx, cprev, sprev, ssm_conv_w[0], ssm_conv_b[0].reshape(1, conv_dim), pad_h(ssm_dt_bias[0]),
        pad_h(-jnp.exp(ssm_a_log[0].astype(F32))), jnp.repeat(ssm_d[0].astype(F32), SSM_HEAD_DIM).reshape(1, HP),
        ssm_norm[0].reshape(1, HP), expand)
    h = _matmul_residual(b2c, y_ssd, _to_bf16(ssm_w_out[0]), x, mod(2), tn=512, tm_want=1024)

    n_exp, _, moe_ff = moe_w_gate.shape[1:]
    w_gate = _to_bf16(moe_w_gate).reshape(-1, D, moe_ff)
    w_up = _to_bf16(moe_w_up).reshape(-1, D, moe_ff)
    w_down = _to_bf16(moe_w_down).reshape(-1, moe_ff, D)

    def experts(layer):
        return w_gate, w_up, w_down, layer * n_exp

    def router(layer):
        return _router_weights(moe_router_g[layer], moe_bias_g[layer], moe_router_e[layer], moe_bias_e[layer])

    moe0 = _hmoe(b2c, h, norm_ffn[0], mod(3), mod(4), router(0), *experts(0))

    kv, h = _norm_matmul(b2c, h, norm_kv, mod(12), mod(13), _to_bf16(attn_w_kv), out_dtype=F32,
                         tn=attn_w_kv.shape[1], tm_want=512, res=moe0, gate=mod(5))
    KV2 = kv.shape[1]
    KVW = KV2 // 2
    W = WINDOW_BLOCKS * ROW_BLOCK
    cache_kv = jnp.concatenate([cache_k.reshape(Bs, W, KVW), cache_v.reshape(Bs, W, KVW)], axis=-1)
    kvp = kv[:Tp].reshape(Bp, Lp, KV2)
    kvs = jnp.concatenate([cache_kv, kv[Tp:].reshape(Bs, Ls, KV2)], axis=1)
    q = _norm_matmul(b2c, h, norm_mix[1], mod(6), mod(7), _to_bf16(attn_w_q[0]), out_dtype=BF16,
                     tn=1024, tm_want=1024)
    o = _attention(tabs, q, kv, cache_kv.reshape(Bs * W, KV2), attn_sinks[0].astype(F32))
    h = _matmul_residual(b2c, o, _to_bf16(attn_w_o[0]), h, mod(8), tn=1024, tm_want=1024)
    moe1 = _hmoe(b2c, h, norm_ffn[1], mod(9), mod(10), router(1), *experts(1))

    fin = functools.partial(_final_norm, b2c, h, moe1, mod(11), mod(14), mod(15), norm_out, tm_want=512)
    y_prompt = fin(row_off=0, n_rows=Tp).reshape(Bp, Lp, D)
    y_sample = fin(row_off=Tp, n_rows=Ts).reshape(Bs, Ls, D)

    kv_heads = KVW // ATT_HEAD_DIM
    tail = lambda a, lo: a[:, -W:, lo:lo + KVW].reshape(a.shape[0], W, kv_heads, ATT_HEAD_DIM)
    conv_tail = cnew[:, CONV_PAD - (CONV_W - 1):]
    ssm_new = jnp.transpose(snew.reshape(C, N, H, SSM_HEAD_DIM), (0, 2, 3, 1))
    return (y_prompt, y_sample, conv_tail[None, :Bp], ssm_new[None, :Bp], tail(kvp, 0), tail(kvp, KVW),
            conv_tail[None, Bp:], ssm_new[None, Bp:], tail(kvs, 0), tail(kvs, KVW))
```

```python
import functools

import numpy as np
import jax
import jax.numpy as jnp
from jax import lax
from jax.experimental import pallas as pl
from jax.experimental.pallas import tpu as pltpu
from jax.experimental.pallas import tpu_sc as plsc

F32 = jnp.float32
BF16 = jnp.bfloat16
I32 = jnp.int32
EPS = 1e-6
ROW_BLOCK = 64
WINDOW_BLOCKS = 2
LANES = 128
MIB = 1024 * 1024

SSM_HEAD_DIM = 64
SSM_GROUPS = 8
SSM_D_STATE = 128
CONV_W = 4
CONV_PAD = 8
ATT_HEAD_DIM = 64
ATT_GROUP = 8
MOE_GROUPS = 4
MOE_PER_GROUP = 4
MOE_PAIR_LO = (0, 0, 0, 1, 1, 2)
MOE_PAIR_HI = (1, 2, 3, 2, 3, 3)
MOE_PAIRS = len(MOE_PAIR_LO)
MOE_BUCKETS = MOE_GROUPS * MOE_PAIRS
MOE_TILE = 256
SC_CORES = 2
SC_SUBCORES = 16
SC_MAX_INDICES = 128
SC_CHUNK_BYTES = 80 * 1024


def _pick_tile(n, want):
    t = min(want, n)
    t -= t % ROW_BLOCK
    while n % t:
        t -= ROW_BLOCK
    return t


def _silu(x):
    return (0.5 * x) * (1.0 + jnp.tanh(0.5 * x))


def _softplus(x):
    return jnp.maximum(x, 0.0) + jnp.log1p(jnp.exp(-jnp.abs(x)))


def _rms_mod(hv, g, shift, scale):
    ms = jnp.mean(hv * hv, axis=-1, keepdims=True)
    return (hv * lax.rsqrt(ms + EPS) * g) * (1.0 + scale) + shift


def _dot(a, b):
    return jnp.dot(a, b, preferred_element_type=F32)


def _nt_dot(a, b):
    return lax.dot_general(a, b, (((1,), (1,)), ((), ())), preferred_element_type=F32)


def _split_bf16(x, parts):
    out = []
    r = x
    for _ in range(parts):
        p = r.astype(BF16)
        out.append(p)
        r = r - p.astype(F32)
    return out


def _pack_bf16_pairs(x):
    c = x.shape[1] // 2
    xb = x.astype(BF16).astype(F32)
    hi = lax.bitcast_convert_type(xb[:, :c], I32)
    lo = lax.bitcast_convert_type(xb[:, c:], I32)
    return hi | lax.shift_right_logical(lo, jnp.int32(16))


def _unpack_bf16_pairs(w):
    hi = lax.bitcast_convert_type(w & jnp.int32(-65536), F32)
    lo = lax.bitcast_convert_type(lax.shift_left(w, jnp.int32(16)), F32)
    return jnp.concatenate([hi, lo], axis=1)


def _row_block(s):
    return pl.ds(pl.multiple_of(s * ROW_BLOCK, ROW_BLOCK), ROW_BLOCK)


def _cast_kernel(x_ref, o_ref, *, valid_cols):
    x = x_ref[...]
    if valid_cols is not None:
        col = pl.program_id(1) * x.shape[1] + lax.broadcasted_iota(I32, x.shape, 1)
        x = jnp.where(col < valid_cols, x, 0.0)
    o_ref[...] = x.astype(BF16)


def _to_bf16(w, *, tn=None, out_cols=None):
    w = w.reshape(-1, w.shape[-1])
    R, cols = w.shape
    out_cols = out_cols or cols
    tn = tn or out_cols
    tm = _pick_tile(R, max(ROW_BLOCK, (8 * MIB) // (tn * 4)))
    return pl.pallas_call(
        functools.partial(_cast_kernel, valid_cols=cols if out_cols != cols else None),
        out_shape=jax.ShapeDtypeStruct((R, out_cols), BF16),
        grid=(R // tm, out_cols // tn),
        in_specs=[pl.BlockSpec((tm, tn), lambda i, j: (i, j))],
        out_specs=pl.BlockSpec((tm, tn), lambda i, j: (i, j)),
        compiler_params=pltpu.CompilerParams(dimension_semantics=("parallel", "parallel"),
                                             vmem_limit_bytes=int(12 * tm * tn + 4 * MIB)),
        name="cast_bf16",
    )(w)


def _mods_kernel(c_ref, w_ref, b_ref, o_ref):
    a = _silu(c_ref[...]).astype(BF16)
    o_ref[...] = _dot(a, w_ref[...].astype(BF16)) + b_ref[...]


def _mods(c_all, ada_w, ada_b):
    C, D = c_all.shape
    N = ada_w.shape[1]
    tn = 1024
    return pl.pallas_call(
        _mods_kernel,
        out_shape=jax.ShapeDtypeStruct((C, N), F32),
        grid=(N // tn,),
        in_specs=[pl.BlockSpec((C, D), lambda j: (0, 0)),
                  pl.BlockSpec((D, tn), lambda j: (0, j)),
                  pl.BlockSpec((1, tn), lambda j: (0, j))],
        out_specs=pl.BlockSpec((C, tn), lambda j: (0, j)),
        compiler_params=pltpu.CompilerParams(dimension_semantics=("parallel",), vmem_limit_bytes=40 * MIB),
        name="ada_mods",
    )(c_all, ada_w, ada_b.reshape(1, N))


def _nmm_kernel(*refs, nsb, has_res):
    if has_res:
        b2c_ref, h_ref, res_ref, gate_ref, shift_ref, scale_ref, g_ref, w_ref, o_ref, hnew_ref, xn_ref = refs
    else:
        b2c_ref, h_ref, shift_ref, scale_ref, g_ref, w_ref, o_ref, xn_ref = refs
    i = pl.program_id(0)

    @pl.when(pl.program_id(1) == 0)
    def _():
        def body(s, carry):
            c = b2c_ref[i * nsb + s]
            rows = _row_block(s)
            hv = h_ref[rows, :]
            if has_res:
                hv = hv + gate_ref[c] * _unpack_bf16_pairs(res_ref[rows, :])
                hnew_ref[rows, :] = hv
            xn_ref[rows, :] = _rms_mod(hv, g_ref[...], shift_ref[c], scale_ref[c]).astype(BF16)
            return carry
        lax.fori_loop(0, nsb, body, 0)

    o_ref[...] = _dot(xn_ref[...], w_ref[...]).astype(o_ref.dtype)


def _norm_matmul(b2c, h, g, shift, scale, w, *, out_dtype, tn, tm_want, res=None, gate=None):
    T, D = h.shape
    N = w.shape[1]
    C = shift.shape[0]
    tm = _pick_tile(T, tm_want)
    nsb = tm // ROW_BLOCK
    has_res = res is not None
    row_spec = pl.BlockSpec((tm, D), lambda i, j, b: (i, 0))
    mod_spec = pl.BlockSpec((C, 1, D), lambda i, j, b: (0, 0, 0))
    in_specs = [row_spec]
    args = [h]
    if has_res:
        in_specs += [pl.BlockSpec((tm, D // 2), lambda i, j, b: (i, 0)), mod_spec]
        args += [res, gate]
    in_specs += [mod_spec, mod_spec, pl.BlockSpec((1, D), lambda i, j, b: (0, 0)),
                 pl.BlockSpec((D, tn), lambda i, j, b: (0, j))]
    args += [shift, scale, g.reshape(1, D), w]
    out_shape = [jax.ShapeDtypeStruct((T, N), out_dtype)]
    out_specs = [pl.BlockSpec((tm, tn), lambda i, j, b: (i, j))]
    if has_res:
        out_shape.append(jax.ShapeDtypeStruct((T, D), F32))
        out_specs.append(row_spec)
    n_row_bufs = 3 if has_res else 1
    vmem = (2 * n_row_bufs * tm * D * 4 + tm * D * 2 + 2 * D * tn * 2
            + 2 * tm * tn * jnp.dtype(out_dtype).itemsize + 8 * MIB)
    outs = pl.pallas_call(
        functools.partial(_nmm_kernel, nsb=nsb, has_res=has_res),
        out_shape=out_shape,
        grid_spec=pltpu.PrefetchScalarGridSpec(
            num_scalar_prefetch=1, grid=(T // tm, N // tn), in_specs=in_specs, out_specs=out_specs,
            scratch_shapes=[pltpu.VMEM((tm, D), BF16)]),
        compiler_params=pltpu.CompilerParams(dimension_semantics=("parallel", "arbitrary"),
                                             vmem_limit_bytes=int(vmem)),
        name="norm_matmul_res" if has_res else "norm_matmul",
    )(b2c, *args)
    return outs if has_res else outs[0]


def _mmres_kernel(b2c_ref, a_ref, w_ref, h_ref, gate_ref, o_ref, acc_ref, *, nsb):
    i = pl.program_id(0)
    acc_ref[...] = _dot(a_ref[...], w_ref[...])

    def body(s, carry):
        c = b2c_ref[i * nsb + s]
        rows = _row_block(s)
        o_ref[rows, :] = h_ref[rows, :] + gate_ref[c] * acc_ref[rows, :]
        return carry
    lax.fori_loop(0, nsb, body, 0)


def _matmul_residual(b2c, a, w, h, gate, *, tn, tm_want):
    T, K = a.shape
    D = w.shape[1]
    C = gate.shape[0]
    tm = _pick_tile(T, tm_want)
    nsb = tm // ROW_BLOCK
    vmem = 2 * tm * K * 2 + 2 * K * tn * 2 + 5 * tm * tn * 4 + 8 * MIB
    return pl.pallas_call(
        functools.partial(_mmres_kernel, nsb=nsb),
        out_shape=jax.ShapeDtypeStruct((T, D), F32),
        grid_spec=pltpu.PrefetchScalarGridSpec(
            num_scalar_prefetch=1, grid=(T // tm, D // tn),
            in_specs=[pl.BlockSpec((tm, K), lambda i, j, b: (i, 0)),
                      pl.BlockSpec((K, tn), lambda i, j, b: (0, j)),
                      pl.BlockSpec((tm, tn), lambda i, j, b: (i, j)),
                      pl.BlockSpec((C, 1, tn), lambda i, j, b: (0, 0, j))],
            out_specs=pl.BlockSpec((tm, tn), lambda i, j, b: (i, j)),
            scratch_shapes=[pltpu.VMEM((tm, tn), F32)]),
        compiler_params=pltpu.CompilerParams(dimension_semantics=("parallel", "arbitrary"),
                                             vmem_limit_bytes=int(vmem)),
        name="matmul_residual",
    )(b2c, a, w, h, gate)


def _final_kernel(b2c_ref, h_ref, res_ref, gate_ref, shift_ref, scale_ref, g_ref, o_ref, *, nsb, blk_off):
    i = pl.program_id(0) + blk_off

    def body(s, carry):
        c = b2c_ref[i * nsb + s]
        rows = _row_block(s)
        hv = h_ref[rows, :] + gate_ref[c] * _unpack_bf16_pairs(res_ref[rows, :])
        o_ref[rows, :] = _rms_mod(hv, g_ref[...], shift_ref[c], scale_ref[c])
        return carry
    lax.fori_loop(0, nsb, body, 0)


def _final_norm(b2c, h, res, gate, shift, scale, g, *, row_off, n_rows, tm_want):
    T, D = h.shape
    C = shift.shape[0]
    tm = _pick_tile(int(np.gcd(row_off, n_rows)) if row_off else n_rows, tm_want)
    nsb = tm // ROW_BLOCK
    blk_off = row_off // tm
    row_spec = pl.BlockSpec((tm, D), lambda i, b: (i + blk_off, 0))
    mod_spec = pl.BlockSpec((C, 1, D), lambda i, b: (0, 0, 0))
    return pl.pallas_call(
        functools.partial(_final_kernel, nsb=nsb, blk_off=blk_off),
        out_shape=jax.ShapeDtypeStruct((n_rows, D), F32),
        grid_spec=pltpu.PrefetchScalarGridSpec(
            num_scalar_prefetch=1, grid=(n_rows // tm,),
            in_specs=[row_spec, pl.BlockSpec((tm, D // 2), lambda i, b: (i + blk_off, 0)), mod_spec, mod_spec, mod_spec,
                      pl.BlockSpec((1, D), lambda i, b: (0, 0))],
            out_specs=pl.BlockSpec((tm, D), lambda i, b: (i, 0))),
        compiler_params=pltpu.CompilerParams(dimension_semantics=("parallel",),
                                             vmem_limit_bytes=int(6 * tm * D * 4 + 8 * MIB)),
        name="final_norm",
    )(b2c, h, res, gate, shift, scale, g.reshape(1, D))


def _ssd_kernel(seq_ref, first_ref, last_ref,
                z_ref, x_ref, b_ref, c_ref, dt_ref,
                cpx_ref, cpb_ref, cpc_ref, sprev_ref,
                cwx_ref, cwb_ref, cwc_ref, cbx_ref, cbb_ref, cbc_ref,
                dtb_ref, a_ref, dsk_ref, ng_ref, expand_ref,
                y_ref, cnew_ref, snew_ref,
                extx_ref, extb_ref, extc_ref, st_ref, xs_ref, bm_ref, cm_ref, col_ref, rowa_ref, rowd_ref,
                ybuf_ref, cb_ref, bt_ref, lm_ref):
    Q = ROW_BLOCK
    N = SSM_D_STATE
    HP = x_ref.shape[1]
    GN = b_ref.shape[1]
    G = GN // N
    GW = HP // G
    PAIRS_PER_GROUP = GW // LANES
    CH = 2 * LANES
    k = pl.program_id(0)

    @pl.when(first_ref[k] == 1)
    def _():
        extx_ref[0:CONV_PAD, :] = cpx_ref[0]
        extb_ref[0:CONV_PAD, :] = cpb_ref[0]
        extc_ref[0:CONV_PAD, :] = cpc_ref[0]
        st_ref[...] = sprev_ref[0]

    sr = lax.broadcasted_iota(I32, ((CONV_W - 1) * Q, Q), 0)
    sc_ = lax.broadcasted_iota(I32, ((CONV_W - 1) * Q, Q), 1)
    shift = jnp.where(sc_ == (sr & (Q - 1)) + (sr // Q) - (CONV_W - 1), 1.0, 0.0).astype(BF16)

    def conv(ext_ref, cur_ref, w_ref, bias_ref, out_ref, new_off):
        for lo in range(0, cur_ref.shape[1], CH):
            cs = slice(lo, lo + CH)
            cur_bf = cur_ref[:, cs]
            cur = cur_bf.astype(F32)
            shifted = _dot(shift, cur_bf)
            acc = bias_ref[:, cs] + w_ref[CONV_W - 1:CONV_W, cs] * cur
            for j in range(CONV_W - 1):
                acc = acc + w_ref[j:j + 1, cs] * shifted[j * Q:(j + 1) * Q]
            ext_ref[CONV_PAD:2 * CONV_PAD, cs] = cur[0:CONV_PAD]
            head = bias_ref[:, cs] + w_ref[CONV_W - 1:CONV_W, cs] * cur[0:CONV_PAD]
            for j in range(CONV_W - 1):
                off = CONV_PAD - (CONV_W - 1) + j
                head = head + w_ref[j:j + 1, cs] * ext_ref[off:off + CONV_PAD, cs]
            out_ref[:, cs] = _silu(jnp.concatenate([head, acc[CONV_PAD:]], axis=0)).astype(out_ref.dtype)
            tail = cur[Q - CONV_PAD:Q]
            cnew_ref[0, :, new_off + lo:new_off + lo + CH] = tail
            ext_ref[0:CONV_PAD, cs] = tail

    conv(extx_ref, x_ref, cwx_ref, cbx_ref, xs_ref, 0)
    conv(extb_ref, b_ref, cwb_ref, cbb_ref, bm_ref, HP)
    conv(extc_ref, c_ref, cwc_ref, cbc_ref, cm_ref, HP + GN)

    dt = _softplus(dt_ref[...].astype(F32) + dtb_ref[...])
    da = dt * a_ref[...]
    rr = lax.broadcasted_iota(I32, (Q, Q), 0)
    cc = lax.broadcasted_iota(I32, (Q, Q), 1)
    tri = jnp.where(rr >= cc, 1.0, 0.0).astype(BF16)
    cs = _dot(tri, jnp.concatenate(_split_bf16(da, 3), axis=1))
    acs = cs[:, 0:LANES] + cs[:, LANES:2 * LANES] + cs[:, 2 * LANES:3 * LANES]

    both = _split_bf16(jnp.concatenate([acs, dt], axis=0), 2)
    col_ref[...] = _dot(both[0], expand_ref[...]) + _dot(both[1], expand_ref[...])

    pr = lax.broadcasted_iota(I32, (LANES // 2, 2 * LANES), 0)
    pk = lax.broadcasted_iota(I32, (LANES // 2, 2 * LANES), 1)
    esel = jnp.where(pk == jnp.where(pk < LANES, 2 * pr, 2 * pr + 1 + LANES), 1.0, 0.0).astype(BF16)
    zero = jnp.zeros((Q, LANES), BF16)

    def pair_rows(v, parts):
        out = None
        for piece in _split_bf16(v, parts):
            vbd = jnp.concatenate([jnp.concatenate([piece, zero], axis=1),
                                   jnp.concatenate([zero, piece], axis=1)], axis=0)
            r = _nt_dot(esel, vbd)
            out = r if out is None else out + r
        return out

    rowa_ref[...] = pair_rows(acs, 3)
    rowd_ref[...] = pair_rows(dt, 2)

    lane = lax.broadcasted_iota(I32, (Q, LANES), 1)
    trow = lax.broadcasted_iota(I32, (Q, LANES), 0)
    causal = trow >= (lane & (Q - 1))
    lo_half = lane < Q
    er = lax.broadcasted_iota(I32, (N, N), 0)
    ec = lax.broadcasted_iota(I32, (N, N), 1)
    eye = jnp.where(er == ec, 1.0, 0.0).astype(BF16)

    for g in range(G):
        gs = slice(g * GW, (g + 1) * GW)
        bg = bm_ref[:, g * N:(g + 1) * N]
        cg = cm_ref[:, g * N:(g + 1) * N]
        cb_ref[g] = _nt_dot(cg, jnp.concatenate([bg, bg], axis=0))
        ybuf_ref[:, gs] = _dot(cg, st_ref[:, gs].astype(BF16))
        bt_ref[g] = _nt_dot(eye, bg).astype(BF16)
    for p in range(G * PAIRS_PER_GROUP):
        ps = slice(p * LANES, (p + 1) * LANES)
        seg = col_ref[0:Q, ps] - rowa_ref[p:p + 1, :]
        decay = jnp.exp(jnp.where(causal, seg, -jnp.inf))
        lm_ref[p] = (cb_ref[p // PAIRS_PER_GROUP] * decay * rowd_ref[p:p + 1, :]).astype(BF16)
    for p in range(G * PAIRS_PER_GROUP):
        ps = slice(p * LANES, (p + 1) * LANES)
        xp = xs_ref[:, ps]
        xbd = jnp.concatenate([jnp.where(lo_half, xp, 0.0), jnp.where(lo_half, 0.0, xp)],
                              axis=0).astype(BF16)
        ybuf_ref[:, ps] = (_dot(lm_ref[p], xbd) + ybuf_ref[:, ps] * jnp.exp(col_ref[0:Q, ps])
                           + dsk_ref[:, ps] * xp)
    for g in range(G):
        gs = slice(g * GW, (g + 1) * GW)
        last = col_ref[Q - 1:Q, gs]
        w_end = col_ref[Q:2 * Q, gs] * jnp.exp(last - col_ref[0:Q, gs])
        xw = (xs_ref[:, gs] * w_end).astype(BF16)
        st_ref[:, gs] = st_ref[:, gs] * jnp.exp(last) + _dot(bt_ref[g], xw)

    sq = jnp.zeros((Q, LANES), F32)
    for lo in range(0, HP, CH):
        cs = slice(lo, lo + CH)
        yv = ybuf_ref[:, cs] * _silu(z_ref[:, cs].astype(F32))
        ybuf_ref[:, cs] = yv
        y2 = yv * yv
        for c in range(CH // LANES):
            sq = sq + y2[:, c * LANES:(c + 1) * LANES]
    inv = lax.rsqrt(jnp.sum(sq, axis=-1, keepdims=True) * (1.0 / HP) + EPS)
    for lo in range(0, HP, CH):
        cs = slice(lo, lo + CH)
        y_ref[:, cs] = (ybuf_ref[:, cs] * inv * ng_ref[:, cs]).astype(BF16)

    @pl.when(last_ref[k] == 1)
    def _():
        snew_ref[0] = st_ref[...]


def _ssd(tabs, zx, cprev, sprev, conv_w, conv_b, dt_bias, a_neg, d_skip, norm_g, expand):
    T = zx.shape[0]
    S, N, HP = sprev.shape
    GN = SSM_GROUPS * N
    Q = ROW_BLOCK
    nblk = T // Q
    bb, cb = HP // GN, HP // GN + 1
    zb = (2 * HP) // GN

    def rows(width, col_blk):
        return pl.BlockSpec((Q, width), lambda k, s, f, l: (k, col_blk))

    def per_seq(shape, col_blk):
        return pl.BlockSpec((1,) + shape, lambda k, s, f, l: (s[k], 0, col_blk))

    def const(shape, col_blk=0):
        return pl.BlockSpec(shape, lambda k, s, f, l: (0, col_blk))

    in_specs = [
        rows(HP, 0), rows(HP, 1), rows(GN, zb), rows(GN, zb + 1), rows(LANES, (2 * HP + 2 * GN) // LANES),
        per_seq((CONV_PAD, HP), 0), per_seq((CONV_PAD, GN), bb), per_seq((CONV_PAD, GN), cb),
        per_seq((N, HP), 0),
        const((CONV_W, HP)), const((CONV_W, GN), bb), const((CONV_W, GN), cb),
        const((1, HP)), const((1, GN), bb), const((1, GN), cb),
        const((1, LANES)), const((1, LANES)), const((1, HP)), const((1, HP)), const((LANES, HP)),
    ]
    out_shape = [jax.ShapeDtypeStruct((T, HP), BF16),
                 jax.ShapeDtypeStruct((S, CONV_PAD, HP + 2 * GN), F32),
                 jax.ShapeDtypeStruct((S, N, HP), F32)]
    out_specs = [pl.BlockSpec((Q, HP), lambda k, s, f, l: (k, 0)),
                 pl.BlockSpec((1, CONV_PAD, HP + 2 * GN), lambda k, s, f, l: (s[k], 0, 0)),
                 pl.BlockSpec((1, N, HP), lambda k, s, f, l: (s[k], 0, 0))]
    scratch = [pltpu.VMEM((2 * CONV_PAD, HP), F32), pltpu.VMEM((2 * CONV_PAD, GN), F32),
               pltpu.VMEM((2 * CONV_PAD, GN), F32), pltpu.VMEM((N, HP), F32), pltpu.VMEM((Q, HP), F32),
               pltpu.VMEM((Q, GN), BF16), pltpu.VMEM((Q, GN), BF16), pltpu.VMEM((2 * Q, HP), F32),
               pltpu.VMEM((LANES // 2, 2 * Q), F32), pltpu.VMEM((LANES // 2, 2 * Q), F32),
               pltpu.VMEM((Q, HP), F32), pltpu.VMEM((SSM_GROUPS, Q, 2 * Q), F32),
               pltpu.VMEM((SSM_GROUPS, N, Q), BF16), pltpu.VMEM((HP // LANES, Q, 2 * Q), BF16)]
    return pl.pallas_call(
        _ssd_kernel,
        out_shape=out_shape,
        grid_spec=pltpu.PrefetchScalarGridSpec(num_scalar_prefetch=3, grid=(nblk,), in_specs=in_specs,
                                               out_specs=out_specs, scratch_shapes=scratch),
        compiler_params=pltpu.CompilerParams(dimension_semantics=("arbitrary",), vmem_limit_bytes=48 * MIB),
        name="ssd_mixer",
    )(tabs["seq"], tabs["first"], tabs["last"],
      zx, zx, zx, zx, zx, cprev, cprev, cprev, sprev,
      conv_w, conv_w, conv_w, conv_b, conv_b, conv_b, dt_bias, a_neg, d_skip, norm_g, expand)


def _attn_kernel(wk0_ref, wk1_ref, wc0_ref, wc1_ref, uc0_ref, uc1_ref, ninv_ref,
                 q_ref, k0_ref, k1_ref, k2_ref, c0_ref, c1_ref, bias_ref, sink_ref, o_ref, sc_ref, p_ref):
    Q = ROW_BLOCK
    DH = ATT_HEAD_DIM
    KVW = k2_ref.shape[1] // 2
    n_kv = KVW // DH
    NK = bias_ref.shape[2]
    k = pl.program_id(0)
    s = lax.broadcasted_iota(I32, (Q, NK), 1)
    mask_bias = jnp.where(s >= ninv_ref[k] * Q, 0.0, -jnp.inf)
    kv = jnp.concatenate([jnp.where(uc0_ref[k] == 1, c0_ref[...], k0_ref[...]),
                          jnp.where(uc1_ref[k] == 1, c1_ref[...], k1_ref[...]), k2_ref[...]], axis=0).astype(BF16)

    for g in range(n_kv):
        qg = jnp.concatenate([q_ref[:, (g * ATT_GROUP + r) * DH:(g * ATT_GROUP + r + 1) * DH]
                              for r in range(ATT_GROUP)], axis=0)
        sc_ref[g] = _nt_dot(qg, kv[:, g * DH:(g + 1) * DH])
    sink_terms = []
    for g in range(n_kv):
        for r in range(ATT_GROUP):
            rows = slice(r * Q, (r + 1) * Q)
            sc = sc_ref[g, rows, :] * (DH ** -0.5) + bias_ref[g, rows, :] + mask_bias
            sink = sink_ref[g, rows, 0:1]
            m = jnp.maximum(jnp.max(sc, axis=-1, keepdims=True), sink)
            p_ref[g, rows, :] = jnp.exp(sc - m).astype(BF16)
            sink_terms.append(jnp.exp(sink - m))
    first_half = lax.broadcasted_iota(I32, (Q, 2 * DH), 1) < DH
    ones = jnp.ones((NK, 2 * DH), BF16)
    for g in range(n_kv):
        vg = kv[:, KVW + g * DH:KVW + (g + 1) * DH]
        og = _dot(p_ref[g], jnp.concatenate([vg, vg], axis=1))
        dn = _dot(p_ref[g], ones)
        for rp in range(ATT_GROUP // 2):
            ev, od = slice(2 * rp * Q, (2 * rp + 1) * Q), slice((2 * rp + 1) * Q, (2 * rp + 2) * Q)
            s_ev, s_od = sink_terms[g * ATT_GROUP + 2 * rp], sink_terms[g * ATT_GROUP + 2 * rp + 1]
            pair = jnp.where(first_half, og[ev] * (1.0 / (dn[ev] + s_ev)), og[od] * (1.0 / (dn[od] + s_od)))
            h0 = g * ATT_GROUP + 2 * rp
            o_ref[:, h0 * DH:(h0 + 2) * DH] = pair.astype(BF16)


def _attention(tabs, q, kv, cache_kv, sinks):
    T, HD = q.shape
    Q = ROW_BLOCK
    NK = (WINDOW_BLOCKS + 1) * Q
    n_heads = HD // ATT_HEAD_DIM
    KV2 = kv.shape[1]
    n_kv = KV2 // (2 * ATT_HEAD_DIM)
    GQ = ATT_GROUP * Q
    slopes = 2.0 ** (-8.0 * np.arange(1, n_heads + 1) / n_heads)
    dist = np.abs(np.arange(Q)[:, None] + WINDOW_BLOCKS * Q - np.arange(NK)[None, :])
    bias = jnp.asarray((-slopes[:, None, None] * dist[None]).reshape(n_kv, GQ, NK).astype(np.float32))
    sink_rows = jnp.broadcast_to(jnp.repeat(sinks.reshape(n_kv, ATT_GROUP), Q, axis=1)[:, :, None],
                                 (n_kv, GQ, LANES))

    def blk(pick):
        return pl.BlockSpec((Q, KV2), lambda k, wk0, wk1, wc0, wc1, uc0, uc1, n: (pick(k, wk0, wk1, wc0, wc1), 0))

    def const(shape):
        return pl.BlockSpec(shape, lambda k, *_: (0, 0, 0))

    return pl.pallas_call(
        _attn_kernel,
        out_shape=jax.ShapeDtypeStruct((T, HD), BF16),
        grid_spec=pltpu.PrefetchScalarGridSpec(
            num_scalar_prefetch=7, grid=(T // Q,),
            in_specs=[pl.BlockSpec((Q, HD), lambda k, *_: (k, 0)),
                      blk(lambda k, wk0, wk1, wc0, wc1: wk0[k]), blk(lambda k, wk0, wk1, wc0, wc1: wk1[k]),
                      blk(lambda k, wk0, wk1, wc0, wc1: k),
                      blk(lambda k, wk0, wk1, wc0, wc1: wc0[k]), blk(lambda k, wk0, wk1, wc0, wc1: wc1[k]),
                      const((n_kv, GQ, NK)), const((n_kv, GQ, LANES))],
            out_specs=pl.BlockSpec((Q, HD), lambda k, *_: (k, 0)),
            scratch_shapes=[pltpu.VMEM((n_kv, GQ, NK), F32), pltpu.VMEM((n_kv, GQ, NK), BF16)]),
        compiler_params=pltpu.CompilerParams(dimension_semantics=("parallel",), vmem_limit_bytes=32 * MIB),
        name="swa_attention",
    )(tabs["wk0"], tabs["wk1"], tabs["wc0"], tabs["wc1"], tabs["uc0"], tabs["uc1"], tabs["ninv"],
      q, kv, kv, kv, cache_kv, cache_kv, bias, sink_rows)


def _router_kernel(b2c_ref, h_ref, shift_ref, scale_ref, g_ref, wh_ref, wl_ref, rb_ref, hn_ref, wrow_ref, meta_ref,
                   xh_ref, xl_ref, wt_ref, *, nsb):
    i = pl.program_id(0)

    def body(s, carry):
        c = b2c_ref[i * nsb + s]
        rows = _row_block(s)
        xn = _rms_mod(h_ref[rows, :], g_ref[...], shift_ref[c], scale_ref[c])
        hn_ref[rows, 0:h_ref.shape[1] // 2] = _pack_bf16_pairs(xn)
        hi, lo = _split_bf16(xn, 2)
        xh_ref[rows, :] = hi
        xl_ref[rows, :] = lo
        return carry
    lax.fori_loop(0, nsb, body, 0)

    lt = (_nt_dot(wh_ref[...], xh_ref[...]) + _nt_dot(wh_ref[...], xl_ref[...])
          + _nt_dot(wl_ref[...], xh_ref[...]) + rb_ref[...])

    def first_max(vals):
        m = vals[0]
        for v in vals[1:]:
            m = jnp.maximum(m, v)
        idx = jnp.full(m.shape, len(vals) - 1, I32)
        for j in range(len(vals) - 2, -1, -1):
            idx = jnp.where(vals[j] == m, j, idx)
        return m, idx

    lg = [lt[j:j + 1, :] for j in range(MOE_GROUPS)]
    mg, gi = first_max(lg)
    p_sel = 1.0 / sum(jnp.exp(v - mg) for v in lg)
    le = []
    for j in range(MOE_PER_GROUP):
        v = lt[MOE_GROUPS + (MOE_GROUPS - 1) * MOE_PER_GROUP + j:MOE_GROUPS + (MOE_GROUPS - 1) * MOE_PER_GROUP + j + 1, :]
        for grp in range(MOE_GROUPS - 2, -1, -1):
            row = MOE_GROUPS + grp * MOE_PER_GROUP + j
            v = jnp.where(gi == grp, lt[row:row + 1, :], v)
        le.append(v)
    m1, i1 = first_max(le)
    m2, i2 = first_max([jnp.where(i1 == j, -jnp.inf, le[j]) for j in range(MOE_PER_GROUP)])
    e2 = jnp.exp(m2 - m1)
    w1 = p_sel * (1.0 / (1.0 + e2))
    w2 = p_sel * (e2 / (1.0 + e2))
    lo_i = jnp.minimum(i1, i2)
    hi_i = jnp.maximum(i1, i2)
    first_is_lo = i1 < i2
    pair_base = jnp.where(lo_i == 0, 0, jnp.where(lo_i == 1, 3, 5))
    bucket = gi * MOE_PAIRS + pair_base + hi_i - lo_i - 1
    meta_ref[0:1, :] = bucket.astype(F32)
    meta_ref[1:8, :] = jnp.zeros((7, meta_ref.shape[1]), F32)
    wt_ref[...] = jnp.zeros(wt_ref.shape, F32)
    wt_ref[0:1, :] = jnp.where(first_is_lo, w1, w2)
    wt_ref[1:2, :] = jnp.where(first_is_lo, w2, w1)
    wrow_ref[...] = wt_ref[...].T


def _router(b2c, h, g, shift, scale, w_hi, w_lo, rbias, *, tm_want):
    T, D = h.shape
    C = shift.shape[0]
    tm = _pick_tile(T, tm_want)
    nsb = tm // ROW_BLOCK
    row_spec = pl.BlockSpec((tm, D), lambda i, b: (i, 0))
    mod_spec = pl.BlockSpec((C, 1, D), lambda i, b: (0, 0, 0))
    rb = jnp.broadcast_to(rbias.reshape(LANES, 1), (LANES, tm))
    return pl.pallas_call(
        functools.partial(_router_kernel, nsb=nsb),
        out_shape=[jax.ShapeDtypeStruct((T, D // 2), I32), jax.ShapeDtypeStruct((T, LANES), F32),
                   jax.ShapeDtypeStruct((8, T), F32)],
        grid_spec=pltpu.PrefetchScalarGridSpec(
            num_scalar_prefetch=1, grid=(T // tm,),
            in_specs=[row_spec, mod_spec, mod_spec, pl.BlockSpec((1, D), lambda i, b: (0, 0)),
                      pl.BlockSpec((LANES, D), lambda i, b: (0, 0)), pl.BlockSpec((LANES, D), lambda i, b: (0, 0)),
                      pl.BlockSpec((LANES, tm), lambda i, b: (0, 0))],
            out_specs=[pl.BlockSpec((tm, D // 2), lambda i, b: (i, 0)), pl.BlockSpec((tm, LANES), lambda i, b: (i, 0)),
                       pl.BlockSpec((8, tm), lambda i, b: (0, i))],
            scratch_shapes=[pltpu.VMEM((tm, D), BF16), pltpu.VMEM((tm, D), BF16), pltpu.VMEM((LANES, tm), F32)]),
        compiler_params=pltpu.CompilerParams(dimension_semantics=("parallel",),
                                             vmem_limit_bytes=int(5 * tm * D * 4 + 12 * MIB)),
        name="moe_router",
    )(b2c, h, shift, scale, g.reshape(1, D), w_hi, w_lo, rb)


def _gather_rows(src, idx):
    n = idx.shape[0]
    D = src.shape[1]
    n_workers = SC_CORES * SC_SUBCORES
    per_worker = n // n_workers
    limit = min(SC_MAX_INDICES, SC_CHUNK_BYTES // (D * src.dtype.itemsize))
    rows = max(r for r in range(8, limit + 1, 8) if per_worker % r == 0)
    n_chunks = per_worker // rows
    assert n == n_workers * n_chunks * rows, (n, n_workers, rows)
    mesh = plsc.VectorSubcoreMesh(core_axis_name="c", subcore_axis_name="s", num_cores=SC_CORES,
                                  num_subcores=SC_SUBCORES)

    @functools.partial(
        pl.kernel, mesh=mesh, out_type=jax.ShapeDtypeStruct((n, D), src.dtype),
        scratch_types=[pltpu.VMEM((rows,), I32), pltpu.VMEM((rows, D), src.dtype), pltpu.SemaphoreType.DMA],
        name="sc_row_gather")
    def gather(src_hbm, idx_hbm, out_hbm, idx_v, rows_v, sem):
        base = (lax.axis_index("s") * SC_CORES + lax.axis_index("c")) * per_worker

        @pl.loop(0, n_chunks)
        def _(j):
            off = base + j * rows
            pltpu.sync_copy(idx_hbm.at[pl.ds(off, rows)], idx_v)
            pltpu.async_copy(src_hbm.at[idx_v], rows_v, sem).wait()
            pltpu.sync_copy(rows_v, out_hbm.at[pl.ds(off, rows)])

    return gather(src, idx)


def _ffn_kernel(ea_ref, eb_ref, nv_ref, x_ref, ws_ref, wga_ref, wua_ref, wda_ref, wgb_ref, wub_ref, wdb_ref, y_ref):
    i = pl.program_id(0)
    DW = y_ref.shape[1]

    @pl.when(nv_ref[i] > 0)
    def _():
        x = _unpack_bf16_pairs(x_ref[...]).astype(BF16)

        def expert(wg_ref, wu_ref, wd_ref):
            hid = _silu(_dot(x, wg_ref[0])) * _dot(x, wu_ref[0])
            return _dot(hid.astype(BF16), wd_ref[0])
        wa = ws_ref[:, 0:1]
        wb = ws_ref[:, 1:2]
        y_ref[...] = _pack_bf16_pairs(wa * expert(wga_ref, wua_ref, wda_ref) + wb * expert(wgb_ref, wub_ref, wdb_ref))

    @pl.when(nv_ref[i] == 0)
    def _():
        y_ref[...] = jnp.zeros(y_ref.shape, I32)


def _moe_ffn(plan, xs, ws, w_g, w_u, w_d):
    R, DX = xs.shape
    tm = MOE_TILE
    E, F, D = w_d.shape
    up_a = pl.BlockSpec((1, D, F), lambda i, a, b, n: (a[i], 0, 0))
    up_b = pl.BlockSpec((1, D, F), lambda i, a, b, n: (b[i], 0, 0))
    return pl.pallas_call(
        _ffn_kernel,
        out_shape=jax.ShapeDtypeStruct((R, D // 2), I32),
        grid_spec=pltpu.PrefetchScalarGridSpec(
            num_scalar_prefetch=3, grid=(R // tm,),
            in_specs=[pl.BlockSpec((tm, DX), lambda i, a, b, n: (i, 0)),
                      pl.BlockSpec((tm, LANES), lambda i, a, b, n: (i, 0)),
                      up_a, up_a, pl.BlockSpec((1, F, D), lambda i, a, b, n: (a[i], 0, 0)),
                      up_b, up_b, pl.BlockSpec((1, F, D), lambda i, a, b, n: (b[i], 0, 0))],
            out_specs=pl.BlockSpec((tm, D // 2), lambda i, a, b, n: (i, 0))),
        compiler_params=pltpu.CompilerParams(dimension_semantics=("arbitrary",), vmem_limit_bytes=48 * MIB),
        name="moe_experts",
    )(plan["ea"], plan["eb"], plan["nvalid"], xs, ws, w_g, w_u, w_d, w_g, w_u, w_d)


def _moe_plan(meta, T, expert_off):
    tm = MOE_TILE
    n_tiles = -(-T // tm) + MOE_BUCKETS
    bucket = meta[0].astype(I32)
    onehot = (bucket[:, None] == jnp.arange(MOE_BUCKETS, dtype=I32)[None, :]).astype(F32)
    blocks = onehot.reshape(T // LANES, LANES, MOE_BUCKETS)
    within = jnp.einsum("ij,bjk->bik", jnp.tril(jnp.ones((LANES, LANES), F32)), blocks)
    totals = within[:, -1, :]
    n_blk = T // LANES
    before = jnp.einsum("ij,jk->ik", jnp.tril(jnp.ones((n_blk, n_blk), F32), -1), totals,
                        precision=lax.Precision.HIGHEST)
    cum = (within + before[:, None, :]).reshape(T, MOE_BUCKETS)
    rank = jnp.sum(cum * onehot, axis=1).astype(I32) - 1
    counts = jnp.sum(totals, axis=0).astype(I32)
    ntile_b = (counts + tm - 1) // tm
    tend_b = jnp.cumsum(ntile_b)
    tstart_b = tend_b - ntile_b
    dest = tstart_b[bucket] * tm + rank
    tok_of_pos = jnp.zeros((n_tiles * tm,), I32).at[dest].set(jnp.arange(T, dtype=I32))
    tiles = jnp.arange(n_tiles, dtype=I32)
    tile_b = jnp.minimum(jnp.sum((tiles[:, None] >= tend_b[None, :]).astype(I32), axis=1), MOE_BUCKETS - 1)
    nvalid = jnp.where(tiles < tend_b[-1],
                       jnp.clip(counts[tile_b] - (tiles - tstart_b[tile_b]) * tm, 0, tm), 0).astype(I32)
    grp = tile_b // MOE_PAIRS
    pair = tile_b % MOE_PAIRS
    ea = expert_off + grp * MOE_PER_GROUP + jnp.asarray(MOE_PAIR_LO, I32)[pair]
    eb = expert_off + grp * MOE_PER_GROUP + jnp.asarray(MOE_PAIR_HI, I32)[pair]
    return dict(dest=dest, tok_of_pos=tok_of_pos, nvalid=nvalid, ea=ea.astype(I32), eb=eb.astype(I32))


def _hmoe(b2c, h, g, shift, scale, router, w_g, w_u, w_d, expert_off):
    T = h.shape[0]
    hn, wrows, meta = _router(b2c, h, g, shift, scale, *router, tm_want=512)
    plan = _moe_plan(meta, T, expert_off)
    xs = _gather_rows(hn, plan["tok_of_pos"])
    ws = _gather_rows(wrows, plan["tok_of_pos"])
    ys = _moe_ffn(plan, xs, ws, w_g, w_u, w_d)
    return _gather_rows(ys, plan["dest"])


def _tables(Bp, Lp, Bs, Ls):
    nbp, nbs = Lp // ROW_BLOCK, Ls // ROW_BLOCK
    cols = dict(seq=[], first=[], last=[], ninv=[], wk0=[], wk1=[], wc0=[], wc1=[], uc0=[], uc1=[])
    k = 0
    for b in range(Bp + Bs):
        nb = nbp if b < Bp else nbs
        for c in range(nb):
            cols["seq"].append(b)
            cols["first"].append(int(c == 0))
            cols["last"].append(int(c == nb - 1))
            cols["ninv"].append(max(0, WINDOW_BLOCKS - c) if b < Bp else 0)
            for j in range(WINDOW_BLOCKS):
                hist = c - WINDOW_BLOCKS + j
                from_cache = b >= Bp and hist < 0
                cols[f"uc{j}"].append(int(from_cache))
                cols[f"wc{j}"].append((b - Bp) * WINDOW_BLOCKS + WINDOW_BLOCKS + hist if from_cache else 0)
                cols[f"wk{j}"].append(k if from_cache else max(k - WINDOW_BLOCKS + j, 0))
            k += 1
    return {name: jnp.asarray(np.asarray(v, np.int32)) for name, v in cols.items()}


def _router_weights(router_g, bias_g, router_e, bias_e):
    D = router_g.shape[0]
    w = jnp.zeros((LANES, D), F32)
    w = w.at[:MOE_GROUPS].set(router_g.T).at[MOE_GROUPS:MOE_GROUPS + router_e.shape[1]].set(router_e.T)
    w_hi = w.astype(BF16)
    w_lo = (w - w_hi.astype(F32)).astype(BF16)
    rb = jnp.zeros((LANES,), F32).at[:MOE_GROUPS].set(bias_g).at[MOE_GROUPS:MOE_GROUPS + bias_e.shape[0]].set(bias_e)
    return w_hi, w_lo, rb


def kernel(x_prompt, x_sample, state_conv, state_ssm, cache_k, cache_v, c_prompt, c_sample, ada_w, ada_b, norm_mix, norm_ffn, norm_kv, norm_out, ssm_w_in, ssm_conv_w, ssm_conv_b, ssm_dt_bias, ssm_a_log, ssm_d, ssm_norm, ssm_w_out, attn_w_kv, attn_w_q, attn_sinks, attn_w_o, moe_router_g, moe_bias_g, moe_router_e, moe_bias_e, moe_w_gate, moe_w_up, moe_w_down):
    Bp, Lp, D = x_prompt.shape
    Bs, Ls, _ = x_sample.shape
    Tp, Ts = Bp * Lp, Bs * Ls
    T = Tp + Ts
    C = Bp + Bs
    tabs = _tables(Bp, Lp, Bs, Ls)
    b2c = tabs["seq"]

    n_mod = ada_w.shape[1] // D
    mods = _mods(jnp.concatenate([c_prompt, c_sample], axis=0), ada_w, ada_b).reshape(C, n_mod, D)
    mod = lambda i: mods[:, i:i + 1, :]
    x = jnp.concatenate([x_prompt.reshape(Tp, D), x_sample.reshape(Ts, D)], axis=0)

    HP = ssm_w_out.shape[1]
    H = HP // SSM_HEAD_DIM
    N = SSM_D_STATE
    GN = SSM_GROUPS * N
    conv_dim = HP + 2 * GN
    in_cols = 2 * HP + 2 * GN + LANES
    w_in = _to_bf16(ssm_w_in[0], tn=in_cols // 9, out_cols=in_cols)
    zx = _norm_matmul(b2c, x, norm_mix[0], mod(0), mod(1), w_in, out_dtype=BF16, tn=in_cols // 9, tm_want=1024)
    cprev = jnp.zeros((C, CONV_PAD, conv_dim), F32).at[Bp:, CONV_PAD - (CONV_W - 1):].set(state_conv[0])
    sprev = jnp.concatenate([jnp.zeros((Bp, N, HP), F32),
                             jnp.transpose(state_ssm[0], (0, 3, 1, 2)).reshape(Bs, N, HP)], axis=0)
    pad_h = lambda v: jnp.pad(v.astype(F32), (0, LANES - H)).reshape(1, LANES)
    expand = (jnp.arange(LANES)[:, None] == (jnp.arange(HP) // SSM_HEAD_DIM)[None, :]).astype(BF16)
    y_ssd, cnew, snew = _ssd(
        tabs, zx, cprev, sprev, ssm_conv_w[0], ssm_conv_b[0].reshape(1, conv_dim), pad_h(ssm_dt_bias[0]),
        pad_h(-jnp.exp(ssm_a_log[0].astype(F32))), jnp.repeat(ssm_d[0].astype(F32), SSM_HEAD_DIM).reshape(1, HP),
        ssm_norm[0].reshape(1, HP), expand)
    h = _matmul_residual(b2c, y_ssd, _to_bf16(ssm_w_out[0]), x, mod(2), tn=512, tm_want=1024)

    n_exp, _, moe_ff = moe_w_gate.shape[1:]
    w_gate = _to_bf16(moe_w_gate).reshape(-1, D, moe_ff)
    w_up = _to_bf16(moe_w_up).reshape(-1, D, moe_ff)
    w_down = _to_bf16(moe_w_down).reshape(-1, moe_ff, D)

    def experts(layer):
        return w_gate, w_up, w_down, layer * n_exp

    def router(layer):
        return _router_weights(moe_router_g[layer], moe_bias_g[layer], moe_router_e[layer], moe_bias_e[layer])

    moe0 = _hmoe(b2c, h, norm_ffn[0], mod(3), mod(4), router(0), *experts(0))

    kv, h = _norm_matmul(b2c, h, norm_kv, mod(12), mod(13), _to_bf16(attn_w_kv), out_dtype=F32,
                         tn=attn_w_kv.shape[1], tm_want=512, res=moe0, gate=mod(5))
    KV2 = kv.shape[1]
    KVW = KV2 // 2
    W = WINDOW_BLOCKS * ROW_BLOCK
    cache_kv = jnp.concatenate([cache_k.reshape(Bs, W, KVW), cache_v.reshape(Bs, W, KVW)], axis=-1)
    kvp = kv[:Tp].reshape(Bp, Lp, KV2)
    kvs = jnp.concatenate([cache_kv, kv[Tp:].reshape(Bs, Ls, KV2)], axis=1)
    q = _norm_matmul(b2c, h, norm_mix[1], mod(6), mod(7), _to_bf16(attn_w_q[0]), out_dtype=BF16,
                     tn=1024, tm_want=1024)
    o = _attention(tabs, q, kv, cache_kv.reshape(Bs * W, KV2), attn_sinks[0].astype(F32))
    h = _matmul_residual(b2c, o, _to_bf16(attn_w_o[0]), h, mod(8), tn=1024, tm_want=1024)
    moe1 = _hmoe(b2c, h, norm_ffn[1], mod(9), mod(10), router(1), *experts(1))

    fin = functools.partial(_final_norm, b2c, h, moe1, mod(11), mod(14), mod(15), norm_out, tm_want=512)
    y_prompt = fin(row_off=0, n_rows=Tp).reshape(Bp, Lp, D)
    y_sample = fin(row_off=Tp, n_rows=Ts).reshape(Bs, Ls, D)

    kv_heads = KVW // ATT_HEAD_DIM
    tail = lambda a, lo: a[:, -W:, lo:lo + KVW].reshape(a.shape[0], W, kv_heads, ATT_HEAD_DIM)
    conv_tail = cnew[:, CONV_PAD - (CONV_W - 1):]
    ssm_new = jnp.transpose(snew.reshape(C, N, H, SSM_HEAD_DIM), (0, 2, 3, 1))
    return (y_prompt, y_sample, conv_tail[None, :Bp], ssm_new[None, :Bp], tail(kvp, 0), tail(kvp, KVW),
            conv_tail[None, Bp:], ssm_new[None, Bp:], tail(kvs, 0), tail(kvs, KVW))
```

```python
import functools

import numpy as np
import jax
import jax.numpy as jnp
from jax import lax
from jax.experimental import pallas as pl
from jax.experimental.pallas import tpu as pltpu
from jax.experimental.pallas import tpu_sc as plsc

F32 = jnp.float32
BF16 = jnp.bfloat16
I32 = jnp.int32
EPS = 1e-6
ROW_BLOCK = 64
WINDOW_BLOCKS = 2
LANES = 128
MIB = 1024 * 1024

SSM_HEAD_DIM = 64
SSM_GROUPS = 8
SSM_D_STATE = 128
CONV_W = 4
CONV_PAD = 8
ATT_HEAD_DIM = 64
ATT_GROUP = 8
MOE_GROUPS = 4
MOE_PER_GROUP = 4
MOE_PAIR_LO = (0, 0, 0, 1, 1, 2)
MOE_PAIR_HI = (1, 2, 3, 2, 3, 3)
MOE_PAIRS = len(MOE_PAIR_LO)
MOE_BUCKETS = MOE_GROUPS * MOE_PAIRS
MOE_TILE = 256
SC_CORES = 2
SC_SUBCORES = 16
SC_MAX_INDICES = 128
SC_CHUNK_BYTES = 80 * 1024


def _pick_tile(n, want):
    t = min(want, n)
    t -= t % ROW_BLOCK
    while n % t:
        t -= ROW_BLOCK
    return t


def _silu(x):
    return (0.5 * x) * (1.0 + jnp.tanh(0.5 * x))


def _softplus(x):
    return jnp.maximum(x, 0.0) + jnp.log1p(jnp.exp(-jnp.abs(x)))


def _rms_mod(hv, g, shift, scale):
    ms = jnp.mean(hv * hv, axis=-1, keepdims=True)
    return (hv * lax.rsqrt(ms + EPS) * g) * (1.0 + scale) + shift


def _dot(a, b):
    return jnp.dot(a, b, preferred_element_type=F32)


def _nt_dot(a, b):
    return lax.dot_general(a, b, (((1,), (1,)), ((), ())), preferred_element_type=F32)


def _split_bf16(x, parts):
    out = []
    r = x
    for _ in range(parts):
        p = r.astype(BF16)
        out.append(p)
        r = r - p.astype(F32)
    return out


def _pack_bf16_pairs(x):
    c = x.shape[1] // 2
    xb = x.astype(BF16).astype(F32)
    hi = lax.bitcast_convert_type(xb[:, :c], I32)
    lo = lax.bitcast_convert_type(xb[:, c:], I32)
    return hi | lax.shift_right_logical(lo, jnp.int32(16))


def _unpack_bf16_pairs(w):
    hi = lax.bitcast_convert_type(w & jnp.int32(-65536), F32)
    lo = lax.bitcast_convert_type(lax.shift_left(w, jnp.int32(16)), F32)
    return jnp.concatenate([hi, lo], axis=1)


def _row_block(s):
    return pl.ds(pl.multiple_of(s * ROW_BLOCK, ROW_BLOCK), ROW_BLOCK)


def _cast_kernel(x_ref, o_ref, *, valid_cols):
    x = x_ref[...]
    if valid_cols is not None:
        col = pl.program_id(1) * x.shape[1] + lax.broadcasted_iota(I32, x.shape, 1)
        x = jnp.where(col < valid_cols, x, 0.0)
    o_ref[...] = x.astype(BF16)


def _to_bf16(w, *, tn=None, out_cols=None):
    w = w.reshape(-1, w.shape[-1])
    R, cols = w.shape
    out_cols = out_cols or cols
    tn = tn or out_cols
    tm = _pick_tile(R, max(ROW_BLOCK, (8 * MIB) // (tn * 4)))
    return pl.pallas_call(
        functools.partial(_cast_kernel, valid_cols=cols if out_cols != cols else None),
        out_shape=jax.ShapeDtypeStruct((R, out_cols), BF16),
        grid=(R // tm, out_cols // tn),
        in_specs=[pl.BlockSpec((tm, tn), lambda i, j: (i, j))],
        out_specs=pl.BlockSpec((tm, tn), lambda i, j: (i, j)),
        compiler_params=pltpu.CompilerParams(dimension_semantics=("parallel", "parallel"),
                                             vmem_limit_bytes=int(12 * tm * tn + 4 * MIB)),
        name="cast_bf16",
    )(w)


def _mods_kernel(c_ref, w_ref, b_ref, o_ref):
    a = _silu(c_ref[...]).astype(BF16)
    o_ref[...] = _dot(a, w_ref[...].astype(BF16)) + b_ref[...]


def _mods(c_all, ada_w, ada_b):
    C, D = c_all.shape
    N = ada_w.shape[1]
    tn = 1024
    return pl.pallas_call(
        _mods_kernel,
        out_shape=jax.ShapeDtypeStruct((C, N), F32),
        grid=(N // tn,),
        in_specs=[pl.BlockSpec((C, D), lambda j: (0, 0)),
                  pl.BlockSpec((D, tn), lambda j: (0, j)),
                  pl.BlockSpec((1, tn), lambda j: (0, j))],
        out_specs=pl.BlockSpec((C, tn), lambda j: (0, j)),
        compiler_params=pltpu.CompilerParams(dimension_semantics=("parallel",), vmem_limit_bytes=40 * MIB),
        name="ada_mods",
    )(c_all, ada_w, ada_b.reshape(1, N))


def _nmm_kernel(*refs, nsb, has_res):
    if has_res:
        b2c_ref, h_ref, res_ref, gate_ref, shift_ref, scale_ref, g_ref, w_ref, o_ref, hnew_ref, xn_ref = refs
    else:
        b2c_ref, h_ref, shift_ref, scale_ref, g_ref, w_ref, o_ref, xn_ref = refs
    i = pl.program_id(0)

    @pl.when(pl.program_id(1) == 0)
    def _():
        def body(s, carry):
            c = b2c_ref[i * nsb + s]
            rows = _row_block(s)
            hv = h_ref[rows, :]
            if has_res:
                hv = hv + gate_ref[c] * _unpack_bf16_pairs(res_ref[rows, :])
                hnew_ref[rows, :] = hv
            xn_ref[rows, :] = _rms_mod(hv, g_ref[...], shift_ref[c], scale_ref[c]).astype(BF16)
            return carry
        lax.fori_loop(0, nsb, body, 0)

    o_ref[...] = _dot(xn_ref[...], w_ref[...]).astype(o_ref.dtype)


def _norm_matmul(b2c, h, g, shift, scale, w, *, out_dtype, tn, tm_want, res=None, gate=None):
    T, D = h.shape
    N = w.shape[1]
    C = shift.shape[0]
    tm = _pick_tile(T, tm_want)
    nsb = tm // ROW_BLOCK
    has_res = res is not None
    row_spec = pl.BlockSpec((tm, D), lambda i, j, b: (i, 0))
    mod_spec = pl.BlockSpec((C, 1, D), lambda i, j, b: (0, 0, 0))
    in_specs = [row_spec]
    args = [h]
    if has_res:
        in_specs += [pl.BlockSpec((tm, D // 2), lambda i, j, b: (i, 0)), mod_spec]
        args += [res, gate]
    in_specs += [mod_spec, mod_spec, pl.BlockSpec((1, D), lambda i, j, b: (0, 0)),
                 pl.BlockSpec((D, tn), lambda i, j, b: (0, j))]
    args += [shift, scale, g.reshape(1, D), w]
    out_shape = [jax.ShapeDtypeStruct((T, N), out_dtype)]
    out_specs = [pl.BlockSpec((tm, tn), lambda i, j, b: (i, j))]
    if has_res:
        out_shape.append(jax.ShapeDtypeStruct((T, D), F32))
        out_specs.append(row_spec)
    n_row_bufs = 3 if has_res else 1
    vmem = (2 * n_row_bufs * tm * D * 4 + tm * D * 2 + 2 * D * tn * 2
            + 2 * tm * tn * jnp.dtype(out_dtype).itemsize + 8 * MIB)
    outs = pl.pallas_call(
        functools.partial(_nmm_kernel, nsb=nsb, has_res=has_res),
        out_shape=out_shape,
        grid_spec=pltpu.PrefetchScalarGridSpec(
            num_scalar_prefetch=1, grid=(T // tm, N // tn), in_specs=in_specs, out_specs=out_specs,
            scratch_shapes=[pltpu.VMEM((tm, D), BF16)]),
        compiler_params=pltpu.CompilerParams(dimension_semantics=("parallel", "arbitrary"),
                                             vmem_limit_bytes=int(vmem)),
        name="norm_matmul_res" if has_res else "norm_matmul",
    )(b2c, *args)
    return outs if has_res else outs[0]


def _mmres_kernel(b2c_ref, a_ref, w_ref, h_ref, gate_ref, o_ref, acc_ref, *, nsb):
    i = pl.program_id(0)
    acc_ref[...] = _dot(a_ref[...], w_ref[...])

    def body(s, carry):
        c = b2c_ref[i * nsb + s]
        rows = _row_block(s)
        o_ref[rows, :] = h_ref[rows, :] + gate_ref[c] * acc_ref[rows, :]
        return carry
    lax.fori_loop(0, nsb, body, 0)


def _matmul_residual(b2c, a, w, h, gate, *, tn, tm_want):
    T, K = a.shape
    D = w.shape[1]
    C = gate.shape[0]
    tm = _pick_tile(T, tm_want)
    nsb = tm // ROW_BLOCK
    vmem = 2 * tm * K * 2 + 2 * K * tn * 2 + 5 * tm * tn * 4 + 8 * MIB
    return pl.pallas_call(
        functools.partial(_mmres_kernel, nsb=nsb),
        out_shape=jax.ShapeDtypeStruct((T, D), F32),
        grid_spec=pltpu.PrefetchScalarGridSpec(
            num_scalar_prefetch=1, grid=(T // tm, D // tn),
            in_specs=[pl.BlockSpec((tm, K), lambda i, j, b: (i, 0)),
                      pl.BlockSpec((K, tn), lambda i, j, b: (0, j)),
                      pl.BlockSpec((tm, tn), lambda i, j, b: (i, j)),
                      pl.BlockSpec((C, 1, tn), lambda i, j, b: (0, 0, j))],
            out_specs=pl.BlockSpec((tm, tn), lambda i, j, b: (i, j)),
            scratch_shapes=[pltpu.VMEM((tm, tn), F32)]),
        compiler_params=pltpu.CompilerParams(dimension_semantics=("parallel", "arbitrary"),
                                             vmem_limit_bytes=int(vmem)),
        name="matmul_residual",
    )(b2c, a, w, h, gate)


def _final_kernel(b2c_ref, h_ref, res_ref, gate_ref, shift_ref, scale_ref, g_ref, o_ref, *, nsb, blk_off):
    i = pl.program_id(0) + blk_off

    def body(s, carry):
        c = b2c_ref[i * nsb + s]
        rows = _row_block(s)
        hv = h_ref[rows, :] + gate_ref[c] * _unpack_bf16_pairs(res_ref[rows, :])
        o_ref[rows, :] = _rms_mod(hv, g_ref[...], shift_ref[c], scale_ref[c])
        return carry
    lax.fori_loop(0, nsb, body, 0)


def _final_norm(b2c, h, res, gate, shift, scale, g, *, row_off, n_rows, tm_want):
    T, D = h.shape
    C = shift.shape[0]
    tm = _pick_tile(int(np.gcd(row_off, n_rows)) if row_off else n_rows, tm_want)
    nsb = tm // ROW_BLOCK
    blk_off = row_off // tm
    row_spec = pl.BlockSpec((tm, D), lambda i, b: (i + blk_off, 0))
    mod_spec = pl.BlockSpec((C, 1, D), lambda i, b: (0, 0, 0))
    return pl.pallas_call(
        functools.partial(_final_kernel, nsb=nsb, blk_off=blk_off),
        out_shape=jax.ShapeDtypeStruct((n_rows, D), F32),
        grid_spec=pltpu.PrefetchScalarGridSpec(
            num_scalar_prefetch=1, grid=(n_rows // tm,),
            in_specs=[row_spec, pl.BlockSpec((tm, D // 2), lambda i, b: (i + blk_off, 0)), mod_spec, mod_spec, mod_spec,
                      pl.BlockSpec((1, D), lambda i, b: (0, 0))],
            out_specs=pl.BlockSpec((tm, D), lambda i, b: (i, 0))),
        compiler_params=pltpu.CompilerParams(dimension_semantics=("parallel",),
                                             vmem_limit_bytes=int(6 * tm * D * 4 + 8 * MIB)),
        name="final_norm",
    )(b2c, h, res, gate, shift, scale, g.reshape(1, D))


def _ssd_kernel(seq_ref, first_ref, last_ref,
                z_ref, x_ref, b_ref, c_ref, dt_ref,
                cpx_ref, cpb_ref, cpc_ref, sprev_ref,
                cwx_ref, cwb_ref, cwc_ref, cbx_ref, cbb_ref, cbc_ref,
                dtb_ref, a_ref, dsk_ref, ng_ref, expand_ref,
                y_ref, cnew_ref, snew_ref,
                extx_ref, extb_ref, extc_ref, st_ref, xs_ref, bm_ref, cm_ref, col_ref, rowa_ref, rowd_ref,
                ybuf_ref, cb_ref, bt_ref, lm_ref):
    Q = ROW_BLOCK
    N = SSM_D_STATE
    HP = x_ref.shape[1]
    GN = b_ref.shape[1]
    G = GN // N
    GW = HP // G
    PAIRS_PER_GROUP = GW // LANES
    CH = 2 * LANES
    k = pl.program_id(0)

    @pl.when(first_ref[k] == 1)
    def _():
        extx_ref[0:CONV_PAD, :] = cpx_ref[0]
        extb_ref[0:CONV_PAD, :] = cpb_ref[0]
        extc_ref[0:CONV_PAD, :] = cpc_ref[0]
        st_ref[...] = sprev_ref[0]

    sr = lax.broadcasted_iota(I32, ((CONV_W - 1) * Q, Q), 0)
    sc_ = lax.broadcasted_iota(I32, ((CONV_W - 1) * Q, Q), 1)
    shift = jnp.where(sc_ == (sr & (Q - 1)) + (sr // Q) - (CONV_W - 1), 1.0, 0.0).astype(BF16)

    def conv(ext_ref, cur_ref, w_ref, bias_ref, out_ref, new_off):
        for lo in range(0, cur_ref.shape[1], CH):
            cs = slice(lo, lo + CH)
            cur_bf = cur_ref[:, cs]
            cur = cur_bf.astype(F32)
            shifted = _dot(shift, cur_bf)
            acc = bias_ref[:, cs] + w_ref[CONV_W - 1:CONV_W, cs] * cur
            for j in range(CONV_W - 1):
                acc = acc + w_ref[j:j + 1, cs] * shifted[j * Q:(j + 1) * Q]
            ext_ref[CONV_PAD:2 * CONV_PAD, cs] = cur[0:CONV_PAD]
            head = bias_ref[:, cs] + w_ref[CONV_W - 1:CONV_W, cs] * cur[0:CONV_PAD]
            for j in range(CONV_W - 1):
                off = CONV_PAD - (CONV_W - 1) + j
                head = head + w_ref[j:j + 1, cs] * ext_ref[off:off + CONV_PAD, cs]
            out_ref[:, cs] = _silu(jnp.concatenate([head, acc[CONV_PAD:]], axis=0)).astype(out_ref.dtype)
            tail = cur[Q - CONV_PAD:Q]
            cnew_ref[0, :, new_off + lo:new_off + lo + CH] = tail
            ext_ref[0:CONV_PAD, cs] = tail

    conv(extx_ref, x_ref, cwx_ref, cbx_ref, xs_ref, 0)
    conv(extb_ref, b_ref, cwb_ref, cbb_ref, bm_ref, HP)
    conv(extc_ref, c_ref, cwc_ref, cbc_ref, cm_ref, HP + GN)

    dt = _softplus(dt_ref[...].astype(F32) + dtb_ref[...])
    da = dt * a_ref[...]
    rr = lax.broadcasted_iota(I32, (Q, Q), 0)
    cc = lax.broadcasted_iota(I32, (Q, Q), 1)
    tri = jnp.where(rr >= cc, 1.0, 0.0).astype(BF16)
    cs = _dot(tri, jnp.concatenate(_split_bf16(da, 3), axis=1))
    acs = cs[:, 0:LANES] + cs[:, LANES:2 * LANES] + cs[:, 2 * LANES:3 * LANES]

    both = _split_bf16(jnp.concatenate([acs, dt], axis=0), 2)
    col_ref[...] = _dot(both[0], expand_ref[...]) + _dot(both[1], expand_ref[...])

    pr = lax.broadcasted_iota(I32, (LANES // 2, 2 * LANES), 0)
    pk = lax.broadcasted_iota(I32, (LANES // 2, 2 * LANES), 1)
    esel = jnp.where(pk == jnp.where(pk < LANES, 2 * pr, 2 * pr + 1 + LANES), 1.0, 0.0).astype(BF16)
    zero = jnp.zeros((Q, LANES), BF16)

    def pair_rows(v, parts):
        out = None
        for piece in _split_bf16(v, parts):
            vbd = jnp.concatenate([jnp.concatenate([piece, zero], axis=1),
                                   jnp.concatenate([zero, piece], axis=1)], axis=0)
            r = _nt_dot(esel, vbd)
            out = r if out is None else out + r
        return out

    rowa_ref[...] = pair_rows(acs, 3)
    rowd_ref[...] = pair_rows(dt, 2)

    lane = lax.broadcasted_iota(I32, (Q, LANES), 1)
    trow = lax.broadcasted_iota(I32, (Q, LANES), 0)
    causal = trow >= (lane & (Q - 1))
    lo_half = lane < Q
    er = lax.broadcasted_iota(I32, (N, N), 0)
    ec = lax.broadcasted_iota(I32, (N, N), 1)
    eye = jnp.where(er == ec, 1.0, 0.0).astype(BF16)

    for g in range(G):
        gs = slice(g * GW, (g + 1) * GW)
        bg = bm_ref[:, g * N:(g + 1) * N]
        cg = cm_ref[:, g * N:(g + 1) * N]
        cb_ref[g] = _nt_dot(cg, jnp.concatenate([bg, bg], axis=0))
        ybuf_ref[:, gs] = _dot(cg, st_ref[:, gs].astype(BF16))
        bt_ref[g] = _nt_dot(eye, bg).astype(BF16)
    for p in range(G * PAIRS_PER_GROUP):
        ps = slice(p * LANES, (p + 1) * LANES)
        seg = col_ref[0:Q, ps] - rowa_ref[p:p + 1, :]
        decay = jnp.exp(jnp.where(causal, seg, -jnp.inf))
        lm_ref[p] = (cb_ref[p // PAIRS_PER_GROUP] * decay * rowd_ref[p:p + 1, :]).astype(BF16)
    for p in range(G * PAIRS_PER_GROUP):
        ps = slice(p * LANES, (p + 1) * LANES)
        xp = xs_ref[:, ps]
        xbd = jnp.concatenate([jnp.where(lo_half, xp, 0.0), jnp.where(lo_half, 0.0, xp)],
                              axis=0).astype(BF16)
        ybuf_ref[:, ps] = (_dot(lm_ref[p], xbd) + ybuf_ref[:, ps] * jnp.exp(col_ref[0:Q, ps])
                           + dsk_ref[:, ps] * xp)
    for g in range(G):
        gs = slice(g * GW, (g + 1) * GW)
        last = col_ref[Q - 1:Q, gs]
        w_end = col_ref[Q:2 * Q, gs] * jnp.exp(last - col_ref[0:Q, gs])
        xw = (xs_ref[:, gs] * w_end).astype(BF16)
        st_ref[:, gs] = st_ref[:, gs] * jnp.exp(last) + _dot(bt_ref[g], xw)

    sq = jnp.zeros((Q, LANES), F32)
    for lo in range(0, HP, CH):
        cs = slice(lo, lo + CH)
        yv = ybuf_ref[:, cs] * _silu(z_ref[:, cs].astype(F32))
        ybuf_ref[:, cs] = yv
        y2 = yv * yv
        for c in range(CH // LANES):
            sq = sq + y2[:, c * LANES:(c + 1) * LANES]
    inv = lax.rsqrt(jnp.sum(sq, axis=-1, keepdims=True) * (1.0 / HP) + EPS)
    for lo in range(0, HP, CH):
        cs = slice(lo, lo + CH)
        y_ref[:, cs] = (ybuf_ref[:, cs] * inv * ng_ref[:, cs]).astype(BF16)

    @pl.when(last_ref[k] == 1)
    def _():
        snew_ref[0] = st_ref[...]


def _ssd(tabs, zx, cprev, sprev, conv_w, conv_b, dt_bias, a_neg, d_skip, norm_g, expand):
    T = zx.shape[0]
    S, N, HP = sprev.shape
    GN = SSM_GROUPS * N
    Q = ROW_BLOCK
    nblk = T // Q
    bb, cb = HP // GN, HP // GN + 1
    zb = (2 * HP) // GN

    def rows(width, col_blk):
        return pl.BlockSpec((Q, width), lambda k, s, f, l: (k, col_blk))

    def per_seq(shape, col_blk):
        return pl.BlockSpec((1,) + shape, lambda k, s, f, l: (s[k], 0, col_blk))

    def const(shape, col_blk=0):
        return pl.BlockSpec(shape, lambda k, s, f, l: (0, col_blk))

    in_specs = [
        rows(HP, 0), rows(HP, 1), rows(GN, zb), rows(GN, zb + 1), rows(LANES, (2 * HP + 2 * GN) // LANES),
        per_seq((CONV_PAD, HP), 0), per_seq((CONV_PAD, GN), bb), per_seq((CONV_PAD, GN), cb),
        per_seq((N, HP), 0),
        const((CONV_W, HP)), const((CONV_W, GN), bb), const((CONV_W, GN), cb),
        const((1, HP)), const((1, GN), bb), const((1, GN), cb),
        const((1, LANES)), const((1, LANES)), const((1, HP)), const((1, HP)), const((LANES, HP)),
    ]
    out_shape = [jax.ShapeDtypeStruct((T, HP), BF16),
                 jax.ShapeDtypeStruct((S, CONV_PAD, HP + 2 * GN), F32),
                 jax.ShapeDtypeStruct((S, N, HP), F32)]
    out_specs = [pl.BlockSpec((Q, HP), lambda k, s, f, l: (k, 0)),
                 pl.BlockSpec((1, CONV_PAD, HP + 2 * GN), lambda k, s, f, l: (s[k], 0, 0)),
                 pl.BlockSpec((1, N, HP), lambda k, s, f, l: (s[k], 0, 0))]
    scratch = [pltpu.VMEM((2 * CONV_PAD, HP), F32), pltpu.VMEM((2 * CONV_PAD, GN), F32),
               pltpu.VMEM((2 * CONV_PAD, GN), F32), pltpu.VMEM((N, HP), F32), pltpu.VMEM((Q, HP), F32),
               pltpu.VMEM((Q, GN), BF16), pltpu.VMEM((Q, GN), BF16), pltpu.VMEM((2 * Q, HP), F32),
               pltpu.VMEM((LANES // 2, 2 * Q), F32), pltpu.VMEM((LANES // 2, 2 * Q), F32),
               pltpu.VMEM((Q, HP), F32), pltpu.VMEM((SSM_GROUPS, Q, 2 * Q), F32),
               pltpu.VMEM((SSM_GROUPS, N, Q), BF16), pltpu.VMEM((HP // LANES, Q, 2 * Q), BF16)]
    return pl.pallas_call(
        _ssd_kernel,
        out_shape=out_shape,
        grid_spec=pltpu.PrefetchScalarGridSpec(num_scalar_prefetch=3, grid=(nblk,), in_specs=in_specs,
                                               out_specs=out_specs, scratch_shapes=scratch),
        compiler_params=pltpu.CompilerParams(dimension_semantics=("arbitrary",), vmem_limit_bytes=48 * MIB),
        name="ssd_mixer",
    )(tabs["seq"], tabs["first"], tabs["last"],
      zx, zx, zx, zx, zx, cprev, cprev, cprev, sprev,
      conv_w, conv_w, conv_w, conv_b, conv_b, conv_b, dt_bias, a_neg, d_skip, norm_g, expand)


def _attn_kernel(wk0_ref, wk1_ref, wc0_ref, wc1_ref, uc0_ref, uc1_ref, ninv_ref,
                 q_ref, k0_ref, k1_ref, k2_ref, c0_ref, c1_ref, bias_ref, sink_ref, o_ref, sc_ref, p_ref):
    Q = ROW_BLOCK
    DH = ATT_HEAD_DIM
    KVW = k2_ref.shape[1] // 2
    n_kv = KVW // DH
    NK = bias_ref.shape[2]
    k = pl.program_id(0)
    s = lax.broadcasted_iota(I32, (Q, NK), 1)
    mask_bias = jnp.where(s >= ninv_ref[k] * Q, 0.0, -jnp.inf)
    kv = jnp.concatenate([jnp.where(uc0_ref[k] == 1, c0_ref[...], k0_ref[...]),
                          jnp.where(uc1_ref[k] == 1, c1_ref[...], k1_ref[...]), k2_ref[...]], axis=0).astype(BF16)

    for g in range(n_kv):
        qg = jnp.concatenate([q_ref[:, (g * ATT_GROUP + r) * DH:(g * ATT_GROUP + r + 1) * DH]
                              for r in range(ATT_GROUP)], axis=0)
        sc_ref[g] = _nt_dot(qg, kv[:, g * DH:(g + 1) * DH])
    sink_terms = []
    for g in range(n_kv):
        for r in range(ATT_GROUP):
            rows = slice(r * Q, (r + 1) * Q)
            sc = sc_ref[g, rows, :] * (DH ** -0.5) + bias_ref[g, rows, :] + mask_bias
            sink = sink_ref[g, rows, 0:1]
            m = jnp.maximum(jnp.max(sc, axis=-1, keepdims=True), sink)
            p_ref[g, rows, :] = jnp.exp(sc - m).astype(BF16)
            sink_terms.append(jnp.exp(sink - m))
    first_half = lax.broadcasted_iota(I32, (Q, 2 * DH), 1) < DH
    ones = jnp.ones((NK, 2 * DH), BF16)
    for g in range(n_kv):
        vg = kv[:, KVW + g * DH:KVW + (g + 1) * DH]
        og = _dot(p_ref[g], jnp.concatenate([vg, vg], axis=1))
        dn = _dot(p_ref[g], ones)
        for rp in range(ATT_GROUP // 2):
            ev, od = slice(2 * rp * Q, (2 * rp + 1) * Q), slice((2 * rp + 1) * Q, (2 * rp + 2) * Q)
            s_ev, s_od = sink_terms[g * ATT_GROUP + 2 * rp], sink_terms[g * ATT_GROUP + 2 * rp + 1]
            pair = jnp.where(first_half, og[ev] * (1.0 / (dn[ev] + s_ev)), og[od] * (1.0 / (dn[od] + s_od)))
            h0 = g * ATT_GROUP + 2 * rp
            o_ref[:, h0 * DH:(h0 + 2) * DH] = pair.astype(BF16)


def _attention(tabs, q, kv, cache_kv, sinks):
    T, HD = q.shape
    Q = ROW_BLOCK
    NK = (WINDOW_BLOCKS + 1) * Q
    n_heads = HD // ATT_HEAD_DIM
    KV2 = kv.shape[1]
    n_kv = KV2 // (2 * ATT_HEAD_DIM)
    GQ = ATT_GROUP * Q
    slopes = 2.0 ** (-8.0 * np.arange(1, n_heads + 1) / n_heads)
    dist = np.abs(np.arange(Q)[:, None] + WINDOW_BLOCKS * Q - np.arange(NK)[None, :])
    bias = jnp.asarray((-slopes[:, None, None] * dist[None]).reshape(n_kv, GQ, NK).astype(np.float32))
    sink_rows = jnp.broadcast_to(jnp.repeat(sinks.reshape(n_kv, ATT_GROUP), Q, axis=1)[:, :, None],
                                 (n_kv, GQ, LANES))

    def blk(pick):
        return pl.BlockSpec((Q, KV2), lambda k, wk0, wk1, wc0, wc1, uc0, uc1, n: (pick(k, wk0, wk1, wc0, wc1), 0))

    def const(shape):
        return pl.BlockSpec(shape, lambda k, *_: (0, 0, 0))

    return pl.pallas_call(
        _attn_kernel,
        out_shape=jax.ShapeDtypeStruct((T, HD), BF16),
        grid_spec=pltpu.PrefetchScalarGridSpec(
            num_scalar_prefetch=7, grid=(T // Q,),
            in_specs=[pl.BlockSpec((Q, HD), lambda k, *_: (k, 0)),
                      blk(lambda k, wk0, wk1, wc0, wc1: wk0[k]), blk(lambda k, wk0, wk1, wc0, wc1: wk1[k]),
                      blk(lambda k, wk0, wk1, wc0, wc1: k),
                      blk(lambda k, wk0, wk1, wc0, wc1: wc0[k]), blk(lambda k, wk0, wk1, wc0, wc1: wc1[k]),
                      const((n_kv, GQ, NK)), const((n_kv, GQ, LANES))],
            out_specs=pl.BlockSpec((Q, HD), lambda k, *_: (k, 0)),
            scratch_shapes=[pltpu.VMEM((n_kv, GQ, NK), F32), pltpu.VMEM((n_kv, GQ, NK), BF16)]),
        compiler_params=pltpu.CompilerParams(dimension_semantics=("parallel",), vmem_limit_bytes=32 * MIB),
        name="swa_attention",
    )(tabs["wk0"], tabs["wk1"], tabs["wc0"], tabs["wc1"], tabs["uc0"], tabs["uc1"], tabs["ninv"],
      q, kv, kv, kv, cache_kv, cache_kv, bias, sink_rows)


def _router_kernel(b2c_ref, h_ref, shift_ref, scale_ref, g_ref, wh_ref, wl_ref, rb_ref, hn_ref, wrow_ref, meta_ref,
                   xh_ref, xl_ref, wt_ref, *, nsb):
    i = pl.program_id(0)

    def body(s, carry):
        c = b2c_ref[i * nsb + s]
        rows = _row_block(s)
        xn = _rms_mod(h_ref[rows, :], g_ref[...], shift_ref[c], scale_ref[c])
        hn_ref[rows, 0:h_ref.shape[1] // 2] = _pack_bf16_pairs(xn)
        hi, lo = _split_bf16(xn, 2)
        xh_ref[rows, :] = hi
        xl_ref[rows, :] = lo
        return carry
    lax.fori_loop(0, nsb, body, 0)

    lt = (_nt_dot(wh_ref[...], xh_ref[...]) + _nt_dot(wh_ref[...], xl_ref[...])
          + _nt_dot(wl_ref[...], xh_ref[...]) + rb_ref[...])

    def first_max(vals):
        m = vals[0]
        for v in vals[1:]:
            m = jnp.maximum(m, v)
        idx = jnp.full(m.shape, len(vals) - 1, I32)
        for j in range(len(vals) - 2, -1, -1):
            idx = jnp.where(vals[j] == m, j, idx)
        return m, idx

    lg = [lt[j:j + 1, :] for j in range(MOE_GROUPS)]
    mg, gi = first_max(lg)
    p_sel = 1.0 / sum(jnp.exp(v - mg) for v in lg)
    le = []
    for j in range(MOE_PER_GROUP):
        v = lt[MOE_GROUPS + (MOE_GROUPS - 1) * MOE_PER_GROUP + j:MOE_GROUPS + (MOE_GROUPS - 1) * MOE_PER_GROUP + j + 1, :]
        for grp in range(MOE_GROUPS - 2, -1, -1):
            row = MOE_GROUPS + grp * MOE_PER_GROUP + j
            v = jnp.where(gi == grp, lt[row:row + 1, :], v)
        le.append(v)
    m1, i1 = first_max(le)
    m2, i2 = first_max([jnp.where(i1 == j, -jnp.inf, le[j]) for j in range(MOE_PER_GROUP)])
    e2 = jnp.exp(m2 - m1)
    w1 = p_sel * (1.0 / (1.0 + e2))
    w2 = p_sel * (e2 / (1.0 + e2))
    lo_i = jnp.minimum(i1, i2)
    hi_i = jnp.maximum(i1, i2)
    first_is_lo = i1 < i2
    pair_base = jnp.where(lo_i == 0, 0, jnp.where(lo_i == 1, 3, 5))
    bucket = gi * MOE_PAIRS + pair_base + hi_i - lo_i - 1
    meta_ref[0:1, :] = bucket.astype(F32)
    meta_ref[1:8, :] = jnp.zeros((7, meta_ref.shape[1]), F32)
    wt_ref[...] = jnp.zeros(wt_ref.shape, F32)
    wt_ref[0:1, :] = jnp.where(first_is_lo, w1, w2)
    wt_ref[1:2, :] = jnp.where(first_is_lo, w2, w1)
    wrow_ref[...] = wt_ref[...].T


def _router(b2c, h, g, shift, scale, w_hi, w_lo, rbias, *, tm_want):
    T, D = h.shape
    C = shift.shape[0]
    tm = _pick_tile(T, tm_want)
    nsb = tm // ROW_BLOCK
    row_spec = pl.BlockSpec((tm, D), lambda i, b: (i, 0))
    mod_spec = pl.BlockSpec((C, 1, D), lambda i, b: (0, 0, 0))
    rb = jnp.broadcast_to(rbias.reshape(LANES, 1), (LANES, tm))
    return pl.pallas_call(
        functools.partial(_router_kernel, nsb=nsb),
        out_shape=[jax.ShapeDtypeStruct((T, D // 2), I32), jax.ShapeDtypeStruct((T, LANES), F32),
                   jax.ShapeDtypeStruct((8, T), F32)],
        grid_spec=pltpu.PrefetchScalarGridSpec(
            num_scalar_prefetch=1, grid=(T // tm,),
            in_specs=[row_spec, mod_spec, mod_spec, pl.BlockSpec((1, D), lambda i, b: (0, 0)),
                      pl.BlockSpec((LANES, D), lambda i, b: (0, 0)), pl.BlockSpec((LANES, D), lambda i, b: (0, 0)),
                      pl.BlockSpec((LANES, tm), lambda i, b: (0, 0))],
            out_specs=[pl.BlockSpec((tm, D // 2), lambda i, b: (i, 0)), pl.BlockSpec((tm, LANES), lambda i, b: (i, 0)),
                       pl.BlockSpec((8, tm), lambda i, b: (0, i))],
            scratch_shapes=[pltpu.VMEM((tm, D), BF16), pltpu.VMEM((tm, D), BF16), pltpu.VMEM((LANES, tm), F32)]),
        compiler_params=pltpu.CompilerParams(dimension_semantics=("parallel",),
                                             vmem_limit_bytes=int(5 * tm * D * 4 + 12 * MIB)),
        name="moe_router",
    )(b2c, h, shift, scale, g.reshape(1, D), w_hi, w_lo, rb)


def _gather_rows(src, idx):
    n = idx.shape[0]
    D = src.shape[1]
    n_workers = SC_CORES * SC_SUBCORES
    per_worker = n // n_workers
    limit = min(SC_MAX_INDICES, SC_CHUNK_BYTES // (D * src.dtype.itemsize))
    rows = max(r for r in range(8, limit + 1, 8) if per_worker % r == 0)
    n_chunks = per_worker // rows
    assert n == n_workers * n_chunks * rows, (n, n_workers, rows)
    mesh = plsc.VectorSubcoreMesh(core_axis_name="c", subcore_axis_name="s", num_cores=SC_CORES,
                                  num_subcores=SC_SUBCORES)

    @functools.partial(
        pl.kernel, mesh=mesh, out_type=jax.ShapeDtypeStruct((n, D), src.dtype),
        scratch_types=[pltpu.VMEM((rows,), I32), pltpu.VMEM((rows, D), src.dtype), pltpu.SemaphoreType.DMA],
        name="sc_row_gather")
    def gather(src_hbm, idx_hbm, out_hbm, idx_v, rows_v, sem):
        base = (lax.axis_index("s") * SC_CORES + lax.axis_index("c")) * per_worker

        @pl.loop(0, n_chunks)
        def _(j):
            off = base + j * rows
            pltpu.sync_copy(idx_hbm.at[pl.ds(off, rows)], idx_v)
            pltpu.async_copy(src_hbm.at[idx_v], rows_v, sem).wait()
            pltpu.sync_copy(rows_v, out_hbm.at[pl.ds(off, rows)])

    return gather(src, idx)


def _ffn_kernel(ea_ref, eb_ref, nv_ref, x_ref, ws_ref, wga_ref, wua_ref, wda_ref, wgb_ref, wub_ref, wdb_ref, y_ref):
    i = pl.program_id(0)
    DW = y_ref.shape[1]

    @pl.when(nv_ref[i] > 0)
    def _():
        x = _unpack_bf16_pairs(x_ref[...]).astype(BF16)

        def expert(wg_ref, wu_ref, wd_ref):
            hid = _silu(_dot(x, wg_ref[0])) * _dot(x, wu_ref[0])
            return _dot(hid.astype(BF16), wd_ref[0])
        wa = ws_ref[:, 0:1]
        wb = ws_ref[:, 1:2]
        y_ref[...] = _pack_bf16_pairs(wa * expert(wga_ref, wua_ref, wda_ref) + wb * expert(wgb_ref, wub_ref, wdb_ref))

    @pl.when(nv_ref[i] == 0)
    def _():
        y_ref[...] = jnp.zeros(y_ref.shape, I32)


def _moe_ffn(plan, xs, ws, w_g, w_u, w_d):
    R, DX = xs.shape
    tm = MOE_TILE
    E, F, D = w_d.shape
    up_a = pl.BlockSpec((1, D, F), lambda i, a, b, n: (a[i], 0, 0))
    up_b = pl.BlockSpec((1, D, F), lambda i, a, b, n: (b[i], 0, 0))
    return pl.pallas_call(
        _ffn_kernel,
        out_shape=jax.ShapeDtypeStruct((R, D // 2), I32),
        grid_spec=pltpu.PrefetchScalarGridSpec(
            num_scalar_prefetch=3, grid=(R // tm,),
            in_specs=[pl.BlockSpec((tm, DX), lambda i, a, b, n: (i, 0)),
                      pl.BlockSpec((tm, LANES), lambda i, a, b, n: (i, 0)),
                      up_a, up_a, pl.BlockSpec((1, F, D), lambda i, a, b, n: (a[i], 0, 0)),
                      up_b, up_b, pl.BlockSpec((1, F, D), lambda i, a, b, n: (b[i], 0, 0))],
            out_specs=pl.BlockSpec((tm, D // 2), lambda i, a, b, n: (i, 0))),
        compiler_params=pltpu.CompilerParams(dimension_semantics=("arbitrary",), vmem_limit_bytes=48 * MIB),
        name="moe_experts",
    )(plan["ea"], plan["eb"], plan["nvalid"], xs, ws, w_g, w_u, w_d, w_g, w_u, w_d)


def _moe_plan(meta, T, expert_off):
    tm = MOE_TILE
    n_tiles = -(-T // tm) + MOE_BUCKETS
    bucket = meta[0].astype(I32)
    onehot = (bucket[:, None] == jnp.arange(MOE_BUCKETS, dtype=I32)[None, :]).astype(F32)
    blocks = onehot.reshape(T // LANES, LANES, MOE_BUCKETS)
    within = jnp.einsum("ij,bjk->bik", jnp.tril(jnp.ones((LANES, LANES), F32)), blocks)
    totals = within[:, -1, :]
    n_blk = T // LANES
    before = jnp.einsum("ij,jk->ik", jnp.tril(jnp.ones((n_blk, n_blk), F32), -1), totals,
                        precision=lax.Precision.HIGHEST)
    cum = (within + before[:, None, :]).reshape(T, MOE_BUCKETS)
    rank = jnp.sum(cum * onehot, axis=1).astype(I32) - 1
    counts = jnp.sum(totals, axis=0).astype(I32)
    ntile_b = (counts + tm - 1) // tm
    tend_b = jnp.cumsum(ntile_b)
    tstart_b = tend_b - ntile_b
    dest = tstart_b[bucket] * tm + rank
    tok_of_pos = (jnp.arange(n_tiles * tm, dtype=I32) % T).at[dest].set(jnp.arange(T, dtype=I32))
    tiles = jnp.arange(n_tiles, dtype=I32)
    tile_b = jnp.minimum(jnp.sum((tiles[:, None] >= tend_b[None, :]).astype(I32), axis=1), MOE_BUCKETS - 1)
    nvalid = jnp.where(tiles < tend_b[-1],
                       jnp.clip(counts[tile_b] - (tiles - tstart_b[tile_b]) * tm, 0, tm), 0).astype(I32)
    grp = tile_b // MOE_PAIRS
    pair = tile_b % MOE_PAIRS
    ea = expert_off + grp * MOE_PER_GROUP + jnp.asarray(MOE_PAIR_LO, I32)[pair]
    eb = expert_off + grp * MOE_PER_GROUP + jnp.asarray(MOE_PAIR_HI, I32)[pair]
    return dict(dest=dest, tok_of_pos=tok_of_pos, nvalid=nvalid, ea=ea.astype(I32), eb=eb.astype(I32))


def _hmoe(b2c, h, g, shift, scale, router, w_g, w_u, w_d, expert_off):
    T = h.shape[0]
    hn, wrows, meta = _router(b2c, h, g, shift, scale, *router, tm_want=512)
    plan = _moe_plan(meta, T, expert_off)
    xs = _gather_rows(hn, plan["tok_of_pos"])
    ws = _gather_rows(wrows, plan["tok_of_pos"])
    ys = _moe_ffn(plan, xs, ws, w_g, w_u, w_d)
    return _gather_rows(ys, plan["dest"])


def _tables(Bp, Lp, Bs, Ls):
    nbp, nbs = Lp // ROW_BLOCK, Ls // ROW_BLOCK
    cols = dict(seq=[], first=[], last=[], ninv=[], wk0=[], wk1=[], wc0=[], wc1=[], uc0=[], uc1=[])
    k = 0
    for b in range(Bp + Bs):
        nb = nbp if b < Bp else nbs
        for c in range(nb):
            cols["seq"].append(b)
            cols["first"].append(int(c == 0))
            cols["last"].append(int(c == nb - 1))
            cols["ninv"].append(max(0, WINDOW_BLOCKS - c) if b < Bp else 0)
            for j in range(WINDOW_BLOCKS):
                hist = c - WINDOW_BLOCKS + j
                from_cache = b >= Bp and hist < 0
                cols[f"uc{j}"].append(int(from_cache))
                cols[f"wc{j}"].append((b - Bp) * WINDOW_BLOCKS + WINDOW_BLOCKS + hist if from_cache else 0)
                cols[f"wk{j}"].append(k if from_cache else max(k - WINDOW_BLOCKS + j, 0))
            k += 1
    return {name: jnp.asarray(np.asarray(v, np.int32)) for name, v in cols.items()}


def _router_weights(router_g, bias_g, router_e, bias_e):
    D = router_g.shape[0]
    w = jnp.zeros((LANES, D), F32)
    w = w.at[:MOE_GROUPS].set(router_g.T).at[MOE_GROUPS:MOE_GROUPS + router_e.shape[1]].set(router_e.T)
    w_hi = w.astype(BF16)
    w_lo = (w - w_hi.astype(F32)).astype(BF16)
    rb = jnp.zeros((LANES,), F32).at[:MOE_GROUPS].set(bias_g).at[MOE_GROUPS:MOE_GROUPS + bias_e.shape[0]].set(bias_e)
    return w_hi, w_lo, rb


def kernel(x_prompt, x_sample, state_conv, state_ssm, cache_k, cache_v, c_prompt, c_sample, ada_w, ada_b, norm_mix, norm_ffn, norm_kv, norm_out, ssm_w_in, ssm_conv_w, ssm_conv_b, ssm_dt_bias, ssm_a_log, ssm_d, ssm_norm, ssm_w_out, attn_w_kv, attn_w_q, attn_sinks, attn_w_o, moe_router_g, moe_bias_g, moe_router_e, moe_bias_e, moe_w_gate, moe_w_up, moe_w_down):
    Bp, Lp, D = x_prompt.shape
    Bs, Ls, _ = x_sample.shape
    Tp, Ts = Bp * Lp, Bs * Ls
    T = Tp + Ts
    C = Bp + Bs
    tabs = _tables(Bp, Lp, Bs, Ls)
    b2c = tabs["seq"]

    n_mod = ada_w.shape[1] // D
    mods = _mods(jnp.concatenate([c_prompt, c_sample], axis=0), ada_w, ada_b).reshape(C, n_mod, D)
    mod = lambda i: mods[:, i:i + 1, :]
    x = jnp.concatenate([x_prompt.reshape(Tp, D), x_sample.reshape(Ts, D)], axis=0)

    HP = ssm_w_out.shape[1]
    H = HP // SSM_HEAD_DIM
    N = SSM_D_STATE
    GN = SSM_GROUPS * N
    conv_dim = HP + 2 * GN
    in_cols = 2 * HP + 2 * GN + LANES
    w_in = _to_bf16(ssm_w_in[0], tn=in_cols // 9, out_cols=in_cols)
    zx = _norm_matmul(b2c, x, norm_mix[0], mod(0), mod(1), w_in, out_dtype=BF16, tn=in_cols // 9, tm_want=1024)
    cprev = jnp.zeros((C, CONV_PAD, conv_dim), F32).at[Bp:, CONV_PAD - (CONV_W - 1):].set(state_conv[0])
    sprev = jnp.concatenate([jnp.zeros((Bp, N, HP), F32),
                             jnp.transpose(state_ssm[0], (0, 3, 1, 2)).reshape(Bs, N, HP)], axis=0)
    pad_h = lambda v: jnp.pad(v.astype(F32), (0, LANES - H)).reshape(1, LANES)
    expand = (jnp.arange(LANES)[:, None] == (jnp.arange(HP) // SSM_HEAD_DIM)[None, :]).astype(BF16)
    y_ssd, cnew, snew = _ssd(
        tabs, zx, cprev, sprev, ssm_conv_w[0], ssm_conv_b[0].reshape(1, conv_dim), pad_h(ssm_dt_bias[0]),
        pad_h(-jnp.exp(ssm_a_log[0].astype(F32))), jnp.repeat(ssm_d[0].astype(F32), SSM_HEAD_DIM).reshape(1, HP),
        ssm_norm[0].reshape(1, HP), expand)
    h = _matmul_residual(b2c, y_ssd, _to_bf16(ssm_w_out[0]), x, mod(2), tn=512, tm_want=1024)

    n_exp, _, moe_ff = moe_w_gate.shape[1:]
    w_gate = _to_bf16(moe_w_gate).reshape(-1, D, moe_ff)
    w_up = _to_bf16(moe_w_up).reshape(-1, D, moe_ff)
    w_down = _to_bf16(moe_w_down).reshape(-1, moe_ff, D)

    def experts(layer):
        return w_gate, w_up, w_down, layer * n_exp

    def router(layer):
        return _router_weights(moe_router_g[layer], moe_bias_g[layer], moe_router_e[layer], moe_bias_e[layer])

    moe0 = _hmoe(b2c, h, norm_ffn[0], mod(3), mod(4), router(0), *experts(0))

    kv, h = _norm_matmul(b2c, h, norm_kv, mod(12), mod(13), _to_bf16(attn_w_kv), out_dtype=F32,
                         tn=attn_w_kv.shape[1], tm_want=512, res=moe0, gate=mod(5))
    KV2 = kv.shape[1]
    KVW = KV2 // 2
    W = WINDOW_BLOCKS * ROW_BLOCK
    cache_kv = jnp.concatenate([cache_k.reshape(Bs, W, KVW), cache_v.reshape(Bs, W, KVW)], axis=-1)
    kvp = kv[:Tp].reshape(Bp, Lp, KV2)
    kvs = jnp.concatenate([cache_kv, kv[Tp:].reshape(Bs, Ls, KV2)], axis=1)
    q = _norm_matmul(b2c, h, norm_mix[1], mod(6), mod(7), _to_bf16(attn_w_q[0]), out_dtype=BF16,
                     tn=1024, tm_want=1024)
    o = _attention(tabs, q, kv, cache_kv.reshape(Bs * W, KV2), attn_sinks[0].astype(F32))
    h = _matmul_residual(b2c, o, _to_bf16(attn_w_o[0]), h, mod(8), tn=1024, tm_want=1024)
    moe1 = _hmoe(b2c, h, norm_ffn[1], mod(9), mod(10), router(1), *experts(1))

    fin = functools.partial(_final_norm, b2c, h, moe1, mod(11), mod(14), mod(15), norm_out, tm_want=512)
    y_prompt = fin(row_off=0, n_rows=Tp).reshape(Bp, Lp, D)
    y_sample = fin(row_off=Tp, n_rows=Ts).reshape(Bs, Ls, D)

    kv_heads = KVW // ATT_HEAD_DIM
    tail = lambda a, lo: a[:, -W:, lo:lo + KVW].reshape(a.shape[0], W, kv_heads, ATT_HEAD_DIM)
    conv_tail = cnew[:, CONV_PAD - (CONV_W - 1):]
    ssm_new = jnp.transpose(snew.reshape(C, N, H, SSM_HEAD_DIM), (0, 2, 3, 1))
    return (y_prompt, y_sample, conv_tail[None, :Bp], ssm_new[None, :Bp], tail(kvp, 0), tail(kvp, KVW),
            conv_tail[None, Bp:], ssm_new[None, Bp:], tail(kvs, 0), tail(kvs, KVW))
```

```python
import functools

import numpy as np
import jax
import jax.numpy as jnp
from jax import lax
from jax.experimental import pallas as pl
from jax.experimental.pallas import tpu as pltpu
from jax.experimental.pallas import tpu_sc as plsc

F32 = jnp.float32
BF16 = jnp.bfloat16
I32 = jnp.int32
EPS = 1e-6
ROW_BLOCK = 64
WINDOW_BLOCKS = 2
LANES = 128
MIB = 1024 * 1024

SSM_HEAD_DIM = 64
SSM_GROUPS = 8
SSM_D_STATE = 128
CONV_W = 4
CONV_PAD = 8
ATT_HEAD_DIM = 64
ATT_GROUP = 8
MOE_GROUPS = 4
MOE_PER_GROUP = 4
MOE_PAIR_LO = (0, 0, 0, 1, 1, 2)
MOE_PAIR_HI = (1, 2, 3, 2, 3, 3)
MOE_PAIRS = len(MOE_PAIR_LO)
MOE_BUCKETS = MOE_GROUPS * MOE_PAIRS
MOE_TILE = 256
SC_CORES = 2
SC_SUBCORES = 16
SC_MAX_INDICES = 128
SC_CHUNK_BYTES = 80 * 1024


def _pick_tile(n, want):
    t = min(want, n)
    t -= t % ROW_BLOCK
    while n % t:
        t -= ROW_BLOCK
    return t


def _silu(x):
    return (0.5 * x) * (1.0 + jnp.tanh(0.5 * x))


def _softplus(x):
    return jnp.maximum(x, 0.0) + jnp.log1p(jnp.exp(-jnp.abs(x)))


def _rms_mod(hv, g, shift, scale):
    ms = jnp.mean(hv * hv, axis=-1, keepdims=True)
    return (hv * lax.rsqrt(ms + EPS) * g) * (1.0 + scale) + shift


def _dot(a, b):
    return jnp.dot(a, b, preferred_element_type=F32)


def _nt_dot(a, b):
    return lax.dot_general(a, b, (((1,), (1,)), ((), ())), preferred_element_type=F32)


def _split_bf16(x, parts):
    out = []
    r = x
    for _ in range(parts):
        p = r.astype(BF16)
        out.append(p)
        r = r - p.astype(F32)
    return out


def _pack_bf16_pairs(x):
    c = x.shape[1] // 2
    xb = x.astype(BF16).astype(F32)
    hi = lax.bitcast_convert_type(xb[:, :c], I32)
    lo = lax.bitcast_convert_type(xb[:, c:], I32)
    return hi | lax.shift_right_logical(lo, jnp.int32(16))


def _unpack_bf16_pairs(w):
    hi = lax.bitcast_convert_type(w & jnp.int32(-65536), F32)
    lo = lax.bitcast_convert_type(lax.shift_left(w, jnp.int32(16)), F32)
    return jnp.concatenate([hi, lo], axis=1)


def _row_block(s):
    return pl.ds(pl.multiple_of(s * ROW_BLOCK, ROW_BLOCK), ROW_BLOCK)


def _cast_kernel(x_ref, o_ref, *, valid_cols):
    x = x_ref[...]
    if valid_cols is not None:
        col = pl.program_id(1) * x.shape[1] + lax.broadcasted_iota(I32, x.shape, 1)
        x = jnp.where(col < valid_cols, x, 0.0)
    o_ref[...] = x.astype(BF16)


def _to_bf16(w, *, tn=None, out_cols=None):
    w = w.reshape(-1, w.shape[-1])
    R, cols = w.shape
    out_cols = out_cols or cols
    tn = tn or out_cols
    tm = _pick_tile(R, max(ROW_BLOCK, (8 * MIB) // (tn * 4)))
    return pl.pallas_call(
        functools.partial(_cast_kernel, valid_cols=cols if out_cols != cols else None),
        out_shape=jax.ShapeDtypeStruct((R, out_cols), BF16),
        grid=(R // tm, out_cols // tn),
        in_specs=[pl.BlockSpec((tm, tn), lambda i, j: (i, j))],
        out_specs=pl.BlockSpec((tm, tn), lambda i, j: (i, j)),
        compiler_params=pltpu.CompilerParams(dimension_semantics=("parallel", "parallel"),
                                             vmem_limit_bytes=int(12 * tm * tn + 4 * MIB)),
        name="cast_bf16",
    )(w)


def _mods_kernel(c_ref, w_ref, b_ref, o_ref):
    a = _silu(c_ref[...]).astype(BF16)
    o_ref[...] = _dot(a, w_ref[...].astype(BF16)) + b_ref[...]


def _mods(c_all, ada_w, ada_b):
    C, D = c_all.shape
    N = ada_w.shape[1]
    tn = 1024
    return pl.pallas_call(
        _mods_kernel,
        out_shape=jax.ShapeDtypeStruct((C, N), F32),
        grid=(N // tn,),
        in_specs=[pl.BlockSpec((C, D), lambda j: (0, 0)),
                  pl.BlockSpec((D, tn), lambda j: (0, j)),
                  pl.BlockSpec((1, tn), lambda j: (0, j))],
        out_specs=pl.BlockSpec((C, tn), lambda j: (0, j)),
        compiler_params=pltpu.CompilerParams(dimension_semantics=("parallel",), vmem_limit_bytes=40 * MIB),
        name="ada_mods",
    )(c_all, ada_w, ada_b.reshape(1, N))


def _nmm_kernel(*refs, nsb, has_res):
    if has_res:
        b2c_ref, h_ref, res_ref, gate_ref, shift_ref, scale_ref, g_ref, w_ref, o_ref, hnew_ref, xn_ref = refs
    else:
        b2c_ref, h_ref, shift_ref, scale_ref, g_ref, w_ref, o_ref, xn_ref = refs
    i = pl.program_id(0)

    @pl.when(pl.program_id(1) == 0)
    def _():
        def body(s, carry):
            c = b2c_ref[i * nsb + s]
            rows = _row_block(s)
            hv = h_ref[rows, :]
            if has_res:
                hv = hv + gate_ref[c] * _unpack_bf16_pairs(res_ref[rows, :])
                hnew_ref[rows, :] = hv
            xn_ref[rows, :] = _rms_mod(hv, g_ref[...], shift_ref[c], scale_ref[c]).astype(BF16)
            return carry
        lax.fori_loop(0, nsb, body, 0)

    o_ref[...] = _dot(xn_ref[...], w_ref[...]).astype(o_ref.dtype)


def _norm_matmul(b2c, h, g, shift, scale, w, *, out_dtype, tn, tm_want, res=None, gate=None):
    T, D = h.shape
    N = w.shape[1]
    C = shift.shape[0]
    tm = _pick_tile(T, tm_want)
    nsb = tm // ROW_BLOCK
    has_res = res is not None
    row_spec = pl.BlockSpec((tm, D), lambda i, j, b: (i, 0))
    mod_spec = pl.BlockSpec((C, 1, D), lambda i, j, b: (0, 0, 0))
    in_specs = [row_spec]
    args = [h]
    if has_res:
        in_specs += [pl.BlockSpec((tm, D // 2), lambda i, j, b: (i, 0)), mod_spec]
        args += [res, gate]
    in_specs += [mod_spec, mod_spec, pl.BlockSpec((1, D), lambda i, j, b: (0, 0)),
                 pl.BlockSpec((D, tn), lambda i, j, b: (0, j))]
    args += [shift, scale, g.reshape(1, D), w]
    out_shape = [jax.ShapeDtypeStruct((T, N), out_dtype)]
    out_specs = [pl.BlockSpec((tm, tn), lambda i, j, b: (i, j))]
    if has_res:
        out_shape.append(jax.ShapeDtypeStruct((T, D), F32))
        out_specs.append(row_spec)
    n_row_bufs = 3 if has_res else 1
    vmem = (2 * n_row_bufs * tm * D * 4 + tm * D * 2 + 2 * D * tn * 2
            + 2 * tm * tn * jnp.dtype(out_dtype).itemsize + 8 * MIB)
    outs = pl.pallas_call(
        functools.partial(_nmm_kernel, nsb=nsb, has_res=has_res),
        out_shape=out_shape,
        grid_spec=pltpu.PrefetchScalarGridSpec(
            num_scalar_prefetch=1, grid=(T // tm, N // tn), in_specs=in_specs, out_specs=out_specs,
            scratch_shapes=[pltpu.VMEM((tm, D), BF16)]),
        compiler_params=pltpu.CompilerParams(dimension_semantics=("parallel", "arbitrary"),
                                             vmem_limit_bytes=int(vmem)),
        name="norm_matmul_res" if has_res else "norm_matmul",
    )(b2c, *args)
    return outs if has_res else outs[0]


def _mmres_kernel(b2c_ref, a_ref, w_ref, h_ref, gate_ref, o_ref, acc_ref, *, nsb):
    i = pl.program_id(0)
    acc_ref[...] = _dot(a_ref[...], w_ref[...])

    def body(s, carry):
        c = b2c_ref[i * nsb + s]
        rows = _row_block(s)
        o_ref[rows, :] = h_ref[rows, :] + gate_ref[c] * acc_ref[rows, :]
        return carry
    lax.fori_loop(0, nsb, body, 0)


def _matmul_residual(b2c, a, w, h, gate, *, tn, tm_want):
    T, K = a.shape
    D = w.shape[1]
    C = gate.shape[0]
    tm = _pick_tile(T, tm_want)
    nsb = tm // ROW_BLOCK
    vmem = 2 * tm * K * 2 + 2 * K * tn * 2 + 5 * tm * tn * 4 + 8 * MIB
    return pl.pallas_call(
        functools.partial(_mmres_kernel, nsb=nsb),
        out_shape=jax.ShapeDtypeStruct((T, D), F32),
        grid_spec=pltpu.PrefetchScalarGridSpec(
            num_scalar_prefetch=1, grid=(T // tm, D // tn),
            in_specs=[pl.BlockSpec((tm, K), lambda i, j, b: (i, 0)),
                      pl.BlockSpec((K, tn), lambda i, j, b: (0, j)),
                      pl.BlockSpec((tm, tn), lambda i, j, b: (i, j)),
                      pl.BlockSpec((C, 1, tn), lambda i, j, b: (0, 0, j))],
            out_specs=pl.BlockSpec((tm, tn), lambda i, j, b: (i, j)),
            scratch_shapes=[pltpu.VMEM((tm, tn), F32)]),
        compiler_params=pltpu.CompilerParams(dimension_semantics=("parallel", "arbitrary"),
                                             vmem_limit_bytes=int(vmem)),
        name="matmul_residual",
    )(b2c, a, w, h, gate)


def _final_kernel(b2c_ref, h_ref, res_ref, gate_ref, shift_ref, scale_ref, g_ref, o_ref, *, nsb, blk_off):
    i = pl.program_id(0) + blk_off

    def body(s, carry):
        c = b2c_ref[i * nsb + s]
        rows = _row_block(s)
        hv = h_ref[rows, :] + gate_ref[c] * _unpack_bf16_pairs(res_ref[rows, :])
        o_ref[rows, :] = _rms_mod(hv, g_ref[...], shift_ref[c], scale_ref[c])
        return carry
    lax.fori_loop(0, nsb, body, 0)


def _final_norm(b2c, h, res, gate, shift, scale, g, *, row_off, n_rows, tm_want):
    T, D = h.shape
    C = shift.shape[0]
    tm = _pick_tile(int(np.gcd(row_off, n_rows)) if row_off else n_rows, tm_want)
    nsb = tm // ROW_BLOCK
    blk_off = row_off // tm
    row_spec = pl.BlockSpec((tm, D), lambda i, b: (i + blk_off, 0))
    mod_spec = pl.BlockSpec((C, 1, D), lambda i, b: (0, 0, 0))
    return pl.pallas_call(
        functools.partial(_final_kernel, nsb=nsb, blk_off=blk_off),
        out_shape=jax.ShapeDtypeStruct((n_rows, D), F32),
        grid_spec=pltpu.PrefetchScalarGridSpec(
            num_scalar_prefetch=1, grid=(n_rows // tm,),
            in_specs=[row_spec, pl.BlockSpec((tm, D // 2), lambda i, b: (i + blk_off, 0)), mod_spec, mod_spec, mod_spec,
                      pl.BlockSpec((1, D), lambda i, b: (0, 0))],
            out_specs=pl.BlockSpec((tm, D), lambda i, b: (i, 0))),
        compiler_params=pltpu.CompilerParams(dimension_semantics=("parallel",),
                                             vmem_limit_bytes=int(6 * tm * D * 4 + 8 * MIB)),
        name="final_norm",
    )(b2c, h, res, gate, shift, scale, g.reshape(1, D))


def _ssd_kernel(seq_ref, first_ref, last_ref,
                z_ref, x_ref, b_ref, c_ref, dt_ref,
                cpx_ref, cpb_ref, cpc_ref, sprev_ref,
                cwx_ref, cwb_ref, cwc_ref, cbx_ref, cbb_ref, cbc_ref,
                dtb_ref, a_ref, dsk_ref, ng_ref, expand_ref,
                y_ref, cnew_ref, snew_ref,
                extx_ref, extb_ref, extc_ref, st_ref, xs_ref, bm_ref, cm_ref, col_ref, rowa_ref, rowd_ref,
                ybuf_ref, cb_ref, bt_ref, lm_ref):
    Q = ROW_BLOCK
    N = SSM_D_STATE
    HP = x_ref.shape[1]
    GN = b_ref.shape[1]
    G = GN // N
    GW = HP // G
    PAIRS_PER_GROUP = GW // LANES
    CH = 2 * LANES
    k = pl.program_id(0)

    @pl.when(first_ref[k] == 1)
    def _():
        extx_ref[0:CONV_PAD, :] = cpx_ref[0]
        extb_ref[0:CONV_PAD, :] = cpb_ref[0]
        extc_ref[0:CONV_PAD, :] = cpc_ref[0]
        st_ref[...] = sprev_ref[0]

    sr = lax.broadcasted_iota(I32, ((CONV_W - 1) * Q, Q), 0)
    sc_ = lax.broadcasted_iota(I32, ((CONV_W - 1) * Q, Q), 1)
    shift = jnp.where(sc_ == (sr & (Q - 1)) + (sr // Q) - (CONV_W - 1), 1.0, 0.0).astype(BF16)

    def conv(ext_ref, cur_ref, w_ref, bias_ref, out_ref, new_off):
        for lo in range(0, cur_ref.shape[1], CH):
            cs = slice(lo, lo + CH)
            cur_bf = cur_ref[:, cs]
            cur = cur_bf.astype(F32)
            shifted = _dot(shift, cur_bf)
            acc = bias_ref[:, cs] + w_ref[CONV_W - 1:CONV_W, cs] * cur
            for j in range(CONV_W - 1):
                acc = acc + w_ref[j:j + 1, cs] * shifted[j * Q:(j + 1) * Q]
            ext_ref[CONV_PAD:2 * CONV_PAD, cs] = cur[0:CONV_PAD]
            head = bias_ref[:, cs] + w_ref[CONV_W - 1:CONV_W, cs] * cur[0:CONV_PAD]
            for j in range(CONV_W - 1):
                off = CONV_PAD - (CONV_W - 1) + j
                head = head + w_ref[j:j + 1, cs] * ext_ref[off:off + CONV_PAD, cs]
            out_ref[:, cs] = _silu(jnp.concatenate([head, acc[CONV_PAD:]], axis=0)).astype(out_ref.dtype)
            tail = cur[Q - CONV_PAD:Q]
            cnew_ref[0, :, new_off + lo:new_off + lo + CH] = tail
            ext_ref[0:CONV_PAD, cs] = tail

    conv(extx_ref, x_ref, cwx_ref, cbx_ref, xs_ref, 0)
    conv(extb_ref, b_ref, cwb_ref, cbb_ref, bm_ref, HP)
    conv(extc_ref, c_ref, cwc_ref, cbc_ref, cm_ref, HP + GN)

    dt = _softplus(dt_ref[...].astype(F32) + dtb_ref[...])
    da = dt * a_ref[...]
    rr = lax.broadcasted_iota(I32, (Q, Q), 0)
    cc = lax.broadcasted_iota(I32, (Q, Q), 1)
    tri = jnp.where(rr >= cc, 1.0, 0.0).astype(BF16)
    cs = _dot(tri, jnp.concatenate(_split_bf16(da, 3), axis=1))
    acs = cs[:, 0:LANES] + cs[:, LANES:2 * LANES] + cs[:, 2 * LANES:3 * LANES]

    both = _split_bf16(jnp.concatenate([acs, dt], axis=0), 2)
    col_ref[...] = _dot(both[0], expand_ref[...]) + _dot(both[1], expand_ref[...])

    pr = lax.broadcasted_iota(I32, (LANES // 2, 2 * LANES), 0)
    pk = lax.broadcasted_iota(I32, (LANES // 2, 2 * LANES), 1)
    esel = jnp.where(pk == jnp.where(pk < LANES, 2 * pr, 2 * pr + 1 + LANES), 1.0, 0.0).astype(BF16)
    zero = jnp.zeros((Q, LANES), BF16)

    def pair_rows(v, parts):
        out = None
        for piece in _split_bf16(v, parts):
            vbd = jnp.concatenate([jnp.concatenate([piece, zero], axis=1),
                                   jnp.concatenate([zero, piece], axis=1)], axis=0)
            r = _nt_dot(esel, vbd)
            out = r if out is None else out + r
        return out

    rowa_ref[...] = pair_rows(acs, 3)
    rowd_ref[...] = pair_rows(dt, 2)

    lane = lax.broadcasted_iota(I32, (Q, LANES), 1)
    trow = lax.broadcasted_iota(I32, (Q, LANES), 0)
    causal = trow >= (lane & (Q - 1))
    lo_half = lane < Q
    er = lax.broadcasted_iota(I32, (N, N), 0)
    ec = lax.broadcasted_iota(I32, (N, N), 1)
    eye = jnp.where(er == ec, 1.0, 0.0).astype(BF16)

    for g in range(G):
        gs = slice(g * GW, (g + 1) * GW)
        bg = bm_ref[:, g * N:(g + 1) * N]
        cg = cm_ref[:, g * N:(g + 1) * N]
        cb_ref[g] = _nt_dot(cg, jnp.concatenate([bg, bg], axis=0))
        ybuf_ref[:, gs] = _dot(cg, st_ref[:, gs].astype(BF16))
        bt_ref[g] = _nt_dot(eye, bg).astype(BF16)
    for p in range(G * PAIRS_PER_GROUP):
        ps = slice(p * LANES, (p + 1) * LANES)
        seg = col_ref[0:Q, ps] - rowa_ref[p:p + 1, :]
        decay = jnp.exp(jnp.where(causal, seg, -jnp.inf))
        lm_ref[p] = (cb_ref[p // PAIRS_PER_GROUP] * decay * rowd_ref[p:p + 1, :]).astype(BF16)
    for p in range(G * PAIRS_PER_GROUP):
        ps = slice(p * LANES, (p + 1) * LANES)
        xp = xs_ref[:, ps]
        xbd = jnp.concatenate([jnp.where(lo_half, xp, 0.0), jnp.where(lo_half, 0.0, xp)],
                              axis=0).astype(BF16)
        ybuf_ref[:, ps] = (_dot(lm_ref[p], xbd) + ybuf_ref[:, ps] * jnp.exp(col_ref[0:Q, ps])
                           + dsk_ref[:, ps] * xp)
    for g in range(G):
        gs = slice(g * GW, (g + 1) * GW)
        last = col_ref[Q - 1:Q, gs]
        w_end = col_ref[Q:2 * Q, gs] * jnp.exp(last - col_ref[0:Q, gs])
        xw = (xs_ref[:, gs] * w_end).astype(BF16)
        st_ref[:, gs] = st_ref[:, gs] * jnp.exp(last) + _dot(bt_ref[g], xw)

    sq = jnp.zeros((Q, LANES), F32)
    for lo in range(0, HP, CH):
        cs = slice(lo, lo + CH)
        yv = ybuf_ref[:, cs] * _silu(z_ref[:, cs].astype(F32))
        ybuf_ref[:, cs] = yv
        y2 = yv * yv
        for c in range(CH // LANES):
            sq = sq + y2[:, c * LANES:(c + 1) * LANES]
    inv = lax.rsqrt(jnp.sum(sq, axis=-1, keepdims=True) * (1.0 / HP) + EPS)
    for lo in range(0, HP, CH):
        cs = slice(lo, lo + CH)
        y_ref[:, cs] = (ybuf_ref[:, cs] * inv * ng_ref[:, cs]).astype(BF16)

    @pl.when(last_ref[k] == 1)
    def _():
        snew_ref[0] = st_ref[...]


def _ssd(tabs, zx, cprev, sprev, conv_w, conv_b, dt_bias, a_neg, d_skip, norm_g, expand):
    T = zx.shape[0]
    S, N, HP = sprev.shape
    GN = SSM_GROUPS * N
    Q = ROW_BLOCK
    nblk = T // Q
    bb, cb = HP // GN, HP // GN + 1
    zb = (2 * HP) // GN

    def rows(width, col_blk):
        return pl.BlockSpec((Q, width), lambda k, s, f, l: (k, col_blk))

    def per_seq(shape, col_blk):
        return pl.BlockSpec((1,) + shape, lambda k, s, f, l: (s[k], 0, col_blk))

    def const(shape, col_blk=0):
        return pl.BlockSpec(shape, lambda k, s, f, l: (0, col_blk))

    in_specs = [
        rows(HP, 0), rows(HP, 1), rows(GN, zb), rows(GN, zb + 1), rows(LANES, (2 * HP + 2 * GN) // LANES),
        per_seq((CONV_PAD, HP), 0), per_seq((CONV_PAD, GN), bb), per_seq((CONV_PAD, GN), cb),
        per_seq((N, HP), 0),
        const((CONV_W, HP)), const((CONV_W, GN), bb), const((CONV_W, GN), cb),
        const((1, HP)), const((1, GN), bb), const((1, GN), cb),
        const((1, LANES)), const((1, LANES)), const((1, HP)), const((1, HP)), const((LANES, HP)),
    ]
    out_shape = [jax.ShapeDtypeStruct((T, HP), BF16),
                 jax.ShapeDtypeStruct((S, CONV_PAD, HP + 2 * GN), F32),
                 jax.ShapeDtypeStruct((S, N, HP), F32)]
    out_specs = [pl.BlockSpec((Q, HP), lambda k, s, f, l: (k, 0)),
                 pl.BlockSpec((1, CONV_PAD, HP + 2 * GN), lambda k, s, f, l: (s[k], 0, 0)),
                 pl.BlockSpec((1, N, HP), lambda k, s, f, l: (s[k], 0, 0))]
    scratch = [pltpu.VMEM((2 * CONV_PAD, HP), F32), pltpu.VMEM((2 * CONV_PAD, GN), F32),
               pltpu.VMEM((2 * CONV_PAD, GN), F32), pltpu.VMEM((N, HP), F32), pltpu.VMEM((Q, HP), F32),
               pltpu.VMEM((Q, GN), BF16), pltpu.VMEM((Q, GN), BF16), pltpu.VMEM((2 * Q, HP), F32),
               pltpu.VMEM((LANES // 2, 2 * Q), F32), pltpu.VMEM((LANES // 2, 2 * Q), F32),
               pltpu.VMEM((Q, HP), F32), pltpu.VMEM((SSM_GROUPS, Q, 2 * Q), F32),
               pltpu.VMEM((SSM_GROUPS, N, Q), BF16), pltpu.VMEM((HP // LANES, Q, 2 * Q), BF16)]
    return pl.pallas_call(
        _ssd_kernel,
        out_shape=out_shape,
        grid_spec=pltpu.PrefetchScalarGridSpec(num_scalar_prefetch=3, grid=(nblk,), in_specs=in_specs,
                                               out_specs=out_specs, scratch_shapes=scratch),
        compiler_params=pltpu.CompilerParams(dimension_semantics=("arbitrary",), vmem_limit_bytes=48 * MIB),
        name="ssd_mixer",
    )(tabs["seq"], tabs["first"], tabs["last"],
      zx, zx, zx, zx, zx, cprev, cprev, cprev, sprev,
      conv_w, conv_w, conv_w, conv_b, conv_b, conv_b, dt_bias, a_neg, d_skip, norm_g, expand)


def _attn_kernel(wk0_ref, wk1_ref, wc0_ref, wc1_ref, uc0_ref, uc1_ref, ninv_ref,
                 q_ref, k0_ref, k1_ref, k2_ref, c0_ref, c1_ref, bias_ref, sink_ref, o_ref, sc_ref, p_ref):
    Q = ROW_BLOCK
    DH = ATT_HEAD_DIM
    KVW = k2_ref.shape[1] // 2
    n_kv = KVW // DH
    NK = bias_ref.shape[2]
    k = pl.program_id(0)
    s = lax.broadcasted_iota(I32, (Q, NK), 1)
    mask_bias = jnp.where(s >= ninv_ref[k] * Q, 0.0, -jnp.inf)
    kv = jnp.concatenate([jnp.where(uc0_ref[k] == 1, c0_ref[...], k0_ref[...]),
                          jnp.where(uc1_ref[k] == 1, c1_ref[...], k1_ref[...]), k2_ref[...]], axis=0).astype(BF16)

    for g in range(n_kv):
        qg = jnp.concatenate([q_ref[:, (g * ATT_GROUP + r) * DH:(g * ATT_GROUP + r + 1) * DH]
                              for r in range(ATT_GROUP)], axis=0)
        sc_ref[g] = _nt_dot(qg, kv[:, g * DH:(g + 1) * DH])
    sink_terms = []
    for g in range(n_kv):
        for r in range(ATT_GROUP):
            rows = slice(r * Q, (r + 1) * Q)
            sc = sc_ref[g, rows, :] * (DH ** -0.5) + bias_ref[g, rows, :] + mask_bias
            sink = sink_ref[g, rows, 0:1]
            m = jnp.maximum(jnp.max(sc, axis=-1, keepdims=True), sink)
            p_ref[g, rows, :] = jnp.exp(sc - m).astype(BF16)
            sink_terms.append(jnp.exp(sink - m))
    first_half = lax.broadcasted_iota(I32, (Q, 2 * DH), 1) < DH
    ones = jnp.ones((NK, 2 * DH), BF16)
    for g in range(n_kv):
        vg = kv[:, KVW + g * DH:KVW + (g + 1) * DH]
        og = _dot(p_ref[g], jnp.concatenate([vg, vg], axis=1))
        dn = _dot(p_ref[g], ones)
        for rp in range(ATT_GROUP // 2):
            ev, od = slice(2 * rp * Q, (2 * rp + 1) * Q), slice((2 * rp + 1) * Q, (2 * rp + 2) * Q)
            s_ev, s_od = sink_terms[g * ATT_GROUP + 2 * rp], sink_terms[g * ATT_GROUP + 2 * rp + 1]
            pair = jnp.where(first_half, og[ev] * (1.0 / (dn[ev] + s_ev)), og[od] * (1.0 / (dn[od] + s_od)))
            h0 = g * ATT_GROUP + 2 * rp
            o_ref[:, h0 * DH:(h0 + 2) * DH] = pair.astype(BF16)


def _attention(tabs, q, kv, cache_kv, sinks):
    T, HD = q.shape
    Q = ROW_BLOCK
    NK = (WINDOW_BLOCKS + 1) * Q
    n_heads = HD // ATT_HEAD_DIM
    KV2 = kv.shape[1]
    n_kv = KV2 // (2 * ATT_HEAD_DIM)
    GQ = ATT_GROUP * Q
    slopes = 2.0 ** (-8.0 * np.arange(1, n_heads + 1) / n_heads)
    dist = np.abs(np.arange(Q)[:, None] + WINDOW_BLOCKS * Q - np.arange(NK)[None, :])
    bias = jnp.asarray((-slopes[:, None, None] * dist[None]).reshape(n_kv, GQ, NK).astype(np.float32))
    sink_rows = jnp.broadcast_to(jnp.repeat(sinks.reshape(n_kv, ATT_GROUP), Q, axis=1)[:, :, None],
                                 (n_kv, GQ, LANES))

    def blk(pick):
        return pl.BlockSpec((Q, KV2), lambda k, wk0, wk1, wc0, wc1, uc0, uc1, n: (pick(k, wk0, wk1, wc0, wc1), 0))

    def const(shape):
        return pl.BlockSpec(shape, lambda k, *_: (0, 0, 0))

    return pl.pallas_call(
        _attn_kernel,
        out_shape=jax.ShapeDtypeStruct((T, HD), BF16),
        grid_spec=pltpu.PrefetchScalarGridSpec(
            num_scalar_prefetch=7, grid=(T // Q,),
            in_specs=[pl.BlockSpec((Q, HD), lambda k, *_: (k, 0)),
                      blk(lambda k, wk0, wk1, wc0, wc1: wk0[k]), blk(lambda k, wk0, wk1, wc0, wc1: wk1[k]),
                      blk(lambda k, wk0, wk1, wc0, wc1: k),
                      blk(lambda k, wk0, wk1, wc0, wc1: wc0[k]), blk(lambda k, wk0, wk1, wc0, wc1: wc1[k]),
                      const((n_kv, GQ, NK)), const((n_kv, GQ, LANES))],
            out_specs=pl.BlockSpec((Q, HD), lambda k, *_: (k, 0)),
            scratch_shapes=[pltpu.VMEM((n_kv, GQ, NK), F32), pltpu.VMEM((n_kv, GQ, NK), BF16)]),
        compiler_params=pltpu.CompilerParams(dimension_semantics=("parallel",), vmem_limit_bytes=32 * MIB),
        name="swa_attention",
    )(tabs["wk0"], tabs["wk1"], tabs["wc0"], tabs["wc1"], tabs["uc0"], tabs["uc1"], tabs["ninv"],
      q, kv, kv, kv, cache_kv, cache_kv, bias, sink_rows)


def _router_kernel(b2c_ref, h_ref, shift_ref, scale_ref, g_ref, wh_ref, wl_ref, rb_ref, hn_ref, wrow_ref, meta_ref,
                   xh_ref, xl_ref, wt_ref, *, nsb):
    i = pl.program_id(0)

    def body(s, carry):
        c = b2c_ref[i * nsb + s]
        rows = _row_block(s)
        xn = _rms_mod(h_ref[rows, :], g_ref[...], shift_ref[c], scale_ref[c])
        hn_ref[rows, 0:h_ref.shape[1] // 2] = _pack_bf16_pairs(xn)
        hi, lo = _split_bf16(xn, 2)
        xh_ref[rows, :] = hi
        xl_ref[rows, :] = lo
        return carry
    lax.fori_loop(0, nsb, body, 0)

    lt = (_nt_dot(wh_ref[...], xh_ref[...]) + _nt_dot(wh_ref[...], xl_ref[...])
          + _nt_dot(wl_ref[...], xh_ref[...]) + rb_ref[...])

    def first_max(vals):
        m = vals[0]
        for v in vals[1:]:
            m = jnp.maximum(m, v)
        idx = jnp.full(m.shape, len(vals) - 1, I32)
        for j in range(len(vals) - 2, -1, -1):
            idx = jnp.where(vals[j] == m, j, idx)
        return m, idx

    lg = [lt[j:j + 1, :] for j in range(MOE_GROUPS)]
    mg, gi = first_max(lg)
    p_sel = 1.0 / sum(jnp.exp(v - mg) for v in lg)
    le = []
    for j in range(MOE_PER_GROUP):
        v = lt[MOE_GROUPS + (MOE_GROUPS - 1) * MOE_PER_GROUP + j:MOE_GROUPS + (MOE_GROUPS - 1) * MOE_PER_GROUP + j + 1, :]
        for grp in range(MOE_GROUPS - 2, -1, -1):
            row = MOE_GROUPS + grp * MOE_PER_GROUP + j
            v = jnp.where(gi == grp, lt[row:row + 1, :], v)
        le.append(v)
    m1, i1 = first_max(le)
    m2, i2 = first_max([jnp.where(i1 == j, -jnp.inf, le[j]) for j in range(MOE_PER_GROUP)])
    e2 = jnp.exp(m2 - m1)
    w1 = p_sel * (1.0 / (1.0 + e2))
    w2 = p_sel * (e2 / (1.0 + e2))
    lo_i = jnp.minimum(i1, i2)
    hi_i = jnp.maximum(i1, i2)
    first_is_lo = i1 < i2
    pair_base = jnp.where(lo_i == 0, 0, jnp.where(lo_i == 1, 3, 5))
    bucket = gi * MOE_PAIRS + pair_base + hi_i - lo_i - 1
    meta_ref[0:1, :] = bucket.astype(F32)
    meta_ref[1:8, :] = jnp.zeros((7, meta_ref.shape[1]), F32)
    wt_ref[...] = jnp.zeros(wt_ref.shape, F32)
    wt_ref[0:1, :] = jnp.where(first_is_lo, w1, w2)
    wt_ref[1:2, :] = jnp.where(first_is_lo, w2, w1)
    wrow_ref[...] = wt_ref[...].T


def _router(b2c, h, g, shift, scale, w_hi, w_lo, rbias, *, tm_want):
    T, D = h.shape
    C = shift.shape[0]
    tm = _pick_tile(T, tm_want)
    nsb = tm // ROW_BLOCK
    row_spec = pl.BlockSpec((tm, D), lambda i, b: (i, 0))
    mod_spec = pl.BlockSpec((C, 1, D), lambda i, b: (0, 0, 0))
    rb = jnp.broadcast_to(rbias.reshape(LANES, 1), (LANES, tm))
    return pl.pallas_call(
        functools.partial(_router_kernel, nsb=nsb),
        out_shape=[jax.ShapeDtypeStruct((T, D // 2), I32), jax.ShapeDtypeStruct((T, LANES), F32),
                   jax.ShapeDtypeStruct((8, T), F32)],
        grid_spec=pltpu.PrefetchScalarGridSpec(
            num_scalar_prefetch=1, grid=(T // tm,),
            in_specs=[row_spec, mod_spec, mod_spec, pl.BlockSpec((1, D), lambda i, b: (0, 0)),
                      pl.BlockSpec((LANES, D), lambda i, b: (0, 0)), pl.BlockSpec((LANES, D), lambda i, b: (0, 0)),
                      pl.BlockSpec((LANES, tm), lambda i, b: (0, 0))],
            out_specs=[pl.BlockSpec((tm, D // 2), lambda i, b: (i, 0)), pl.BlockSpec((tm, LANES), lambda i, b: (i, 0)),
                       pl.BlockSpec((8, tm), lambda i, b: (0, i))],
            scratch_shapes=[pltpu.VMEM((tm, D), BF16), pltpu.VMEM((tm, D), BF16), pltpu.VMEM((LANES, tm), F32)]),
        compiler_params=pltpu.CompilerParams(dimension_semantics=("parallel",),
                                             vmem_limit_bytes=int(5 * tm * D * 4 + 12 * MIB)),
        name="moe_router",
    )(b2c, h, shift, scale, g.reshape(1, D), w_hi, w_lo, rb)


def _gather_rows(src, idx):
    n = idx.shape[0]
    D = src.shape[1]
    n_workers = SC_CORES * SC_SUBCORES
    per_worker = n // n_workers
    limit = min(SC_MAX_INDICES, SC_CHUNK_BYTES // (D * src.dtype.itemsize))
    rows = max(r for r in range(8, limit + 1, 8) if per_worker % r == 0)
    n_chunks = per_worker // rows
    assert n == n_workers * n_chunks * rows, (n, n_workers, rows)
    mesh = plsc.VectorSubcoreMesh(core_axis_name="c", subcore_axis_name="s", num_cores=SC_CORES,
                                  num_subcores=SC_SUBCORES)

    @functools.partial(
        pl.kernel, mesh=mesh, out_type=jax.ShapeDtypeStruct((n, D), src.dtype),
        scratch_types=[pltpu.VMEM((rows,), I32), pltpu.VMEM((rows, D), src.dtype), pltpu.SemaphoreType.DMA],
        name="sc_row_gather")
    def gather(src_hbm, idx_hbm, out_hbm, idx_v, rows_v, sem):
        base = (lax.axis_index("s") * SC_CORES + lax.axis_index("c")) * per_worker

        @pl.loop(0, n_chunks)
        def _(j):
            off = base + j * rows
            pltpu.sync_copy(idx_hbm.at[pl.ds(off, rows)], idx_v)
            pltpu.async_copy(src_hbm.at[idx_v], rows_v, sem).wait()
            pltpu.sync_copy(rows_v, out_hbm.at[pl.ds(off, rows)])

    return gather(src, idx)


def _scatter_rows(src, idx, n_out):
    n, D = src.shape
    n_workers = SC_CORES * SC_SUBCORES
    per_worker = n // n_workers
    limit = min(SC_MAX_INDICES, SC_CHUNK_BYTES // (D * src.dtype.itemsize))
    rows = max(r for r in range(8, limit + 1, 8) if per_worker % r == 0)
    n_chunks = per_worker // rows
    assert n == n_workers * n_chunks * rows, (n, n_workers, rows)
    mesh = plsc.VectorSubcoreMesh(core_axis_name="c", subcore_axis_name="s", num_cores=SC_CORES,
                                  num_subcores=SC_SUBCORES)

    @functools.partial(
        pl.kernel, mesh=mesh, out_type=jax.ShapeDtypeStruct((n_out, D), src.dtype),
        scratch_types=[pltpu.VMEM((rows,), I32), pltpu.VMEM((rows, D), src.dtype)],
        name="sc_row_scatter")
    def scatter(src_hbm, idx_hbm, out_hbm, idx_v, rows_v):
        base = (lax.axis_index("s") * SC_CORES + lax.axis_index("c")) * per_worker

        @pl.loop(0, n_chunks)
        def _(j):
            off = base + j * rows
            pltpu.sync_copy(idx_hbm.at[pl.ds(off, rows)], idx_v)
            pltpu.sync_copy(src_hbm.at[pl.ds(off, rows)], rows_v)
            pltpu.sync_copy(rows_v, out_hbm.at[idx_v])

    return scatter(src, idx)


def _ffn_kernel(ea_ref, eb_ref, nv_ref, x_ref, ws_ref, wga_ref, wua_ref, wda_ref, wgb_ref, wub_ref, wdb_ref, y_ref):
    i = pl.program_id(0)
    DW = y_ref.shape[1]

    @pl.when(nv_ref[i] > 0)
    def _():
        x = _unpack_bf16_pairs(x_ref[...]).astype(BF16)

        def expert(wg_ref, wu_ref, wd_ref):
            hid = _silu(_dot(x, wg_ref[0])) * _dot(x, wu_ref[0])
            return _dot(hid.astype(BF16), wd_ref[0])
        wa = ws_ref[:, 0:1]
        wb = ws_ref[:, 1:2]
        y_ref[...] = _pack_bf16_pairs(wa * expert(wga_ref, wua_ref, wda_ref) + wb * expert(wgb_ref, wub_ref, wdb_ref))

    @pl.when(nv_ref[i] == 0)
    def _():
        y_ref[...] = jnp.zeros(y_ref.shape, I32)


def _moe_ffn(plan, xs, ws, w_g, w_u, w_d):
    R, DX = xs.shape
    tm = MOE_TILE
    E, F, D = w_d.shape
    up_a = pl.BlockSpec((1, D, F), lambda i, a, b, n: (a[i], 0, 0))
    up_b = pl.BlockSpec((1, D, F), lambda i, a, b, n: (b[i], 0, 0))
    return pl.pallas_call(
        _ffn_kernel,
        out_shape=jax.ShapeDtypeStruct((R, D // 2), I32),
        grid_spec=pltpu.PrefetchScalarGridSpec(
            num_scalar_prefetch=3, grid=(R // tm,),
            in_specs=[pl.BlockSpec((tm, DX), lambda i, a, b, n: (i, 0)),
                      pl.BlockSpec((tm, LANES), lambda i, a, b, n: (i, 0)),
                      up_a, up_a, pl.BlockSpec((1, F, D), lambda i, a, b, n: (a[i], 0, 0)),
                      up_b, up_b, pl.BlockSpec((1, F, D), lambda i, a, b, n: (b[i], 0, 0))],
            out_specs=pl.BlockSpec((tm, D // 2), lambda i, a, b, n: (i, 0))),
        compiler_params=pltpu.CompilerParams(dimension_semantics=("arbitrary",), vmem_limit_bytes=48 * MIB),
        name="moe_experts",
    )(plan["ea"], plan["eb"], plan["nvalid"], xs, ws, w_g, w_u, w_d, w_g, w_u, w_d)


def _moe_plan(meta, T, expert_off):
    tm = MOE_TILE
    n_tiles = -(-T // tm) + MOE_BUCKETS
    bucket = meta[0].astype(I32)
    onehot = (bucket[:, None] == jnp.arange(MOE_BUCKETS, dtype=I32)[None, :]).astype(F32)
    blocks = onehot.reshape(T // LANES, LANES, MOE_BUCKETS)
    within = jnp.einsum("ij,bjk->bik", jnp.tril(jnp.ones((LANES, LANES), F32)), blocks)
    totals = within[:, -1, :]
    n_blk = T // LANES
    before = jnp.einsum("ij,jk->ik", jnp.tril(jnp.ones((n_blk, n_blk), F32), -1), totals,
                        precision=lax.Precision.HIGHEST)
    cum = (within + before[:, None, :]).reshape(T, MOE_BUCKETS)
    rank = jnp.sum(cum * onehot, axis=1).astype(I32) - 1
    counts = jnp.sum(totals, axis=0).astype(I32)
    ntile_b = (counts + tm - 1) // tm
    tend_b = jnp.cumsum(ntile_b)
    tstart_b = tend_b - ntile_b
    dest = tstart_b[bucket] * tm + rank
    tok_of_pos = (jnp.arange(n_tiles * tm, dtype=I32) % T).at[dest].set(jnp.arange(T, dtype=I32))
    tiles = jnp.arange(n_tiles, dtype=I32)
    tile_b = jnp.minimum(jnp.sum((tiles[:, None] >= tend_b[None, :]).astype(I32), axis=1), MOE_BUCKETS - 1)
    nvalid = jnp.where(tiles < tend_b[-1],
                       jnp.clip(counts[tile_b] - (tiles - tstart_b[tile_b]) * tm, 0, tm), 0).astype(I32)
    grp = tile_b // MOE_PAIRS
    pair = tile_b % MOE_PAIRS
    ea = expert_off + grp * MOE_PER_GROUP + jnp.asarray(MOE_PAIR_LO, I32)[pair]
    eb = expert_off + grp * MOE_PER_GROUP + jnp.asarray(MOE_PAIR_HI, I32)[pair]
    return dict(dest=dest, tok_of_pos=tok_of_pos, nvalid=nvalid, ea=ea.astype(I32), eb=eb.astype(I32))


def _hmoe(b2c, h, g, shift, scale, router, w_g, w_u, w_d, expert_off):
    T = h.shape[0]
    hn, wrows, meta = _router(b2c, h, g, shift, scale, *router, tm_want=512)
    plan = _moe_plan(meta, T, expert_off)
    n_sorted = plan["tok_of_pos"].shape[0]
    xs = _scatter_rows(hn, plan["dest"], n_sorted)
    ws = _scatter_rows(wrows, plan["dest"], n_sorted)
    ys = _moe_ffn(plan, xs, ws, w_g, w_u, w_d)
    return _gather_rows(ys, plan["dest"])


def _tables(Bp, Lp, Bs, Ls):
    nbp, nbs = Lp // ROW_BLOCK, Ls // ROW_BLOCK
    cols = dict(seq=[], first=[], last=[], ninv=[], wk0=[], wk1=[], wc0=[], wc1=[], uc0=[], uc1=[])
    k = 0
    for b in range(Bp + Bs):
        nb = nbp if b < Bp else nbs
        for c in range(nb):
            cols["seq"].append(b)
            cols["first"].append(int(c == 0))
            cols["last"].append(int(c == nb - 1))
            cols["ninv"].append(max(0, WINDOW_BLOCKS - c) if b < Bp else 0)
            for j in range(WINDOW_BLOCKS):
                hist = c - WINDOW_BLOCKS + j
                from_cache = b >= Bp and hist < 0
                cols[f"uc{j}"].append(int(from_cache))
                cols[f"wc{j}"].append((b - Bp) * WINDOW_BLOCKS + WINDOW_BLOCKS + hist if from_cache else 0)
                cols[f"wk{j}"].append(k if from_cache else max(k - WINDOW_BLOCKS + j, 0))
            k += 1
    return {name: jnp.asarray(np.asarray(v, np.int32)) for name, v in cols.items()}


def _router_weights(router_g, bias_g, router_e, bias_e):
    D = router_g.shape[0]
    w = jnp.zeros((LANES, D), F32)
    w = w.at[:MOE_GROUPS].set(router_g.T).at[MOE_GROUPS:MOE_GROUPS + router_e.shape[1]].set(router_e.T)
    w_hi = w.astype(BF16)
    w_lo = (w - w_hi.astype(F32)).astype(BF16)
    rb = jnp.zeros((LANES,), F32).at[:MOE_GROUPS].set(bias_g).at[MOE_GROUPS:MOE_GROUPS + bias_e.shape[0]].set(bias_e)
    return w_hi, w_lo, rb


def kernel(x_prompt, x_sample, state_conv, state_ssm, cache_k, cache_v, c_prompt, c_sample, ada_w, ada_b, norm_mix, norm_ffn, norm_kv, norm_out, ssm_w_in, ssm_conv_w, ssm_conv_b, ssm_dt_bias, ssm_a_log, ssm_d, ssm_norm, ssm_w_out, attn_w_kv, attn_w_q, attn_sinks, attn_w_o, moe_router_g, moe_bias_g, moe_router_e, moe_bias_e, moe_w_gate, moe_w_up, moe_w_down):
    Bp, Lp, D = x_prompt.shape
    Bs, Ls, _ = x_sample.shape
    Tp, Ts = Bp * Lp, Bs * Ls
    T = Tp + Ts
    C = Bp + Bs
    tabs = _tables(Bp, Lp, Bs, Ls)
    b2c = tabs["seq"]

    n_mod = ada_w.shape[1] // D
    mods = _mods(jnp.concatenate([c_prompt, c_sample], axis=0), ada_w, ada_b).reshape(C, n_mod, D)
    mod = lambda i: mods[:, i:i + 1, :]
    x = jnp.concatenate([x_prompt.reshape(Tp, D), x_sample.reshape(Ts, D)], axis=0)

    HP = ssm_w_out.shape[1]
    H = HP // SSM_HEAD_DIM
    N = SSM_D_STATE
    GN = SSM_GROUPS * N
    conv_dim = HP + 2 * GN
    in_cols = 2 * HP + 2 * GN + LANES
    w_in = _to_bf16(ssm_w_in[0], tn=in_cols // 9, out_cols=in_cols)
    zx = _norm_matmul(b2c, x, norm_mix[0], mod(0), mod(1), w_in, out_dtype=BF16, tn=in_cols // 9, tm_want=1024)
    cprev = jnp.zeros((C, CONV_PAD, conv_dim), F32).at[Bp:, CONV_PAD - (CONV_W - 1):].set(state_conv[0])
    sprev = jnp.concatenate([jnp.zeros((Bp, N, HP), F32),
                             jnp.transpose(state_ssm[0], (0, 3, 1, 2)).reshape(Bs, N, HP)], axis=0)
    pad_h = lambda v: jnp.pad(v.astype(F32), (0, LANES - H)).reshape(1, LANES)
    expand = (jnp.arange(LANES)[:, None] == (jnp.arange(HP) // SSM_HEAD_DIM)[None, :]).astype(BF16)
    y_ssd, cnew, snew = _ssd(
        tabs, zx, cprev, sprev, ssm_conv_w[0], ssm_conv_b[0].reshape(1, conv_dim), pad_h(ssm_dt_bias[0]),
        pad_h(-jnp.exp(ssm_a_log[0].astype(F32))), jnp.repeat(ssm_d[0].astype(F32), SSM_HEAD_DIM).reshape(1, HP),
        ssm_norm[0].reshape(1, HP), expand)
    h = _matmul_residual(b2c, y_ssd, _to_bf16(ssm_w_out[0]), x, mod(2), tn=512, tm_want=1024)

    n_exp, _, moe_ff = moe_w_gate.shape[1:]
    w_gate = _to_bf16(moe_w_gate).reshape(-1, D, moe_ff)
    w_up = _to_bf16(moe_w_up).reshape(-1, D, moe_ff)
    w_down = _to_bf16(moe_w_down).reshape(-1, moe_ff, D)

    def experts(layer):
        return w_gate, w_up, w_down, layer * n_exp

    def router(layer):
        return _router_weights(moe_router_g[layer], moe_bias_g[layer], moe_router_e[layer], moe_bias_e[layer])

    moe0 = _hmoe(b2c, h, norm_ffn[0], mod(3), mod(4), router(0), *experts(0))

    kv, h = _norm_matmul(b2c, h, norm_kv, mod(12), mod(13), _to_bf16(attn_w_kv), out_dtype=F32,
                         tn=attn_w_kv.shape[1], tm_want=512, res=moe0, gate=mod(5))
    KV2 = kv.shape[1]
    KVW = KV2 // 2
    W = WINDOW_BLOCKS * ROW_BLOCK
    cache_kv = jnp.concatenate([cache_k.reshape(Bs, W, KVW), cache_v.reshape(Bs, W, KVW)], axis=-1)
    kvp = kv[:Tp].reshape(Bp, Lp, KV2)
    kvs = jnp.concatenate([cache_kv, kv[Tp:].reshape(Bs, Ls, KV2)], axis=1)
    q = _norm_matmul(b2c, h, norm_mix[1], mod(6), mod(7), _to_bf16(attn_w_q[0]), out_dtype=BF16,
                     tn=1024, tm_want=1024)
    o = _attention(tabs, q, kv, cache_kv.reshape(Bs * W, KV2), attn_sinks[0].astype(F32))
    h = _matmul_residual(b2c, o, _to_bf16(attn_w_o[0]), h, mod(8), tn=1024, tm_want=1024)
    moe1 = _hmoe(b2c, h, norm_ffn[1], mod(9), mod(10), router(1), *experts(1))

    fin = functools.partial(_final_norm, b2c, h, moe1, mod(11), mod(14), mod(15), norm_out, tm_want=512)
    y_prompt = fin(row_off=0, n_rows=Tp).reshape(Bp, Lp, D)
    y_sample = fin(row_off=Tp, n_rows=Ts).reshape(Bs, Ls, D)

    kv_heads = KVW // ATT_HEAD_DIM
    tail = lambda a, lo: a[:, -W:, lo:lo + KVW].reshape(a.shape[0], W, kv_heads, ATT_HEAD_DIM)
    conv_tail = cnew[:, CONV_PAD - (CONV_W - 1):]
    ssm_new = jnp.transpose(snew.reshape(C, N, H, SSM_HEAD_DIM), (0, 2, 3, 1))
    return (y_prompt, y_sample, conv_tail[None, :Bp], ssm_new[None, :Bp], tail(kvp, 0), tail(kvp, KVW),
            conv_tail[None, Bp:], ssm_new[None, Bp:], tail(kvs, 0), tail(kvs, KVW))
```
